```python
import jax, jax.numpy as jnp
from jax import lax
import numpy as np

D_MODEL = 1024
BATCH = 2
SEQ = 8192
DEPTH = 2
DEC_BATCH = 32
DEC_SEQ = 1
PAST_LEN = 16384
PAGE_SIZE = 128

CHUNK = 128
D_A = 256
G_A = 4
D_B = 256
W_B = 3
D_C = 256
W_C = 31
N_HEADS = 8
N_KV_HEADS = 2
HEAD_DIM = 64
N_IDX_HEADS = 4
IDX_DIM = 64
TOPK_MAX = 256
Q_BLOCK = 128
ROPE_THETA = 500000.0
ROPE_FRAC = 4
D_FF = 2816
N_EXPERTS = 8
TOP_K = 2
D_FF_EXPERT = 3584
N_BRANCHES = 4
LN_EPS = 1e-5
ALPHA = (2 * DEPTH) ** 0.25
BETA = (8 * DEPTH) ** -0.25
SPLIT_WIDTHS = (D_A, D_A, D_B, D_B, D_B, D_C, D_C, N_HEADS * HEAD_DIM, N_KV_HEADS * HEAD_DIM,
                N_KV_HEADS * HEAD_DIM, N_IDX_HEADS * IDX_DIM, IDX_DIM, N_IDX_HEADS, N_BRANCHES * D_MODEL)
D_IN = sum(SPLIT_WIDTHS)

kernel_name = "hybrid_gated_branch_decoder_step"


def layer_norm(x, g, b):
    xf = x.astype(jnp.float32)
    mu = xf.mean(-1, keepdims=True)
    var = jnp.square(xf - mu).mean(-1, keepdims=True)
    return ((xf - mu) * lax.rsqrt(var + LN_EPS) * g.astype(jnp.float32) + b.astype(jnp.float32)).astype(x.dtype)


def partial_rope(x, pos):
    rot = x.shape[-1] // ROPE_FRAC
    half = rot // 2
    freqs = jnp.power(ROPE_THETA, -jnp.arange(half, dtype=jnp.float32) / half)
    ang = pos.astype(jnp.float32)[:, None] * freqs
    cos, sin = jnp.cos(ang)[:, None, :], jnp.sin(ang)[:, None, :]
    xf = x.astype(jnp.float32)
    x1, x2, rest = xf[..., :half], xf[..., half:rot], xf[..., rot:]
    out = jnp.concatenate([x1 * cos - x2 * sin, x1 * sin + x2 * cos, rest], axis=-1)
    return out.astype(x.dtype)


def causal_dwconv(x, prev, w):
    xp = jnp.concatenate([prev.astype(x.dtype), x], axis=1)
    y = lax.conv_general_dilated(xp, w[:, None, :].astype(x.dtype), window_strides=(1,), padding='VALID',
                                 dimension_numbers=('NWC', 'WIO', 'NWC'), feature_group_count=x.shape[-1])
    return y, xp[:, -(w.shape[0] - 1):]


def gmlp_branch(u, v, ln_g, ln_b, w_s, b_s):
    B, T, _ = v.shape
    v = layer_norm(v, ln_g, ln_b)
    rows = min(T, CHUNK)
    n_chunks = T // rows
    ws = jnp.where(jnp.tril(jnp.ones((rows, rows), dtype=bool)), w_s[:, :rows, :rows], 0)
    vc = v.reshape(B, n_chunks, rows, G_A, D_A // G_A)
    s = jnp.einsum('gij,bcjgd->bcigd', ws.astype(v.dtype), vc) + b_s[:, :rows].T[:, :, None]
    return u * s.reshape(B, T, D_A), v


def shortconv_branch(bx, cx, xin, w_conv, prev):
    y, buf = causal_dwconv(cx * xin, prev, w_conv)
    return bx * y, buf


def conformer_branch(a, gate, w_dw, b_dw, ln_g, ln_b, prev):
    h = a * jax.nn.sigmoid(gate)
    y, buf = causal_dwconv(h, prev, w_dw)
    y = jax.nn.silu(layer_norm(y + b_dw, ln_g, ln_b))
    return y, buf


def indexer_topk(qi, w, ki, qpos, n_top):
    dots = jnp.einsum('bqhd,bsd->bqhs', qi, ki, preferred_element_type=jnp.float32)
    score = jnp.einsum('bqh,bqhs->bqs', w.astype(jnp.float32), jax.nn.relu(dots))
    visible = jnp.arange(ki.shape[1], dtype=jnp.int32)[None, :] <= qpos[:, None]
    score = jnp.where(visible[None], score, -jnp.inf)
    _, sel = lax.top_k(score, n_top)
    return sel


def attend_selected(q, k_sel, v_sel, sel, qpos):
    B, Tq, H, hd = q.shape
    qg = q.reshape(B, Tq, N_KV_HEADS, H // N_KV_HEADS, hd)
    s = jnp.einsum('bqkgd,bqskd->bqkgs', qg, k_sel, preferred_element_type=jnp.float32) * (hd ** -0.5)
    valid = (sel <= qpos[None, :, None])[:, :, None, None, :]
    p = jax.nn.softmax(jnp.where(valid, s, -jnp.inf), axis=-1)
    o = jnp.einsum('bqkgs,bqskd->bqkgd', p.astype(v_sel.dtype), v_sel)
    return o.reshape(B, Tq, H * hd)


def dsa_prompt(q, k, v, qi, w, ki):
    B, T = q.shape[:2]
    n_top = min(TOPK_MAX, T // 4)
    nb = T // Q_BLOCK

    def blk(args):
        qb, qib, wb, pb = args
        sel = indexer_topk(qib, wb, ki, pb, n_top)
        ks = jax.vmap(lambda a, i: a[i])(k, sel)
        vs = jax.vmap(lambda a, i: a[i])(v, sel)
        return attend_selected(qb, ks, vs, sel, pb)

    to_blocks = lambda a: jnp.moveaxis(a.reshape((B, nb, Q_BLOCK) + a.shape[2:]), 1, 0)
    pos = jnp.arange(T, dtype=jnp.int32).reshape(nb, Q_BLOCK)
    out = lax.map(blk, (to_blocks(q), to_blocks(qi), to_blocks(w), pos))
    return jnp.moveaxis(out, 0, 1).reshape(B, T, N_HEADS * HEAD_DIM)


def dsa_sample(q, k_new, v_new, qi, w, ki_new, cache_k, cache_v, cache_ki, page_table):
    DB, TS = q.shape[:2]
    past = page_table.shape[1] * PAGE_SIZE
    n_top = min(TOPK_MAX, (past + TS) // 4)
    qpos = past + jnp.arange(TS, dtype=jnp.int32)
    ki_past = cache_ki[page_table].reshape(DB, past, IDX_DIM)
    ki_all = jnp.concatenate([ki_past.astype(ki_new.dtype), ki_new], axis=1)
    sel = indexer_topk(qi, w, ki_all, qpos, n_top)
    in_past = sel < past
    s_past = jnp.minimum(sel, past - 1)
    phys = jnp.take_along_axis(page_table, (s_past // PAGE_SIZE).reshape(DB, -1), axis=1).reshape(sel.shape)
    off = s_past % PAGE_SIZE
    s_new = jnp.clip(sel - past, 0, TS - 1)

    def gather(cache, new):
        from_cache = cache[phys, off].astype(new.dtype)
        from_new = jax.vmap(lambda a, i: a[i])(new, s_new)
        return jnp.where(in_past[..., None, None], from_cache, from_new)

    return attend_selected(q, gather(cache_k, k_new), gather(cache_v, v_new), sel, qpos)


def mixer(x, pos, lp, prev_b, prev_c, attend):
    B, T, _ = x.shape
    h = x @ lp["w_in"]
    points = np.cumsum(SPLIT_WIDTHS)[:-1].tolist()
    ua, va, bx, cx, xin, ca, cgate, q, k, v, qi, ki, wi, gates = jnp.split(h, points, axis=-1)
    y_a, v_rows = gmlp_branch(ua, va, lp["gmlp_ln_g"], lp["gmlp_ln_b"], lp["gmlp_ws"], lp["gmlp_bs"])
    y_b, buf_b = shortconv_branch(bx, cx, xin, lp["conv_b_w"], prev_b)
    y_c, buf_c = conformer_branch(ca, cgate, lp["conv_c_w"], lp["conv_c_bias"], lp["conf_ln_g"], lp["conf_ln_b"], prev_c)
    q = partial_rope(q.reshape(B, T, N_HEADS, HEAD_DIM), pos)
    k = partial_rope(k.reshape(B, T, N_KV_HEADS, HEAD_DIM), pos)
    v = v.reshape(B, T, N_KV_HEADS, HEAD_DIM)
    qi = partial_rope(qi.reshape(B, T, N_IDX_HEADS, IDX_DIM), pos)
    ki = partial_rope(ki.reshape(B, T, 1, IDX_DIM), pos)[:, :, 0]
    wi = wi * (N_IDX_HEADS * IDX_DIM) ** -0.5
    y_d = attend(q, k, v, qi, wi, ki)
    g = jax.nn.sigmoid(gates.astype(jnp.float32)).astype(x.dtype).reshape(B, T, N_BRANCHES, D_MODEL)
    merged = (g[:, :, 0] * (y_a @ lp["w_br_a"]) + g[:, :, 1] * (y_b @ lp["w_br_b"])
              + g[:, :, 2] * (y_c @ lp["w_br_c"]) + g[:, :, 3] * (y_d @ lp["w_br_d"]))
    return merged @ lp["w_o"], (k, v, ki, buf_b, buf_c, v_rows)


def swiglu(x, wg, wu, wd):
    return (jax.nn.silu(x @ wg) * (x @ wu)) @ wd


def moe_swiglu(x, router, wg, wu, wd):
    logits = (x @ router).astype(jnp.float32)
    top_v, top_i = lax.top_k(logits, TOP_K)
    top_w = jax.nn.softmax(top_v, axis=-1)
    gate = jnp.sum(jax.nn.one_hot(top_i, N_EXPERTS, dtype=jnp.float32) * top_w[..., None], axis=-2)
    out = jnp.zeros_like(x)
    for e in range(N_EXPERTS):
        out = out + gate[..., e:e + 1].astype(x.dtype) * swiglu(x, wg[e], wu[e], wd[e])
    return out


def channel_mixer(x, l, lp):
    if l % 2 == 0:
        return swiglu(x, lp["ffn_w_gate"], lp["ffn_w_up"], lp["ffn_w_down"])
    return moe_swiglu(x, lp["moe_router"], lp["moe_w_gate"], lp["moe_w_up"], lp["moe_w_down"])


def setup_inputs(seed: int = 0) -> dict:
    key = jax.random.key(seed)
    ks = iter(jax.random.split(key, 64))

    def nrm(shape, scale=1.0):
        return jax.random.normal(next(ks), shape, jnp.float32) * scale

    n_pages = PAST_LEN // PAGE_SIZE
    n_pool = (DEC_BATCH * n_pages * 5) // 4
    n_dense = (DEPTH + 1) // 2
    n_moe = DEPTH // 2
    page_table = jax.random.permutation(next(ks), n_pool)[:DEC_BATCH * n_pages].reshape(DEC_BATCH, n_pages).astype(jnp.int32)
    return {
        "x_prompt": nrm((BATCH, SEQ, D_MODEL)),
        "x_sample": nrm((DEC_BATCH, DEC_SEQ, D_MODEL)),
        "cache_k": nrm((DEPTH, n_pool, PAGE_SIZE, N_KV_HEADS, HEAD_DIM)),
        "cache_v": nrm((DEPTH, n_pool, PAGE_SIZE, N_KV_HEADS, HEAD_DIM)),
        "cache_idx_k": nrm((DEPTH, n_pool, PAGE_SIZE, IDX_DIM)),
        "state_conv_b": nrm((DEPTH, DEC_BATCH, W_B - 1, D_B)),
        "state_conv_c": nrm((DEPTH, DEC_BATCH, W_C - 1, D_C)),
        "page_table": page_table,
        "w_in": nrm((DEPTH, D_MODEL, D_IN), D_MODEL ** -0.5),
        "gmlp_ln_g": 1.0 + nrm((DEPTH, D_A), 0.02),
        "gmlp_ln_b": nrm((DEPTH, D_A), 0.02),
        "gmlp_ws": nrm((DEPTH, G_A, CHUNK, CHUNK), CHUNK ** -0.5),
        "gmlp_bs": 1.0 + nrm((DEPTH, G_A, CHUNK), 0.02),
        "conv_b_w": nrm((DEPTH, W_B, D_B), W_B ** -0.5),
        "conv_c_w": nrm((DEPTH, W_C, D_C), W_C ** -0.5),
        "conv_c_bias": nrm((DEPTH, D_C), 0.02),
        "conf_ln_g": 1.0 + nrm((DEPTH, D_C), 0.02),
        "conf_ln_b": nrm((DEPTH, D_C), 0.02),
        "w_br_a": nrm((DEPTH, D_A, D_MODEL), D_A ** -0.5),
        "w_br_b": nrm((DEPTH, D_B, D_MODEL), D_B ** -0.5),
        "w_br_c": nrm((DEPTH, D_C, D_MODEL), D_C ** -0.5),
        "w_br_d": nrm((DEPTH, N_HEADS * HEAD_DIM, D_MODEL), (N_HEADS * HEAD_DIM) ** -0.5),
        "w_o": nrm((DEPTH, D_MODEL, D_MODEL), BETA * D_MODEL ** -0.5),
        "ln1_g": 1.0 + nrm((DEPTH, D_MODEL), 0.02),
        "ln1_b": nrm((DEPTH, D_MODEL), 0.02),
        "ln2_g": 1.0 + nrm((DEPTH, D_MODEL), 0.02),
        "ln2_b": nrm((DEPTH, D_MODEL), 0.02),
        "ffn_w_gate": nrm((n_dense, D_MODEL, D_FF), D_MODEL ** -0.5),
        "ffn_w_up": nrm((n_dense, D_MODEL, D_FF), D_MODEL ** -0.5),
        "ffn_w_down": nrm((n_dense, D_FF, D_MODEL), BETA * D_FF ** -0.5),
        "moe_router": nrm((n_moe, D_MODEL, N_EXPERTS), D_MODEL ** -0.5),
        "moe_w_gate": nrm((n_moe, N_EXPERTS, D_MODEL, D_FF_EXPERT), D_MODEL ** -0.5),
        "moe_w_up": nrm((n_moe, N_EXPERTS, D_MODEL, D_FF_EXPERT), D_MODEL ** -0.5),
        "moe_w_down": nrm((n_moe, N_EXPERTS, D_FF_EXPERT, D_MODEL), BETA * D_FF_EXPERT ** -0.5),
    }


def reference(x_prompt, x_sample, cache_k, cache_v, cache_idx_k, state_conv_b, state_conv_c, page_table,
              w_in, gmlp_ln_g, gmlp_ln_b, gmlp_ws, gmlp_bs, conv_b_w, conv_c_w, conv_c_bias, conf_ln_g, conf_ln_b,
              w_br_a, w_br_b, w_br_c, w_br_d, w_o, ln1_g, ln1_b, ln2_g, ln2_b,
              ffn_w_gate, ffn_w_up, ffn_w_down, moe_router, moe_w_gate, moe_w_up, moe_w_down):
    xp, xs = x_prompt, x_sample
    B, T = xp.shape[:2]
    DB, TS = xs.shape[:2]
    past = page_table.shape[1] * PAGE_SIZE
    pos_p = jnp.arange(T, dtype=jnp.int32)
    pos_s = past + jnp.arange(TS, dtype=jnp.int32)
    kp, vp, kip, cbp, ccp = [], [], [], [], []
    ks_, vs_, kis, cbs, ccs, gvs = [], [], [], [], [], []
    for l in range(DEPTH):
        lp = {"w_in": w_in[l], "gmlp_ln_g": gmlp_ln_g[l], "gmlp_ln_b": gmlp_ln_b[l], "gmlp_ws": gmlp_ws[l],
              "gmlp_bs": gmlp_bs[l], "conv_b_w": conv_b_w[l], "conv_c_w": conv_c_w[l], "conv_c_bias": conv_c_bias[l],
              "conf_ln_g": conf_ln_g[l], "conf_ln_b": conf_ln_b[l], "w_br_a": w_br_a[l], "w_br_b": w_br_b[l],
              "w_br_c": w_br_c[l], "w_br_d": w_br_d[l], "w_o": w_o[l]}
        if l % 2 == 0:
            j = l // 2
            lp.update({"ffn_w_gate": ffn_w_gate[j], "ffn_w_up": ffn_w_up[j], "ffn_w_down": ffn_w_down[j]})
        else:
            j = l // 2
            lp.update({"moe_router": moe_router[j], "moe_w_gate": moe_w_gate[j], "moe_w_up": moe_w_up[j],
                       "moe_w_down": moe_w_down[j]})
        zb = jnp.zeros((B, W_B - 1, D_B), xp.dtype)
        zc = jnp.zeros((B, W_C - 1, D_C), xp.dtype)
        mix_p, st_p = mixer(xp, pos_p, lp, zb, zc, dsa_prompt)
        xp = layer_norm(ALPHA * xp + mix_p, ln1_g[l], ln1_b[l])
        xp = layer_norm(ALPHA * xp + channel_mixer(xp, l, lp), ln2_g[l], ln2_b[l])
        attend_s = lambda q, k, v, qi, wi, ki: dsa_sample(q, k, v, qi, wi, ki, cache_k[l], cache_v[l],
                                                          cache_idx_k[l], page_table)
        mix_s, st_s = mixer(xs, pos_s, lp, state_conv_b[l], state_conv_c[l], attend_s)
        xs = layer_norm(ALPHA * xs + mix_s, ln1_g[l], ln1_b[l])
        xs = layer_norm(ALPHA * xs + channel_mixer(xs, l, lp), ln2_g[l], ln2_b[l])
        kp.append(st_p[0]); vp.append(st_p[1]); kip.append(st_p[2]); cbp.append(st_p[3]); ccp.append(st_p[4])
        ks_.append(st_s[0]); vs_.append(st_s[1]); kis.append(st_s[2]); cbs.append(st_s[3]); ccs.append(st_s[4])
        gvs.append(st_s[5])
    return (xp, xs, jnp.stack(kp), jnp.stack(vp), jnp.stack(kip), jnp.stack(cbp), jnp.stack(ccp),
            jnp.stack(ks_), jnp.stack(vs_), jnp.stack(kis), jnp.stack(cbs), jnp.stack(ccs), jnp.stack(gvs))
```

```python
import functools

import jax
import jax.numpy as jnp
from jax import lax
from jax.experimental import pallas as pl
from jax.experimental.pallas import tpu as pltpu

PAGE = 128
CHUNK = 128
D_A, G_A = 256, 4
D_B, W_B = 256, 3
D_C, W_C = 256, 31
N_HEADS, N_KV, HEAD_DIM = 8, 2, 64
N_IDX_HEADS, IDX_DIM = 4, 64
TOPK_MAX = 256
Q_BLOCK = 128
ROPE_THETA = 500000.0
ROPE_FRAC = 4
N_EXPERTS, TOP_K = 8, 2
N_BRANCHES = 4
LN_EPS = 1e-5

LANES = 128
SUBLANES = 8
VMEM_LIMIT = 56 * 1024 * 1024

MIX_W = 2 * D_A + 3 * D_B + 2 * D_C
Q_W, KV_W, QI_W = N_HEADS * HEAD_DIM, N_KV * HEAD_DIM, N_IDX_HEADS * IDX_DIM
ATT_W = Q_W + 2 * KV_W + QI_W + LANES
INT_MIN = -2 ** 31
NEG = -1e30

_NT = (((1,), (1,)), ((), ()))


def _tile(n, pref):
    t = min(pref, n)
    while t >= 16:
        if n % t == 0 and t % 16 == 0:
            return t
        t -= 16
    return n


def _cparams(sem):
    return pltpu.CompilerParams(dimension_semantics=sem, vmem_limit_bytes=VMEM_LIMIT)


def _ln(x, g, b):
    mu = jnp.mean(x, axis=-1, keepdims=True)
    xc = x - mu
    var = jnp.mean(xc * xc, axis=-1, keepdims=True)
    return xc * lax.rsqrt(var + LN_EPS) * g + b


def _const_spec(shape):
    nd = len(shape)
    return pl.BlockSpec(shape, lambda *_: (0,) * nd)


def _proj_mix_kernel(x_ref, w_ref, o_ref):
    o_ref[...] = jnp.dot(x_ref[...], w_ref[...], preferred_element_type=jnp.float32)


def _proj_gates_kernel(x_ref, w_ref, o_ref):
    h = jnp.dot(x_ref[...], w_ref[...], preferred_element_type=jnp.float32)
    o_ref[...] = jax.nn.sigmoid(h).astype(o_ref.dtype)


def _proj_attn_kernel(x_ref, w_ref, c_ref, a_ref, b_ref, h_ref, q_ref, kv_ref, qi_ref, ki_ref):
    h = jnp.dot(x_ref[...], w_ref[...], preferred_element_type=jnp.float32)
    cc, aa, bb = c_ref[...], a_ref[...], b_ref[...]
    lane = lax.broadcasted_iota(jnp.int32, cc.shape, 1)
    n_grp = ATT_W // LANES
    v_grp = (Q_W + KV_W) // LANES
    outs = []
    for gi in range(n_grp):
        xg = h[:, gi * LANES:(gi + 1) * LANES]
        if gi == v_grp:
            outs.append(xg)
            continue
        rot = xg * cc + pltpu.roll(xg, LANES - 8, 1) * aa + pltpu.roll(xg, 8, 1) * bb
        if gi == n_grp - 1:
            wi_scale = float(N_IDX_HEADS * IDX_DIM) ** -0.5
            rot = jnp.where(lane < IDX_DIM, rot, xg * wi_scale)
        outs.append(rot)
    hr = jnp.concatenate(outs, axis=1)
    h_ref[...] = hr
    q_ref[...] = (hr[:, :Q_W] * (HEAD_DIM ** -0.5)).astype(q_ref.dtype)
    kv_ref[...] = hr[:, Q_W:Q_W + 2 * KV_W].astype(kv_ref.dtype)
    qi_ref[...] = hr[:, Q_W + 2 * KV_W:Q_W + 2 * KV_W + QI_W].astype(qi_ref.dtype)
    ki_ref[...] = hr[:, ATT_W - LANES:].astype(ki_ref.dtype)


def _in_projection(xb, w_mix, w_att, w_gat, rope_c, rope_a, rope_b, n_pos_tiles, tm):
    n, d = xb.shape
    grid = (n // tm,)
    row = lambda w: pl.BlockSpec((tm, w), lambda i: (i, 0))
    mix = pl.pallas_call(
        _proj_mix_kernel, grid=grid,
        in_specs=[row(d), _const_spec(w_mix.shape)], out_specs=row(MIX_W),
        out_shape=jax.ShapeDtypeStruct((n, MIX_W), jnp.float32),
        compiler_params=_cparams(("parallel",)), name="proj_mix")(xb, w_mix)
    gw = w_gat.shape[1]
    gsig = pl.pallas_call(
        _proj_gates_kernel, grid=grid,
        in_specs=[row(d), _const_spec(w_gat.shape)], out_specs=row(gw),
        out_shape=jax.ShapeDtypeStruct((n, gw), jnp.bfloat16),
        compiler_params=_cparams(("parallel",)), name="proj_gates")(xb, w_gat)
    tab = pl.BlockSpec((tm, LANES), lambda i: (i % n_pos_tiles, 0))
    h_att, q_b, kv_b, qi_b, ki_b = pl.pallas_call(
        _proj_attn_kernel, grid=grid,
        in_specs=[row(d), _const_spec(w_att.shape), tab, tab, tab],
        out_specs=[row(ATT_W), row(Q_W), row(2 * KV_W), row(QI_W), row(LANES)],
        out_shape=[jax.ShapeDtypeStruct((n, ATT_W), jnp.float32),
                   jax.ShapeDtypeStruct((n, Q_W), jnp.bfloat16),
                   jax.ShapeDtypeStruct((n, 2 * KV_W), jnp.bfloat16),
                   jax.ShapeDtypeStruct((n, QI_W), jnp.bfloat16),
                   jax.ShapeDtypeStruct((n, LANES), jnp.bfloat16)],
        compiler_params=_cparams(("parallel",)), name="proj_attn")(xb, w_att, rope_c, rope_a, rope_b)
    return mix, gsig, h_att, q_b, kv_b, qi_b, ki_b


HALO_B, HALO_C = 8, 32
CONV_ROWS = 64


def _mix_prompt_kernel(mix_ref, g0_ref, g1_ref, g2_ref, lag_ref, lab_ref, ws_ref, bias_ref,
                       wcb_ref, wcc_ref, bdw_ref, lcg_ref, lcb_ref, wa_ref, wb_ref, wc_ref,
                       out_ref, cb_ref, cc_ref, ext_b, ext_c, conv_b, conv_c):
    tm = mix_ref.shape[0]
    j = pl.program_id(1)

    @pl.when(j == 0)
    def _():
        ext_b[0:HALO_B, :] = jnp.zeros((HALO_B, D_B), jnp.float32)
        ext_c[0:HALO_C, :] = jnp.zeros((HALO_C, D_C), jnp.float32)

    mix = mix_ref[...]
    ua, va = mix[:, 0:D_A], mix[:, D_A:2 * D_A]
    o = 2 * D_A
    bx, cx, xin = mix[:, o:o + D_B], mix[:, o + D_B:o + 2 * D_B], mix[:, o + 2 * D_B:o + 3 * D_B]
    o += 3 * D_B
    ca, cgate = mix[:, o:o + D_C], mix[:, o + D_C:o + 2 * D_C]

    v = _ln(va, lag_ref[...], lab_ref[...])
    vb = v.astype(jnp.bfloat16)
    lane = lax.broadcasted_iota(jnp.int32, (CHUNK, D_A), 1)
    r_i = lax.broadcasted_iota(jnp.int32, (CHUNK, CHUNK), 0)
    c_i = lax.broadcasted_iota(jnp.int32, (CHUNK, CHUNK), 1)
    gw = D_A // G_A
    s_rows = []
    for c in range(tm // CHUNK):
        vc = vb[c * CHUNK:(c + 1) * CHUNK, :]
        sc = bias_ref[...]
        for g in range(G_A):
            wsg = jnp.where(c_i <= r_i, ws_ref[g], 0.0).astype(jnp.bfloat16)
            sg = jnp.dot(wsg, vc, preferred_element_type=jnp.float32)
            sc = sc + jnp.where((lane >= g * gw) & (lane < (g + 1) * gw), sg, 0.0)
        s_rows.append(sc)
    y_a = ua * jnp.concatenate(s_rows, axis=0)

    zb = cx * xin
    hc = ca * jax.nn.sigmoid(cgate)
    ext_b[HALO_B:HALO_B + tm, :] = zb
    ext_c[HALO_C:HALO_C + tm, :] = hc

    for r in range(tm // CONV_ROWS):
        r0 = r * CONV_ROWS
        acc = jnp.zeros((CONV_ROWS, D_B), jnp.float32)
        for d in range(W_B):
            acc = acc + wcb_ref[W_B - 1 - d:W_B - d, :] * ext_b[r0 + HALO_B - d:r0 + HALO_B - d + CONV_ROWS, :]
        conv_b[r0:r0 + CONV_ROWS, :] = acc
        acc = jnp.zeros((CONV_ROWS, D_C), jnp.float32)
        for d in range(W_C):
            acc = acc + wcc_ref[W_C - 1 - d:W_C - d, :] * ext_c[r0 + HALO_C - d:r0 + HALO_C - d + CONV_ROWS, :]
        conv_c[r0:r0 + CONV_ROWS, :] = acc
    y_b = bx * conv_b[...]
    yc = _ln(conv_c[...] + bdw_ref[...], lcg_ref[...], lcb_ref[...])
    y_c = yc * jax.nn.sigmoid(yc)

    ext_b[0:HALO_B, :] = zb[tm - HALO_B:, :]
    ext_c[0:HALO_C, :] = hc[tm - HALO_C:, :]
    cb_ref[0] = zb[tm - (W_B - 1):, :]
    cc_ref[0] = hc[tm - (W_C - 1):, :]

    pa = jnp.dot(y_a.astype(jnp.bfloat16), wa_ref[...], preferred_element_type=jnp.float32)
    pb = jnp.dot(y_b.astype(jnp.bfloat16), wb_ref[...], preferred_element_type=jnp.float32)
    pc = jnp.dot(y_c.astype(jnp.bfloat16), wc_ref[...], preferred_element_type=jnp.float32)
    out_ref[...] = (g0_ref[...].astype(jnp.float32) * pa + g1_ref[...].astype(jnp.float32) * pb
                    + g2_ref[...].astype(jnp.float32) * pc)


def _mix_prompt(mix, gsig, lp, nb, t, tm):
    n = mix.shape[0]
    d = lp["w_br_a"].shape[1]
    nt = t // tm
    row = lambda w, col=0: pl.BlockSpec((tm, w), lambda b, j, col=col: (b * nt + j, col))
    consts = [lp["gmlp_ln_g"], lp["gmlp_ln_b"], lp["gmlp_ws"], lp["gmlp_bias_full"], lp["conv_b_w"],
              lp["conv_c_w"], lp["conv_c_bias"], lp["conf_ln_g"], lp["conf_ln_b"],
              lp["w_br_a"], lp["w_br_b"], lp["w_br_c"]]
    out, cb, cc = pl.pallas_call(
        _mix_prompt_kernel, grid=(nb, nt),
        in_specs=[row(MIX_W), row(d, 0), row(d, 1), row(d, 2)] + [_const_spec(c.shape) for c in consts],
        out_specs=[row(d),
                   pl.BlockSpec((1, W_B - 1, D_B), lambda b, j: (b, 0, 0)),
                   pl.BlockSpec((1, W_C - 1, D_C), lambda b, j: (b, 0, 0))],
        out_shape=[jax.ShapeDtypeStruct((n, d), jnp.float32),
                   jax.ShapeDtypeStruct((nb, W_B - 1, D_B), jnp.float32),
                   jax.ShapeDtypeStruct((nb, W_C - 1, D_C), jnp.float32)],
        scratch_shapes=[pltpu.VMEM((HALO_B + tm, D_B), jnp.float32),
                        pltpu.VMEM((HALO_C + tm, D_C), jnp.float32),
                        pltpu.VMEM((tm, D_B), jnp.float32),
                        pltpu.VMEM((tm, D_C), jnp.float32)],
        compiler_params=_cparams(("arbitrary", "arbitrary")), name="mix_prompt")(
            mix, gsig, gsig, gsig, *consts)
    return out, cb, cc


def _mix_sample_kernel(mix_ref, g0_ref, g1_ref, g2_ref, pb_ref, pc_ref, lag_ref, lab_ref, ws0_ref, bs0_ref,
                       wcb_ref, wcc_ref, bdw_ref, lcg_ref, lcb_ref, wa_ref, wb_ref, wc_ref,
                       out_ref, v_ref, zb_ref, hc_ref):
    mix = mix_ref[...]
    ua, va = mix[:, 0:D_A], mix[:, D_A:2 * D_A]
    o = 2 * D_A
    bx, cx, xin = mix[:, o:o + D_B], mix[:, o + D_B:o + 2 * D_B], mix[:, o + 2 * D_B:o + 3 * D_B]
    o += 3 * D_B
    ca, cgate = mix[:, o:o + D_C], mix[:, o + D_C:o + 2 * D_C]

    v = _ln(va, lag_ref[...], lab_ref[...])
    v_ref[...] = v
    y_a = ua * (ws0_ref[...] * v + bs0_ref[...])

    zb = cx * xin
    zb_ref[...] = zb
    acc = wcb_ref[W_B - 1:W_B, :] * zb
    for k in range(W_B - 1):
        acc = acc + wcb_ref[k:k + 1, :] * pb_ref[k]
    y_b = bx * acc

    hc = ca * jax.nn.sigmoid(cgate)
    hc_ref[...] = hc
    acc = wcc_ref[W_C - 1:W_C, :] * hc
    for k in range(W_C - 1):
        acc = acc + wcc_ref[k:k + 1, :] * pc_ref[k]
    yc = _ln(acc + bdw_ref[...], lcg_ref[...], lcb_ref[...])
    y_c = yc * jax.nn.sigmoid(yc)

    pa = jnp.dot(y_a.astype(jnp.bfloat16), wa_ref[...], preferred_element_type=jnp.float32)
    pb = jnp.dot(y_b.astype(jnp.bfloat16), wb_ref[...], preferred_element_type=jnp.float32)
    pc = jnp.dot(y_c.astype(jnp.bfloat16), wc_ref[...], preferred_element_type=jnp.float32)
    out_ref[...] = (g0_ref[...].astype(jnp.float32) * pa + g1_ref[...].astype(jnp.float32) * pb
                    + g2_ref[...].astype(jnp.float32) * pc)


def _mix_sample(mix, gsig, prev_b, prev_c, lp):
    n = mix.shape[0]
    d = lp["w_br_a"].shape[1]
    gcol = lambda col: pl.BlockSpec((n, d), lambda i, col=col: (0, col))
    pbt = jnp.transpose(prev_b, (1, 0, 2))
    pct = jnp.transpose(prev_c, (1, 0, 2))
    consts = [lp["gmlp_ln_g"], lp["gmlp_ln_b"], lp["gmlp_ws0"], lp["gmlp_bs0"], lp["conv_b_w"],
              lp["conv_c_w"], lp["conv_c_bias"], lp["conf_ln_g"], lp["conf_ln_b"],
              lp["w_br_a"], lp["w_br_b"], lp["w_br_c"]]
    return pl.pallas_call(
        _mix_sample_kernel, grid=(1,),
        in_specs=[_const_spec(mix.shape), gcol(0), gcol(1), gcol(2), _const_spec(pbt.shape),
                  _const_spec(pct.shape)] + [_const_spec(c.shape) for c in consts],
        out_specs=[_const_spec((n, d)), _const_spec((n, D_A)), _const_spec((n, D_B)), _const_spec((n, D_C))],
        out_shape=[jax.ShapeDtypeStruct((n, d), jnp.float32), jax.ShapeDtypeStruct((n, D_A), jnp.float32),
                   jax.ShapeDtypeStruct((n, D_B), jnp.float32), jax.ShapeDtypeStruct((n, D_C), jnp.float32)],
        compiler_params=_cparams(("arbitrary",)), name="mix_sample")(
            mix, gsig, gsig, gsig, pbt, pct, *consts)


def _sort_key(score, visible):
    score = jnp.where(score == 0.0, 0.0, score)
    bits = pltpu.bitcast(score, jnp.int32)
    key = jnp.where(bits < 0, bits ^ jnp.int32(0x7FFFFFFF), bits)
    return jnp.where(visible, key, jnp.int32(INT_MIN))


KEY_CHUNK = 512


def _dsa_prompt_kernel(q_ref, qi_ref, wt_ref, ki_ref, k_ref, v_ref, o_ref, skey, bias, satt, *, n_top, idx_bits):
    i = pl.program_id(1)
    sub = KEY_CHUNK // Q_BLOCK
    nch = (i + sub) // sub
    row128 = lax.broadcasted_iota(jnp.int32, (Q_BLOCK, Q_BLOCK), 0)
    col128 = lax.broadcasted_iota(jnp.int32, (Q_BLOCK, Q_BLOCK), 1)
    rowck = lax.broadcasted_iota(jnp.int32, (KEY_CHUNK, Q_BLOCK), 0)

    def score_blk(c, _):
        r0 = pl.multiple_of(c * Q_BLOCK, Q_BLOCK)
        kc = ki_ref[0, pl.ds(r0, Q_BLOCK), :]
        acc = jnp.zeros((Q_BLOCK, Q_BLOCK), jnp.float32)
        for h in range(N_IDX_HEADS):
            dots = lax.dot_general(kc, qi_ref[0, h], _NT, preferred_element_type=jnp.float32)
            acc = acc + wt_ref[0, h:h + 1, :] * jnp.maximum(dots, 0.0)
        visible = (c < i) | ((c == i) & (row128 <= col128))
        skey[pl.ds(r0, Q_BLOCK), :] = _sort_key(acc, visible)
        return 0

    lax.fori_loop(0, nch * sub, score_blk, 0)

    def count(pred_fn):
        def body(c, acc):
            r0 = pl.multiple_of(c * KEY_CHUNK, KEY_CHUNK)
            m = pred_fn(skey[pl.ds(r0, KEY_CHUNK), :], r0)
            return acc + m.reshape(KEY_CHUNK // SUBLANES, SUBLANES, Q_BLOCK).sum(axis=0)
        acc = lax.fori_loop(0, nch, body, jnp.zeros((SUBLANES, Q_BLOCK), jnp.float32))
        return acc.sum(axis=0, keepdims=True)

    def thr_step(t, cur):
        cand = cur + lax.shift_left(jnp.int32(1), 31 - t)
        cnt = count(lambda s, r0: jnp.where(s >= cand, 1.0, 0.0))
        return jnp.where(cnt >= n_top, cand, cur)

    thr = lax.fori_loop(0, 32, thr_step, jnp.full((1, Q_BLOCK), INT_MIN, jnp.int32))
    c_gt = count(lambda s, r0: jnp.where(s > thr, 1.0, 0.0))
    need = n_top - c_gt

    def cut_step(t, cur):
        cand = cur + lax.shift_left(jnp.int32(1), idx_bits - 1 - t)
        cnt = count(lambda s, r0: jnp.where(s == thr, jnp.where(rowck < cand - r0, 1.0, 0.0), 0.0))
        return jnp.where(cnt <= need, cand, cur)

    cut = lax.fori_loop(0, idx_bits, cut_step, jnp.zeros((1, Q_BLOCK), jnp.int32))
    cut = jnp.where(thr == INT_MIN, 0, cut)

    def bias_blk(c, _):
        r0 = pl.multiple_of(c * Q_BLOCK, Q_BLOCK)
        s = skey[pl.ds(r0, Q_BLOCK), :]
        sel = jnp.where(s > thr, 1, jnp.where(s == thr, jnp.where(row128 < cut - r0, 1, 0), 0))
        bias[c] = jnp.where(sel == 1, 0.0, NEG).astype(jnp.float32).T
        return 0

    lax.fori_loop(0, nch * sub, bias_blk, 0)

    for h in range(N_HEADS):
        g = h // (N_HEADS // N_KV)
        qh = q_ref[0, h]

        def pass_a(c, macc):
            r0 = pl.multiple_of(c * KEY_CHUNK, KEY_CHUNK)
            s = lax.dot_general(qh, k_ref[0, g, pl.ds(r0, KEY_CHUNK), :], _NT,
                                preferred_element_type=jnp.float32)
            for jj in range(sub):
                sj = s[:, jj * Q_BLOCK:(jj + 1) * Q_BLOCK] + bias[c * sub + jj]
                satt[c * sub + jj] = sj
                macc = jnp.maximum(macc, sj)
            return macc

        macc = lax.fori_loop(0, nch, pass_a, jnp.full((Q_BLOCK, Q_BLOCK), NEG, jnp.float32))
        m = jnp.max(macc, axis=1, keepdims=True)

        def pass_b(c, carry):
            lacc, oacc = carry
            r0 = pl.multiple_of(c * KEY_CHUNK, KEY_CHUNK)
            ps = []
            for jj in range(sub):
                p = jnp.exp(satt[c * sub + jj] - m)
                lacc = lacc + p
                ps.append(p.astype(jnp.bfloat16))
            pc = jnp.concatenate(ps, axis=1)
            oacc = oacc + jnp.dot(pc, v_ref[0, g, pl.ds(r0, KEY_CHUNK), :], preferred_element_type=jnp.float32)
            return lacc, oacc

        lacc, oacc = lax.fori_loop(
            0, nch, pass_b,
            (jnp.zeros((Q_BLOCK, Q_BLOCK), jnp.float32), jnp.zeros((Q_BLOCK, HEAD_DIM), jnp.float32)))
        l = jnp.sum(lacc, axis=1, keepdims=True)
        o_ref[0, h] = (oacc / l).astype(o_ref.dtype)


def _dsa_prompt(q_b, kv_b, qi_b, ki_b, wi, nb, t):
    n = q_b.shape[0]
    n_top = min(TOPK_MAX, t // 4)
    nqb = t // Q_BLOCK
    t_pad = ((t + KEY_CHUNK - 1) // KEY_CHUNK) * KEY_CHUNK
    idx_bits = max(1, (t_pad + 1).bit_length())
    qh = q_b.reshape(nb, t, N_HEADS, HEAD_DIM).transpose(0, 2, 1, 3)
    qih = qi_b.reshape(nb, t, N_IDX_HEADS, IDX_DIM).transpose(0, 2, 1, 3)
    kv = kv_b.reshape(nb, t, 2, N_KV, HEAD_DIM)
    kh = kv[:, :, 0].transpose(0, 2, 1, 3)
    vh = kv[:, :, 1].transpose(0, 2, 1, 3)
    if t_pad != t:
        padk = ((0, 0), (0, 0), (0, t_pad - t), (0, 0))
        kh, vh = jnp.pad(kh, padk), jnp.pad(vh, padk)
    kib = ki_b[:, :IDX_DIM].reshape(nb, t, IDX_DIM)
    if t_pad != t:
        kib = jnp.pad(kib, ((0, 0), (0, t_pad - t), (0, 0)))
    wt = jnp.pad(wi.reshape(nb, t, N_IDX_HEADS).transpose(0, 2, 1), ((0, 0), (0, SUBLANES - N_IDX_HEADS), (0, 0)))
    kern = functools.partial(_dsa_prompt_kernel, n_top=n_top, idx_bits=idx_bits)
    out = pl.pallas_call(
        kern, grid=(nb, nqb),
        in_specs=[pl.BlockSpec((1, N_HEADS, Q_BLOCK, HEAD_DIM), lambda b, i: (b, 0, i, 0)),
                  pl.BlockSpec((1, N_IDX_HEADS, Q_BLOCK, IDX_DIM), lambda b, i: (b, 0, i, 0)),
                  pl.BlockSpec((1, SUBLANES, Q_BLOCK), lambda b, i: (b, 0, i)),
                  pl.BlockSpec((1, t_pad, IDX_DIM), lambda b, i: (b, 0, 0)),
                  pl.BlockSpec((1, N_KV, t_pad, HEAD_DIM), lambda b, i: (b, 0, 0, 0)),
                  pl.BlockSpec((1, N_KV, t_pad, HEAD_DIM), lambda b, i: (b, 0, 0, 0))],
        out_specs=pl.BlockSpec((1, N_HEADS, Q_BLOCK, HEAD_DIM), lambda b, i: (b, 0, i, 0)),
        out_shape=jax.ShapeDtypeStruct((nb, N_HEADS, t, HEAD_DIM), jnp.bfloat16),
        scratch_shapes=[pltpu.VMEM((t_pad, Q_BLOCK), jnp.int32),
                        pltpu.VMEM((t_pad // Q_BLOCK, Q_BLOCK, Q_BLOCK), jnp.float32),
                        pltpu.VMEM((t_pad // Q_BLOCK, Q_BLOCK, Q_BLOCK), jnp.float32)],
        compiler_params=_cparams(("arbitrary", "arbitrary")), name="dsa_prompt")(qh, qih, wt, kib, kh, vh)
    return out.transpose(0, 2, 1, 3).reshape(n, Q_W)


def _dsa_s_score_kernel(pt_ref, qi_ref, w_ref, kin_ref, cache_ref, o_ref, buf, sem):
    b = pl.program_id(0)
    n_pages = pt_ref.shape[1]

    def page_copy(j):
        return pltpu.make_async_copy(cache_ref.at[pt_ref[b, j]], buf.at[pl.ds(j * PAGE, PAGE)], sem)

    def start(j, _):
        page_copy(j).start()
        return 0

    def wait(j, _):
        page_copy(j).wait()
        return 0

    lax.fori_loop(0, n_pages, start, 0)
    lax.fori_loop(0, n_pages, wait, 0)
    qi = qi_ref[0]
    w = w_ref[0]
    dots = lax.dot_general(qi, buf[...].astype(jnp.bfloat16), _NT, preferred_element_type=jnp.float32)
    past = n_pages * PAGE
    o_ref[0, :, 0:past] = jnp.sum(w * jnp.maximum(dots, 0.0), axis=0, keepdims=True)
    d_new = jnp.sum(qi.astype(jnp.float32) * kin_ref[0].astype(jnp.float32), axis=1, keepdims=True)
    s_new = jnp.sum(w * jnp.maximum(d_new, 0.0), axis=0, keepdims=True)
    lane = lax.broadcasted_iota(jnp.int32, (1, LANES), 1)
    o_ref[0, :, past:past + LANES] = jnp.where(lane == 0, s_new, 0.0)


def _dsa_s_select_kernel(s_ref, o_ref, *, n_top, n_keys, idx_bits):
    s = s_ref[...]
    lane = lax.broadcasted_iota(jnp.int32, s.shape, 1)
    key = _sort_key(s, lane < n_keys)

    def thr_step(t, cur):
        cand = cur + lax.shift_left(jnp.int32(1), 31 - t)
        cnt = jnp.sum(jnp.where(key >= cand, 1.0, 0.0), axis=1, keepdims=True)
        return jnp.where(cnt >= n_top, cand, cur)

    thr = lax.fori_loop(0, 32, thr_step, jnp.full((s.shape[0], 1), INT_MIN, jnp.int32))
    need = n_top - jnp.sum(jnp.where(key > thr, 1.0, 0.0), axis=1, keepdims=True)

    def cut_step(t, cur):
        cand = cur + lax.shift_left(jnp.int32(1), idx_bits - 1 - t)
        cnt = jnp.sum(jnp.where(key == thr, jnp.where(lane < cand, 1.0, 0.0), 0.0), axis=1, keepdims=True)
        return jnp.where(cnt <= need, cand, cur)

    cut = lax.fori_loop(0, idx_bits, cut_step, jnp.zeros((s.shape[0], 1), jnp.int32))
    cut = jnp.where(thr == INT_MIN, 0, cut)
    sel = jnp.where(key > thr, 1, jnp.where(key == thr, jnp.where(lane < cut, 1, 0), 0))
    o_ref[...] = jnp.where(sel == 1, 0.0, NEG).astype(jnp.float32)


def _dsa_s_attn_kernel(pt_ref, q_ref, bias_ref, kn_ref, vn_ref, ck_ref, cv_ref, o_ref, kbuf, vbuf, sem_k, sem_v):
    b = pl.program_id(0)
    n_pages = pt_ref.shape[1]
    past = n_pages * PAGE

    def k_copy(j):
        return pltpu.make_async_copy(ck_ref.at[pt_ref[b, j]], kbuf.at[pl.ds(j * PAGE, PAGE)], sem_k)

    def v_copy(j):
        return pltpu.make_async_copy(cv_ref.at[pt_ref[b, j]], vbuf.at[pl.ds(j * PAGE, PAGE)], sem_v)

    def start(j, _):
        k_copy(j).start()
        v_copy(j).start()
        return 0

    def wait(j, _):
        k_copy(j).wait()
        v_copy(j).wait()
        return 0

    lax.fori_loop(0, n_pages, start, 0)
    lax.fori_loop(0, n_pages, wait, 0)
    q = q_ref[0]
    s = lax.dot_general(q, kbuf[...].astype(jnp.bfloat16), _NT, preferred_element_type=jnp.float32)
    s = s + bias_ref[0, :, 0:past]
    s_new = jnp.sum(q.astype(jnp.float32) * kn_ref[0].astype(jnp.float32), axis=1, keepdims=True)
    s_new = s_new + bias_ref[0, :, past:past + 1]
    m = jnp.maximum(jnp.max(s, axis=1, keepdims=True), s_new)
    p = jnp.exp(s - m)
    p_new = jnp.exp(s_new - m)
    l = jnp.sum(p, axis=1, keepdims=True) + p_new
    o = jnp.dot(p.astype(jnp.bfloat16), vbuf[...].astype(jnp.bfloat16), preferred_element_type=jnp.float32)
    o = o + p_new * vn_ref[0].astype(jnp.float32)
    o_ref[0] = o / l


def _dsa_sample(h_att, wi, cache_k_l, cache_v_l, cache_ki_l, page_table):
    nb = h_att.shape[0]
    n_pages = page_table.shape[1]
    past = n_pages * PAGE
    n_keys = past + 1
    n_top = min(TOPK_MAX, n_keys // 4)
    lk = past + LANES
    n_pool = cache_k_l.shape[0]
    q = h_att[:, :Q_W] * (HEAD_DIM ** -0.5)
    k_new = h_att[:, Q_W:Q_W + KV_W]
    v_new = h_att[:, Q_W + KV_W:Q_W + 2 * KV_W]
    qi = h_att[:, Q_W + 2 * KV_W:Q_W + 2 * KV_W + QI_W]
    ki_new = h_att[:, ATT_W - LANES:ATT_W - LANES + IDX_DIM]

    qi8 = jnp.pad(qi.reshape(nb, N_IDX_HEADS, IDX_DIM), ((0, 0), (0, SUBLANES - N_IDX_HEADS), (0, 0)))
    w8 = jnp.pad(wi, ((0, 0), (0, SUBLANES - N_IDX_HEADS)))[:, :, None]
    grid_spec = pltpu.PrefetchScalarGridSpec(
        num_scalar_prefetch=1, grid=(nb,),
        in_specs=[pl.BlockSpec((1, SUBLANES, IDX_DIM), lambda b, pt: (b, 0, 0)),
                  pl.BlockSpec((1, SUBLANES, 1), lambda b, pt: (b, 0, 0)),
                  pl.BlockSpec((1, 1, IDX_DIM), lambda b, pt: (b, 0, 0)),
                  pl.BlockSpec(memory_space=pl.ANY)],
        out_specs=pl.BlockSpec((1, 1, lk), lambda b, pt: (b, 0, 0)),
        scratch_shapes=[pltpu.VMEM((past, IDX_DIM), jnp.float32), pltpu.SemaphoreType.DMA])
    scores = pl.pallas_call(
        _dsa_s_score_kernel, grid_spec=grid_spec,
        out_shape=jax.ShapeDtypeStruct((nb, 1, lk), jnp.float32),
        compiler_params=_cparams(("arbitrary",)), name="dsa_sample_score")(
            page_table, qi8.astype(jnp.bfloat16), w8, ki_new[:, None, :].astype(jnp.bfloat16), cache_ki_l)

    idx_bits = max(1, (lk + 1).bit_length())
    bias = pl.pallas_call(
        functools.partial(_dsa_s_select_kernel, n_top=n_top, n_keys=n_keys, idx_bits=idx_bits),
        grid=(1,), in_specs=[_const_spec((nb, lk))], out_specs=_const_spec((nb, lk)),
        out_shape=jax.ShapeDtypeStruct((nb, lk), jnp.float32),
        compiler_params=_cparams(("arbitrary",)), name="dsa_sample_select")(scores.reshape(nb, lk))

    grp = jnp.arange(N_HEADS) // (N_HEADS // N_KV)
    lane_grp = jnp.arange(KV_W) // HEAD_DIM
    own = (grp[:, None] == lane_grp[None, :])
    q_bd = jnp.where(own[None], jnp.tile(q.reshape(nb, N_HEADS, HEAD_DIM), (1, 1, N_KV)), 0.0)
    grid_spec = pltpu.PrefetchScalarGridSpec(
        num_scalar_prefetch=1, grid=(nb,),
        in_specs=[pl.BlockSpec((1, N_HEADS, KV_W), lambda b, pt: (b, 0, 0)),
                  pl.BlockSpec((1, 1, lk), lambda b, pt: (b, 0, 0)),
                  pl.BlockSpec((1, 1, KV_W), lambda b, pt: (b, 0, 0)),
                  pl.BlockSpec((1, 1, KV_W), lambda b, pt: (b, 0, 0)),
                  pl.BlockSpec(memory_space=pl.ANY), pl.BlockSpec(memory_space=pl.ANY)],
        out_specs=pl.BlockSpec((1, N_HEADS, KV_W), lambda b, pt: (b, 0, 0)),
        scratch_shapes=[pltpu.VMEM((past, KV_W), jnp.float32), pltpu.VMEM((past, KV_W), jnp.float32),
                        pltpu.SemaphoreType.DMA, pltpu.SemaphoreType.DMA])
    o_full = pl.pallas_call(
        _dsa_s_attn_kernel, grid_spec=grid_spec,
        out_shape=jax.ShapeDtypeStruct((nb, N_HEADS, KV_W), jnp.float32),
        compiler_params=_cparams(("arbitrary",)), name="dsa_sample_attn")(
            page_table, q_bd.astype(jnp.bfloat16), bias.reshape(nb, 1, lk),
            k_new[:, None, :].astype(jnp.bfloat16), v_new[:, None, :].astype(jnp.bfloat16),
            cache_k_l.reshape(n_pool, PAGE, KV_W), cache_v_l.reshape(n_pool, PAGE, KV_W))
    o4 = o_full.reshape(nb, N_HEADS, N_KV, HEAD_DIM)
    y = jnp.where((grp[:, None] == jnp.arange(N_KV)[None, :])[None, :, :, None], o4, 0.0).sum(axis=2)
    return y.reshape(nb, Q_W)


def _post_kernel(m_ref, yd_ref, g3_ref, x_ref, wd_ref, wo_ref, lg_ref, lb_ref, o_ref, ob_ref, *, alpha):
    pd = jnp.dot(yd_ref[...], wd_ref[...], preferred_element_type=jnp.float32)
    merged = m_ref[...] + g3_ref[...].astype(jnp.float32) * pd
    mix = jnp.dot(merged.astype(jnp.bfloat16), wo_ref[...], preferred_element_type=jnp.float32)
    y = _ln(alpha * x_ref[...] + mix, lg_ref[...], lb_ref[...])
    o_ref[...] = y
    ob_ref[...] = y.astype(ob_ref.dtype)


def _post(merged_abc, yd_b, gsig, x, lp, alpha, tm):
    n, d = x.shape
    row = lambda w, col=0: pl.BlockSpec((tm, w), lambda i, col=col: (i, col))
    consts = [lp["w_br_d"], lp["w_o"], lp["ln1_g"], lp["ln1_b"]]
    return pl.pallas_call(
        functools.partial(_post_kernel, alpha=alpha), grid=(n // tm,),
        in_specs=[row(d), row(Q_W), row(d, N_BRANCHES - 1), row(d)] + [_const_spec(c.shape) for c in consts],
        out_specs=[row(d), row(d)],
        out_shape=[jax.ShapeDtypeStruct((n, d), jnp.float32), jax.ShapeDtypeStruct((n, d), jnp.bfloat16)],
        compiler_params=_cparams(("parallel",)), name="post")(merged_abc, yd_b, gsig, x, *consts)


def _ffn_kernel(xb_ref, x_ref, wg_ref, wu_ref, wd_ref, lg_ref, lb_ref, o_ref, ob_ref, acc, *, alpha):
    f = pl.program_id(1)

    @pl.when(f == 0)
    def _():
        acc[...] = jnp.zeros_like(acc)

    xb = xb_ref[...]
    hg = jnp.dot(xb, wg_ref[...], preferred_element_type=jnp.float32)
    hu = jnp.dot(xb, wu_ref[...], preferred_element_type=jnp.float32)
    h = (hg * jax.nn.sigmoid(hg) * hu).astype(jnp.bfloat16)
    acc[...] += jnp.dot(h, wd_ref[...], preferred_element_type=jnp.float32)

    @pl.when(f == pl.num_programs(1) - 1)
    def _():
        y = _ln(alpha * x_ref[...] + acc[...], lg_ref[...], lb_ref[...])
        o_ref[...] = y
        ob_ref[...] = y.astype(ob_ref.dtype)


def _ffn(xb, x, wg, wu, wd, lg, lb, alpha, tm):
    n, d = x.shape
    ff = wg.shape[1]
    tf = next(c for c in (512, 256, 128, ff) if ff % c == 0)
    row = pl.BlockSpec((tm, d), lambda i, f: (i, 0))
    return pl.pallas_call(
        functools.partial(_ffn_kernel, alpha=alpha), grid=(n // tm, ff // tf),
        in_specs=[row, row, pl.BlockSpec((d, tf), lambda i, f: (0, f)), pl.BlockSpec((d, tf), lambda i, f: (0, f)),
                  pl.BlockSpec((tf, d), lambda i, f: (f, 0)), _const_spec(lg.shape), _const_spec(lb.shape)],
        out_specs=[row, row],
        out_shape=[jax.ShapeDtypeStruct((n, d), jnp.float32), jax.ShapeDtypeStruct((n, d), jnp.bfloat16)],
        scratch_shapes=[pltpu.VMEM((tm, d), jnp.float32)],
        compiler_params=_cparams(("parallel", "arbitrary")), name="ffn")(xb, x, wg, wu, wd, lg, lb)


def _moe_kernel(xb_ref, x_ref, r_ref, wg_ref, wu_ref, wd_ref, lg_ref, lb_ref, o_ref, acc, gate, gcol, *, alpha):
    e = pl.program_id(1)
    f = pl.program_id(2)
    lane = lax.broadcasted_iota(jnp.int32, gate.shape, 1)

    @pl.when((e == 0) & (f == 0))
    def _():
        acc[...] = jnp.zeros_like(acc)
        logits = jnp.dot(x_ref[...], r_ref[...], preferred_element_type=jnp.float32,
                         precision=lax.Precision.HIGHEST)
        logits = jnp.where(lane < N_EXPERTS, logits, -jnp.inf)
        m1 = jnp.max(logits, axis=1, keepdims=True)
        lanef = lane.astype(jnp.float32)
        i1 = jnp.min(jnp.where(logits == m1, lanef, float(LANES)), axis=1, keepdims=True)
        rest = jnp.where(lanef == i1, -jnp.inf, logits)
        m2 = jnp.max(rest, axis=1, keepdims=True)
        i2 = jnp.min(jnp.where(rest == m2, lanef, float(LANES)), axis=1, keepdims=True)
        e2 = jnp.exp(m2 - m1)
        w1 = 1.0 / (1.0 + e2)
        w2 = e2 / (1.0 + e2)
        gate[...] = jnp.where(lanef == i1, w1, jnp.where(lanef == i2, w2, 0.0))

    @pl.when(f == 0)
    def _():
        gcol[...] = jnp.sum(jnp.where(lane == e, gate[...], 0.0), axis=1, keepdims=True)

    xb = xb_ref[...]
    hg = jnp.dot(xb, wg_ref[0], preferred_element_type=jnp.float32)
    hu = jnp.dot(xb, wu_ref[0], preferred_element_type=jnp.float32)
    h = (hg * jax.nn.sigmoid(hg) * hu * gcol[...]).astype(jnp.bfloat16)
    acc[...] += jnp.dot(h, wd_ref[0], preferred_element_type=jnp.float32)

    @pl.when((e == pl.num_programs(1) - 1) & (f == pl.num_programs(2) - 1))
    def _():
        o_ref[...] = _ln(alpha * x_ref[...] + acc[...], lg_ref[...], lb_ref[...])


def _moe(xb, x, router_p, wg, wu, wd, lg, lb, alpha, tm):
    n, d = x.shape
    ne, _, ff = wg.shape
    tf = next(c for c in (512, 256, 128, ff) if ff % c == 0)
    row = pl.BlockSpec((tm, d), lambda i, e, f: (i, 0))
    return pl.pallas_call(
        functools.partial(_moe_kernel, alpha=alpha), grid=(n // tm, ne, ff // tf),
        in_specs=[row, row, _const_spec(router_p.shape),
                  pl.BlockSpec((1, d, tf), lambda i, e, f: (e, 0, f)),
                  pl.BlockSpec((1, d, tf), lambda i, e, f: (e, 0, f)),
                  pl.BlockSpec((1, tf, d), lambda i, e, f: (e, f, 0)),
                  _const_spec(lg.shape), _const_spec(lb.shape)],
        out_specs=row,
        out_shape=jax.ShapeDtypeStruct((n, d), jnp.float32),
        scratch_shapes=[pltpu.VMEM((tm, d), jnp.float32), pltpu.VMEM((tm, LANES), jnp.float32),
                        pltpu.VMEM((tm, 1), jnp.float32)],
        compiler_params=_cparams(("parallel", "arbitrary", "arbitrary")), name="moe")(
            xb, x, router_p, wg, wu, wd, lg, lb)


def _rope_tables(pos):
    rot = HEAD_DIM // ROPE_FRAC
    half = rot // 2
    freqs = jnp.power(ROPE_THETA, -jnp.arange(half, dtype=jnp.float32) / half)
    ang = pos.astype(jnp.float32)[:, None] * freqs
    cos, sin = jnp.cos(ang), jnp.sin(ang)
    t = pos.shape[0]
    ones = jnp.ones((t, HEAD_DIM - rot), jnp.float32)
    zeros = jnp.zeros((t, HEAD_DIM - rot), jnp.float32)
    zh = jnp.zeros((t, half), jnp.float32)
    c = jnp.concatenate([cos, cos, ones], axis=1)
    a = jnp.concatenate([-sin, zh, zeros], axis=1)
    b = jnp.concatenate([zh, sin, zeros], axis=1)
    rep = LANES // HEAD_DIM
    return jnp.tile(c, (1, rep)), jnp.tile(a, (1, rep)), jnp.tile(b, (1, rep))


def _layer_params(l, w_in, gmlp_ln_g, gmlp_ln_b, gmlp_ws, gmlp_bs, conv_b_w, conv_c_w, conv_c_bias, conf_ln_g,
                  conf_ln_b, w_br_a, w_br_b, w_br_c, w_br_d, w_o, ln1_g, ln1_b, ln2_g, ln2_b):
    bf = jnp.bfloat16
    w = w_in[l]
    d = w.shape[0]
    att_end = MIX_W + Q_W + 2 * KV_W + QI_W + IDX_DIM + N_IDX_HEADS
    w_att = jnp.pad(w[:, MIX_W:att_end], ((0, 0), (0, ATT_W - (att_end - MIX_W))))
    row = lambda a: a[l][None, :]
    gw = D_A // G_A
    return {
        "w_mix": w[:, :MIX_W].astype(bf), "w_att": w_att.astype(bf), "w_gat": w[:, att_end:].astype(bf),
        "gmlp_ln_g": row(gmlp_ln_g), "gmlp_ln_b": row(gmlp_ln_b), "gmlp_ws": gmlp_ws[l],
        "gmlp_bias_full": jnp.repeat(gmlp_bs[l].T, gw, axis=1),
        "gmlp_ws0": jnp.repeat(gmlp_ws[l][:, 0, 0], gw)[None, :],
        "gmlp_bs0": jnp.repeat(gmlp_bs[l][:, 0], gw)[None, :],
        "conv_b_w": conv_b_w[l], "conv_c_w": conv_c_w[l], "conv_c_bias": row(conv_c_bias),
        "conf_ln_g": row(conf_ln_g), "conf_ln_b": row(conf_ln_b),
        "w_br_a": w_br_a[l].astype(bf), "w_br_b": w_br_b[l].astype(bf), "w_br_c": w_br_c[l].astype(bf),
        "w_br_d": w_br_d[l].astype(bf), "w_o": w_o[l].astype(bf),
        "ln1_g": row(ln1_g), "ln1_b": row(ln1_b), "ln2_g": row(ln2_g), "ln2_b": row(ln2_b),
        "d": d,
    }


def _channel_mixer(l, xb, x, lp, ffn_w, moe_w, alpha, tm):
    if l % 2 == 0:
        wg, wu, wd = ffn_w
        j = l // 2
        return _ffn(xb, x, wg[j], wu[j], wd[j], lp["ln2_g"], lp["ln2_b"], alpha, tm)
    router, wg, wu, wd = moe_w
    j = l // 2
    out = _moe(xb, x, router[j], wg[j], wu[j], wd[j], lp["ln2_g"], lp["ln2_b"], alpha, tm)
    return out, out.astype(jnp.bfloat16)


def kernel(x_prompt, x_sample, cache_k, cache_v, cache_idx_k, state_conv_b, state_conv_c, page_table,
           w_in, gmlp_ln_g, gmlp_ln_b, gmlp_ws, gmlp_bs, conv_b_w, conv_c_w, conv_c_bias, conf_ln_g, conf_ln_b,
           w_br_a, w_br_b, w_br_c, w_br_d, w_o, ln1_g, ln1_b, ln2_g, ln2_b,
           ffn_w_gate, ffn_w_up, ffn_w_down, moe_router, moe_w_gate, moe_w_up, moe_w_down):
    bf = jnp.bfloat16
    nb, t, d = x_prompt.shape
    ns, ts, _ = x_sample.shape
    assert ts == 1 and t % Q_BLOCK == 0
    depth = w_in.shape[0]
    alpha = float((2 * depth) ** 0.25)
    past = page_table.shape[1] * PAGE

    ffn_w = (ffn_w_gate.astype(bf), ffn_w_up.astype(bf), ffn_w_down.astype(bf))
    router_p = jnp.pad(moe_router, ((0, 0), (0, 0), (0, LANES - N_EXPERTS)))
    moe_w = (router_p, moe_w_gate.astype(bf), moe_w_up.astype(bf), moe_w_down.astype(bf))

    rope_p = _rope_tables(jnp.arange(t, dtype=jnp.int32))
    rope_s = tuple(jnp.tile(r, (ns, 1)) for r in _rope_tables(past + jnp.arange(1, dtype=jnp.int32)))

    n = nb * t
    tm_p = _tile(t, 512)
    tm_f = _tile(n, 1024)
    xp = x_prompt.reshape(n, d)
    xs = x_sample.reshape(ns, d)
    xp_b, xs_b = xp.astype(bf), xs.astype(bf)
    outs = {k: [] for k in ("kp", "vp", "kip", "cbp", "ccp", "ks", "vs", "kis", "cbs", "ccs", "gvs")}
    for l in range(depth):
        lp = _layer_params(l, w_in, gmlp_ln_g, gmlp_ln_b, gmlp_ws, gmlp_bs, conv_b_w, conv_c_w, conv_c_bias,
                           conf_ln_g, conf_ln_b, w_br_a, w_br_b, w_br_c, w_br_d, w_o, ln1_g, ln1_b, ln2_g, ln2_b)
        mix, gsig, h_att, q_b, kv_b, qi_b, ki_b = _in_projection(
            xp_b, lp["w_mix"], lp["w_att"], lp["w_gat"], *rope_p, t // tm_p, tm_p)
        merged_abc, cb, cc = _mix_prompt(mix, gsig, lp, nb, t, tm_p)
        wi = h_att[:, ATT_W - LANES + IDX_DIM:ATT_W - LANES + IDX_DIM + N_IDX_HEADS]
        y_d = _dsa_prompt(q_b, kv_b, qi_b, ki_b, wi, nb, t)
        x1, x1_b = _post(merged_abc, y_d, gsig, xp, lp, alpha, tm_p)
        xp, xp_b = _channel_mixer(l, x1_b, x1, lp, ffn_w, moe_w, alpha, tm_f)
        outs["kp"].append(h_att[:, Q_W:Q_W + KV_W].reshape(nb, t, N_KV, HEAD_DIM))
        outs["vp"].append(h_att[:, Q_W + KV_W:Q_W + 2 * KV_W].reshape(nb, t, N_KV, HEAD_DIM))
        outs["kip"].append(h_att[:, ATT_W - LANES:ATT_W - LANES + IDX_DIM].reshape(nb, t, IDX_DIM))
        outs["cbp"].append(cb)
        outs["ccp"].append(cc)
        mix, gsig, h_att, _, _, _, _ = _in_projection(
            xs_b, lp["w_mix"], lp["w_att"], lp["w_gat"], *rope_s, 1, ns)
        merged_abc, v_rows, zb, hc = _mix_sample(mix, gsig, state_conv_b[l], state_conv_c[l], lp)
        wi = h_att[:, ATT_W - LANES + IDX_DIM:ATT_W - LANES + IDX_DIM + N_IDX_HEADS]
        y_d = _dsa_sample(h_att, wi, cache_k[l], cache_v[l], cache_idx_k[l], page_table)
        x1, x1_b = _post(merged_abc, y_d.astype(bf), gsig, xs, lp, alpha, ns)
        xs, xs_b = _channel_mixer(l, x1_b, x1, lp, ffn_w, moe_w, alpha, ns)
        outs["ks"].append(h_att[:, Q_W:Q_W + KV_W].reshape(ns, 1, N_KV, HEAD_DIM))
        outs["vs"].append(h_att[:, Q_W + KV_W:Q_W + 2 * KV_W].reshape(ns, 1, N_KV, HEAD_DIM))
        outs["kis"].append(h_att[:, ATT_W - LANES:ATT_W - LANES + IDX_DIM].reshape(ns, 1, IDX_DIM))
        outs["cbs"].append(jnp.concatenate([state_conv_b[l][:, 1:], zb[:, None, :]], axis=1))
        outs["ccs"].append(jnp.concatenate([state_conv_c[l][:, 1:], hc[:, None, :]], axis=1))
        outs["gvs"].append(v_rows[:, None, :])
    st = lambda k: jnp.stack(outs[k])
    return (xp.reshape(nb, t, d), xs.reshape(ns, 1, d), st("kp"), st("vp"), st("kip"), st("cbp"), st("ccp"),
            st("ks"), st("vs"), st("kis"), st("cbs"), st("ccs"), st("gvs"))
```

```python
import functools

import jax
import jax.numpy as jnp
from jax import lax
from jax.experimental import pallas as pl
from jax.experimental.pallas import tpu as pltpu

PAGE = 128
CHUNK = 128
D_A, G_A = 256, 4
D_B, W_B = 256, 3
D_C, W_C = 256, 31
N_HEADS, N_KV, HEAD_DIM = 8, 2, 64
N_IDX_HEADS, IDX_DIM = 4, 64
TOPK_MAX = 256
Q_BLOCK = 128
ROPE_THETA = 500000.0
ROPE_FRAC = 4
N_EXPERTS, TOP_K = 8, 2
N_BRANCHES = 4
LN_EPS = 1e-5

LANES = 128
SUBLANES = 8
VMEM_LIMIT = 56 * 1024 * 1024

MIX_W = 2 * D_A + 3 * D_B + 2 * D_C
Q_W, KV_W, QI_W = N_HEADS * HEAD_DIM, N_KV * HEAD_DIM, N_IDX_HEADS * IDX_DIM
ATT_W = Q_W + 2 * KV_W + QI_W + LANES
INT_MIN = -2 ** 31
NEG = -1e30

_NT = (((1,), (1,)), ((), ()))


def _tile(n, pref):
    t = min(pref, n)
    while t >= 16:
        if n % t == 0 and t % 16 == 0:
            return t
        t -= 16
    return n


def _cparams(sem):
    return pltpu.CompilerParams(dimension_semantics=sem, vmem_limit_bytes=VMEM_LIMIT)


def _ln(x, g, b):
    mu = jnp.mean(x, axis=-1, keepdims=True)
    xc = x - mu
    var = jnp.mean(xc * xc, axis=-1, keepdims=True)
    return xc * lax.rsqrt(var + LN_EPS) * g + b


def _const_spec(shape):
    nd = len(shape)
    return pl.BlockSpec(shape, lambda *_: (0,) * nd)


def _proj_mix_kernel(x_ref, w_ref, o_ref):
    o_ref[...] = jnp.dot(x_ref[...], w_ref[...], preferred_element_type=jnp.float32)


def _proj_gates_kernel(x_ref, w_ref, o_ref):
    h = jnp.dot(x_ref[...], w_ref[...], preferred_element_type=jnp.float32)
    o_ref[...] = jax.nn.sigmoid(h).astype(o_ref.dtype)


def _proj_attn_kernel(x_ref, w_ref, c_ref, a_ref, b_ref, h_ref, q_ref, kv_ref, qi_ref, ki_ref):
    h = jnp.dot(x_ref[...], w_ref[...], preferred_element_type=jnp.float32)
    cc, aa, bb = c_ref[...], a_ref[...], b_ref[...]
    lane = lax.broadcasted_iota(jnp.int32, cc.shape, 1)
    n_grp = ATT_W // LANES
    v_grp = (Q_W + KV_W) // LANES
    outs = []
    for gi in range(n_grp):
        xg = h[:, gi * LANES:(gi + 1) * LANES]
        if gi == v_grp:
            outs.append(xg)
            continue
        rot = xg * cc + pltpu.roll(xg, LANES - 8, 1) * aa + pltpu.roll(xg, 8, 1) * bb
        if gi == n_grp - 1:
            wi_scale = float(N_IDX_HEADS * IDX_DIM) ** -0.5
            rot = jnp.where(lane < IDX_DIM, rot, xg * wi_scale)
        outs.append(rot)
    hr = jnp.concatenate(outs, axis=1)
    h_ref[...] = hr
    q_ref[...] = (hr[:, :Q_W] * (HEAD_DIM ** -0.5)).astype(q_ref.dtype)
    kv_ref[...] = hr[:, Q_W:Q_W + 2 * KV_W].astype(kv_ref.dtype)
    qi_ref[...] = hr[:, Q_W + 2 * KV_W:Q_W + 2 * KV_W + QI_W].astype(qi_ref.dtype)
    ki_ref[...] = hr[:, ATT_W - LANES:].astype(ki_ref.dtype)


def _in_projection(xb, w_mix, w_att, w_gat, rope_c, rope_a, rope_b, n_pos_tiles, tm):
    n, d = xb.shape
    grid = (n // tm,)
    row = lambda w: pl.BlockSpec((tm, w), lambda i: (i, 0))
    mix = pl.pallas_call(
        _proj_mix_kernel, grid=grid,
        in_specs=[row(d), _const_spec(w_mix.shape)], out_specs=row(MIX_W),
        out_shape=jax.ShapeDtypeStruct((n, MIX_W), jnp.float32),
        compiler_params=_cparams(("parallel",)), name="proj_mix")(xb, w_mix)
    gw = w_gat.shape[1]
    gsig = pl.pallas_call(
        _proj_gates_kernel, grid=grid,
        in_specs=[row(d), _const_spec(w_gat.shape)], out_specs=row(gw),
        out_shape=jax.ShapeDtypeStruct((n, gw), jnp.bfloat16),
        compiler_params=_cparams(("parallel",)), name="proj_gates")(xb, w_gat)
    tab = pl.BlockSpec((tm, LANES), lambda i: (i % n_pos_tiles, 0))
    h_att, q_b, kv_b, qi_b, ki_b = pl.pallas_call(
        _proj_attn_kernel, grid=grid,
        in_specs=[row(d), _const_spec(w_att.shape), tab, tab, tab],
        out_specs=[row(ATT_W), row(Q_W), row(2 * KV_W), row(QI_W), row(LANES)],
        out_shape=[jax.ShapeDtypeStruct((n, ATT_W), jnp.float32),
                   jax.ShapeDtypeStruct((n, Q_W), jnp.bfloat16),
                   jax.ShapeDtypeStruct((n, 2 * KV_W), jnp.bfloat16),
                   jax.ShapeDtypeStruct((n, QI_W), jnp.bfloat16),
                   jax.ShapeDtypeStruct((n, LANES), jnp.bfloat16)],
        compiler_params=_cparams(("parallel",)), name="proj_attn")(xb, w_att, rope_c, rope_a, rope_b)
    return mix, gsig, h_att, q_b, kv_b, qi_b, ki_b


HALO_B, HALO_C = 8, 32
CONV_ROWS = 64


def _mix_prompt_kernel(mix_ref, g0_ref, g1_ref, g2_ref, lag_ref, lab_ref, ws_ref, bias_ref,
                       wcb_ref, wcc_ref, bdw_ref, lcg_ref, lcb_ref, wa_ref, wb_ref, wc_ref,
                       out_ref, cb_ref, cc_ref, ext_b, ext_c, conv_b, conv_c):
    tm = mix_ref.shape[0]
    j = pl.program_id(1)

    @pl.when(j == 0)
    def _():
        ext_b[0:HALO_B, :] = jnp.zeros((HALO_B, D_B), jnp.float32)
        ext_c[0:HALO_C, :] = jnp.zeros((HALO_C, D_C), jnp.float32)

    mix = mix_ref[...]
    ua, va = mix[:, 0:D_A], mix[:, D_A:2 * D_A]
    o = 2 * D_A
    bx, cx, xin = mix[:, o:o + D_B], mix[:, o + D_B:o + 2 * D_B], mix[:, o + 2 * D_B:o + 3 * D_B]
    o += 3 * D_B
    ca, cgate = mix[:, o:o + D_C], mix[:, o + D_C:o + 2 * D_C]

    v = _ln(va, lag_ref[...], lab_ref[...])
    vb = v.astype(jnp.bfloat16)
    lane = lax.broadcasted_iota(jnp.int32, (CHUNK, D_A), 1)
    r_i = lax.broadcasted_iota(jnp.int32, (CHUNK, CHUNK), 0)
    c_i = lax.broadcasted_iota(jnp.int32, (CHUNK, CHUNK), 1)
    gw = D_A // G_A
    s_rows = []
    for c in range(tm // CHUNK):
        vc = vb[c * CHUNK:(c + 1) * CHUNK, :]
        sc = bias_ref[...]
        for g in range(G_A):
            wsg = jnp.where(c_i <= r_i, ws_ref[g], 0.0).astype(jnp.bfloat16)
            sg = jnp.dot(wsg, vc, preferred_element_type=jnp.float32)
            sc = sc + jnp.where((lane >= g * gw) & (lane < (g + 1) * gw), sg, 0.0)
        s_rows.append(sc)
    y_a = ua * jnp.concatenate(s_rows, axis=0)

    zb = cx * xin
    hc = ca * jax.nn.sigmoid(cgate)
    ext_b[HALO_B:HALO_B + tm, :] = zb
    ext_c[HALO_C:HALO_C + tm, :] = hc

    for r in range(tm // CONV_ROWS):
        r0 = r * CONV_ROWS
        acc = jnp.zeros((CONV_ROWS, D_B), jnp.float32)
        for d in range(W_B):
            acc = acc + wcb_ref[W_B - 1 - d:W_B - d, :] * ext_b[r0 + HALO_B - d:r0 + HALO_B - d + CONV_ROWS, :]
        conv_b[r0:r0 + CONV_ROWS, :] = acc
        acc = jnp.zeros((CONV_ROWS, D_C), jnp.float32)
        for d in range(W_C):
            acc = acc + wcc_ref[W_C - 1 - d:W_C - d, :] * ext_c[r0 + HALO_C - d:r0 + HALO_C - d + CONV_ROWS, :]
        conv_c[r0:r0 + CONV_ROWS, :] = acc
    y_b = bx * conv_b[...]
    yc = _ln(conv_c[...] + bdw_ref[...], lcg_ref[...], lcb_ref[...])
    y_c = yc * jax.nn.sigmoid(yc)

    ext_b[0:HALO_B, :] = zb[tm - HALO_B:, :]
    ext_c[0:HALO_C, :] = hc[tm - HALO_C:, :]
    cb_ref[0] = zb[tm - (W_B - 1):, :]
    cc_ref[0] = hc[tm - (W_C - 1):, :]

    pa = jnp.dot(y_a.astype(jnp.bfloat16), wa_ref[...], preferred_element_type=jnp.float32)
    pb = jnp.dot(y_b.astype(jnp.bfloat16), wb_ref[...], preferred_element_type=jnp.float32)
    pc = jnp.dot(y_c.astype(jnp.bfloat16), wc_ref[...], preferred_element_type=jnp.float32)
    out_ref[...] = (g0_ref[...].astype(jnp.float32) * pa + g1_ref[...].astype(jnp.float32) * pb
                    + g2_ref[...].astype(jnp.float32) * pc)


def _mix_prompt(mix, gsig, lp, nb, t, tm):
    n = mix.shape[0]
    d = lp["w_br_a"].shape[1]
    nt = t // tm
    row = lambda w, col=0: pl.BlockSpec((tm, w), lambda b, j, col=col: (b * nt + j, col))
    consts = [lp["gmlp_ln_g"], lp["gmlp_ln_b"], lp["gmlp_ws"], lp["gmlp_bias_full"], lp["conv_b_w"],
              lp["conv_c_w"], lp["conv_c_bias"], lp["conf_ln_g"], lp["conf_ln_b"],
              lp["w_br_a"], lp["w_br_b"], lp["w_br_c"]]
    out, cb, cc = pl.pallas_call(
        _mix_prompt_kernel, grid=(nb, nt),
        in_specs=[row(MIX_W), row(d, 0), row(d, 1), row(d, 2)] + [_const_spec(c.shape) for c in consts],
        out_specs=[row(d),
                   pl.BlockSpec((1, W_B - 1, D_B), lambda b, j: (b, 0, 0)),
                   pl.BlockSpec((1, W_C - 1, D_C), lambda b, j: (b, 0, 0))],
        out_shape=[jax.ShapeDtypeStruct((n, d), jnp.float32),
                   jax.ShapeDtypeStruct((nb, W_B - 1, D_B), jnp.float32),
                   jax.ShapeDtypeStruct((nb, W_C - 1, D_C), jnp.float32)],
        scratch_shapes=[pltpu.VMEM((HALO_B + tm, D_B), jnp.float32),
                        pltpu.VMEM((HALO_C + tm, D_C), jnp.float32),
                        pltpu.VMEM((tm, D_B), jnp.float32),
                        pltpu.VMEM((tm, D_C), jnp.float32)],
        compiler_params=_cparams(("arbitrary", "arbitrary")), name="mix_prompt")(
            mix, gsig, gsig, gsig, *consts)
    return out, cb, cc


def _mix_sample_kernel(mix_ref, g0_ref, g1_ref, g2_ref, pb_ref, pc_ref, lag_ref, lab_ref, ws0_ref, bs0_ref,
                       wcb_ref, wcc_ref, bdw_ref, lcg_ref, lcb_ref, wa_ref, wb_ref, wc_ref,
                       out_ref, v_ref, zb_ref, hc_ref):
    mix = mix_ref[...]
    ua, va = mix[:, 0:D_A], mix[:, D_A:2 * D_A]
    o = 2 * D_A
    bx, cx, xin = mix[:, o:o + D_B], mix[:, o + D_B:o + 2 * D_B], mix[:, o + 2 * D_B:o + 3 * D_B]
    o += 3 * D_B
    ca, cgate = mix[:, o:o + D_C], mix[:, o + D_C:o + 2 * D_C]

    v = _ln(va, lag_ref[...], lab_ref[...])
    v_ref[...] = v
    y_a = ua * (ws0_ref[...] * v + bs0_ref[...])

    zb = cx * xin
    zb_ref[...] = zb
    acc = wcb_ref[W_B - 1:W_B, :] * zb
    for k in range(W_B - 1):
        acc = acc + wcb_ref[k:k + 1, :] * pb_ref[k]
    y_b = bx * acc

    hc = ca * jax.nn.sigmoid(cgate)
    hc_ref[...] = hc
    acc = wcc_ref[W_C - 1:W_C, :] * hc
    for k in range(W_C - 1):
        acc = acc + wcc_ref[k:k + 1, :] * pc_ref[k]
    yc = _ln(acc + bdw_ref[...], lcg_ref[...], lcb_ref[...])
    y_c = yc * jax.nn.sigmoid(yc)

    pa = jnp.dot(y_a.astype(jnp.bfloat16), wa_ref[...], preferred_element_type=jnp.float32)
    pb = jnp.dot(y_b.astype(jnp.bfloat16), wb_ref[...], preferred_element_type=jnp.float32)
    pc = jnp.dot(y_c.astype(jnp.bfloat16), wc_ref[...], preferred_element_type=jnp.float32)
    out_ref[...] = (g0_ref[...].astype(jnp.float32) * pa + g1_ref[...].astype(jnp.float32) * pb
                    + g2_ref[...].astype(jnp.float32) * pc)


def _mix_sample(mix, gsig, prev_b, prev_c, lp):
    n = mix.shape[0]
    d = lp["w_br_a"].shape[1]
    gcol = lambda col: pl.BlockSpec((n, d), lambda i, col=col: (0, col))
    pbt = jnp.transpose(prev_b, (1, 0, 2))
    pct = jnp.transpose(prev_c, (1, 0, 2))
    consts = [lp["gmlp_ln_g"], lp["gmlp_ln_b"], lp["gmlp_ws0"], lp["gmlp_bs0"], lp["conv_b_w"],
              lp["conv_c_w"], lp["conv_c_bias"], lp["conf_ln_g"], lp["conf_ln_b"],
              lp["w_br_a"], lp["w_br_b"], lp["w_br_c"]]
    return pl.pallas_call(
        _mix_sample_kernel, grid=(1,),
        in_specs=[_const_spec(mix.shape), gcol(0), gcol(1), gcol(2), _const_spec(pbt.shape),
                  _const_spec(pct.shape)] + [_const_spec(c.shape) for c in consts],
        out_specs=[_const_spec((n, d)), _const_spec((n, D_A)), _const_spec((n, D_B)), _const_spec((n, D_C))],
        out_shape=[jax.ShapeDtypeStruct((n, d), jnp.float32), jax.ShapeDtypeStruct((n, D_A), jnp.float32),
                   jax.ShapeDtypeStruct((n, D_B), jnp.float32), jax.ShapeDtypeStruct((n, D_C), jnp.float32)],
        compiler_params=_cparams(("arbitrary",)), name="mix_sample")(
            mix, gsig, gsig, gsig, pbt, pct, *consts)


def _sort_key(score, visible):
    score = jnp.where(score == 0.0, 0.0, score)
    bits = pltpu.bitcast(score, jnp.int32)
    key = jnp.where(bits < 0, bits ^ jnp.int32(0x7FFFFFFF), bits)
    return jnp.where(visible, key, jnp.int32(INT_MIN))


KEY_CHUNK = 512


COUNT_ROWS = 64
HEADS_PER_KV = N_HEADS // N_KV
INT_MAX = 2 ** 31 - 1


def _dsa_prompt_kernel(q_ref, qi_ref, wt_ref, ki_ref, k_ref, v_ref, o_ref,
                       skey, ekey, bias, satt, mscr, lscr, oscr, *, n_top, idx_bits):
    i = pl.program_id(1)
    sub = KEY_CHUNK // Q_BLOCK
    nch = (i + sub) // sub
    rowck = lax.broadcasted_iota(jnp.int32, (KEY_CHUNK, Q_BLOCK), 0)
    qpos = i * Q_BLOCK + lax.broadcasted_iota(jnp.int32, (1, Q_BLOCK), 1)

    def score_chunk(c, _):
        r0 = pl.multiple_of(c * KEY_CHUNK, KEY_CHUNK)
        kc = ki_ref[0, pl.ds(r0, KEY_CHUNK), :]
        acc = jnp.zeros((KEY_CHUNK, Q_BLOCK), jnp.float32)
        for h in range(N_IDX_HEADS):
            dots = lax.dot_general(kc, qi_ref[0, h], _NT, preferred_element_type=jnp.float32)
            acc = acc + wt_ref[0, h:h + 1, :] * jnp.maximum(dots, 0.0)
        skey[pl.ds(r0, KEY_CHUNK), :] = _sort_key(acc, rowck <= qpos - r0)
        return 0

    lax.fori_loop(0, nch, score_chunk, 0)

    def count(src, pred_fn, extra=None):
        def body(c, acc):
            r0 = pl.multiple_of(c * KEY_CHUNK, KEY_CHUNK)
            s = src[pl.ds(r0, KEY_CHUNK), :]
            if extra is not None:
                extra(s, r0)
            m = pred_fn(s)
            return acc + m.reshape(KEY_CHUNK // COUNT_ROWS, COUNT_ROWS, Q_BLOCK).sum(axis=0)
        acc = lax.fori_loop(0, nch, body, jnp.zeros((COUNT_ROWS, Q_BLOCK), jnp.float32))
        return acc.sum(axis=0, keepdims=True)

    def thr_step(t, cur):
        cand = cur + lax.shift_left(jnp.int32(1), 31 - t)
        cnt = count(skey, lambda s: jnp.where(s >= cand, 1.0, 0.0))
        return jnp.where(cnt >= n_top, cand, cur)

    thr = lax.fori_loop(0, 32, thr_step, jnp.full((1, Q_BLOCK), INT_MIN, jnp.int32))

    def store_tie_index(s, r0):
        ekey[pl.ds(r0, KEY_CHUNK), :] = jnp.where(s == thr, rowck + r0, jnp.int32(INT_MAX))

    c_gt = count(skey, lambda s: jnp.where(s > thr, 1.0, 0.0), extra=store_tie_index)
    need = n_top - c_gt

    def cut_step(t, cur):
        cand = cur + lax.shift_left(jnp.int32(1), idx_bits - 1 - t)
        cnt = count(ekey, lambda e: jnp.where(e < cand, 1.0, 0.0))
        return jnp.where(cnt <= need, cand, cur)

    cut = lax.fori_loop(0, idx_bits, cut_step, jnp.zeros((1, Q_BLOCK), jnp.int32))
    cut = jnp.where(thr == INT_MIN, 0, cut)

    def bias_chunk(c, _):
        r0 = pl.multiple_of(c * KEY_CHUNK, KEY_CHUNK)
        s = skey[pl.ds(r0, KEY_CHUNK), :]
        e = ekey[pl.ds(r0, KEY_CHUNK), :]
        bt = jnp.where(s > thr, 0.0, jnp.where(e < cut, 0.0, NEG)).astype(jnp.float32)
        for jj in range(sub):
            bias[c * sub + jj] = bt[jj * Q_BLOCK:(jj + 1) * Q_BLOCK, :].T
        return 0

    lax.fori_loop(0, nch, bias_chunk, 0)

    for g in range(N_KV):
        qg = q_ref[0, g, 0]
        mscr[...] = jnp.full(mscr.shape, NEG, jnp.float32)
        lscr[...] = jnp.zeros(lscr.shape, jnp.float32)
        oscr[...] = jnp.zeros(oscr.shape, jnp.float32)

        def pass_a(c, _):
            r0 = pl.multiple_of(c * KEY_CHUNK, KEY_CHUNK)
            s = lax.dot_general(qg, k_ref[0, g, pl.ds(r0, KEY_CHUNK), :], _NT,
                                preferred_element_type=jnp.float32)
            for hh in range(HEADS_PER_KV):
                mx = mscr[hh]
                for jj in range(sub):
                    sj = s[hh * Q_BLOCK:(hh + 1) * Q_BLOCK, jj * Q_BLOCK:(jj + 1) * Q_BLOCK] + bias[c * sub + jj]
                    satt[c * sub + jj, hh] = sj
                    mx = jnp.maximum(mx, sj)
                mscr[hh] = mx
            return 0

        lax.fori_loop(0, nch, pass_a, 0)
        ms = [jnp.max(mscr[hh], axis=1, keepdims=True) for hh in range(HEADS_PER_KV)]

        def pass_b(c, _):
            r0 = pl.multiple_of(c * KEY_CHUNK, KEY_CHUNK)
            rows = []
            for hh in range(HEADS_PER_KV):
                ps = [jnp.exp(satt[c * sub + jj, hh] - ms[hh]) for jj in range(sub)]
                tot = ps[0]
                for p in ps[1:]:
                    tot = tot + p
                lscr[hh] += tot
                rows.append(jnp.concatenate([p.astype(jnp.bfloat16) for p in ps], axis=1))
            pc = jnp.concatenate(rows, axis=0)
            oscr[...] += jnp.dot(pc, v_ref[0, g, pl.ds(r0, KEY_CHUNK), :], preferred_element_type=jnp.float32)
            return 0

        lax.fori_loop(0, nch, pass_b, 0)
        for hh in range(HEADS_PER_KV):
            l = jnp.sum(lscr[hh], axis=1, keepdims=True)
            o_ref[0, g, 0, hh * Q_BLOCK:(hh + 1) * Q_BLOCK, :] = (
                oscr[hh * Q_BLOCK:(hh + 1) * Q_BLOCK, :] / l).astype(o_ref.dtype)


def _dsa_prompt(q_b, kv_b, qi_b, ki_b, wi, nb, t):
    n = q_b.shape[0]
    n_top = min(TOPK_MAX, t // 4)
    nqb = t // Q_BLOCK
    t_pad = ((t + KEY_CHUNK - 1) // KEY_CHUNK) * KEY_CHUNK
    idx_bits = max(1, (t_pad + 1).bit_length())
    rows_m = HEADS_PER_KV * Q_BLOCK
    qh = q_b.reshape(nb, nqb, Q_BLOCK, N_KV, HEADS_PER_KV, HEAD_DIM).transpose(0, 3, 1, 4, 2, 5)
    qh = qh.reshape(nb, N_KV, nqb, rows_m, HEAD_DIM)
    qih = qi_b.reshape(nb, t, N_IDX_HEADS, IDX_DIM).transpose(0, 2, 1, 3)
    kv = kv_b.reshape(nb, t, 2, N_KV, HEAD_DIM)
    kh = kv[:, :, 0].transpose(0, 2, 1, 3)
    vh = kv[:, :, 1].transpose(0, 2, 1, 3)
    if t_pad != t:
        padk = ((0, 0), (0, 0), (0, t_pad - t), (0, 0))
        kh, vh = jnp.pad(kh, padk), jnp.pad(vh, padk)
    kib = ki_b[:, :IDX_DIM].reshape(nb, t, IDX_DIM)
    if t_pad != t:
        kib = jnp.pad(kib, ((0, 0), (0, t_pad - t), (0, 0)))
    wt = jnp.pad(wi.reshape(nb, t, N_IDX_HEADS).transpose(0, 2, 1), ((0, 0), (0, SUBLANES - N_IDX_HEADS), (0, 0)))
    kern = functools.partial(_dsa_prompt_kernel, n_top=n_top, idx_bits=idx_bits)
    once = dict(pipeline_mode=pl.Buffered(1))
    nblk = t_pad // Q_BLOCK
    out = pl.pallas_call(
        kern, grid=(nb, nqb),
        in_specs=[pl.BlockSpec((1, N_KV, 1, rows_m, HEAD_DIM), lambda b, i: (b, 0, i, 0, 0)),
                  pl.BlockSpec((1, N_IDX_HEADS, Q_BLOCK, IDX_DIM), lambda b, i: (b, 0, i, 0)),
                  pl.BlockSpec((1, SUBLANES, Q_BLOCK), lambda b, i: (b, 0, i)),
                  pl.BlockSpec((1, t_pad, IDX_DIM), lambda b, i: (b, 0, 0), **once),
                  pl.BlockSpec((1, N_KV, t_pad, HEAD_DIM), lambda b, i: (b, 0, 0, 0), **once),
                  pl.BlockSpec((1, N_KV, t_pad, HEAD_DIM), lambda b, i: (b, 0, 0, 0), **once)],
        out_specs=pl.BlockSpec((1, N_KV, 1, rows_m, HEAD_DIM), lambda b, i: (b, 0, i, 0, 0)),
        out_shape=jax.ShapeDtypeStruct((nb, N_KV, nqb, rows_m, HEAD_DIM), jnp.bfloat16),
        scratch_shapes=[pltpu.VMEM((t_pad, Q_BLOCK), jnp.int32),
                        pltpu.VMEM((t_pad, Q_BLOCK), jnp.int32),
                        pltpu.VMEM((nblk, Q_BLOCK, Q_BLOCK), jnp.float32),
                        pltpu.VMEM((nblk, HEADS_PER_KV, Q_BLOCK, Q_BLOCK), jnp.float32),
                        pltpu.VMEM((HEADS_PER_KV, Q_BLOCK, Q_BLOCK), jnp.float32),
                        pltpu.VMEM((HEADS_PER_KV, Q_BLOCK, Q_BLOCK), jnp.float32),
                        pltpu.VMEM((rows_m, HEAD_DIM), jnp.float32)],
        compiler_params=_cparams(("arbitrary", "arbitrary")), name="dsa_prompt")(qh, qih, wt, kib, kh, vh)
    out = out.reshape(nb, N_KV, nqb, HEADS_PER_KV, Q_BLOCK, HEAD_DIM).transpose(0, 2, 4, 1, 3, 5)
    return out.reshape(n, Q_W)


def _page_unroll(n_pages):
    return 4 if n_pages % 4 == 0 else 1


def _dsa_s_score_kernel(pt_ref, qit_ref, w_ref, kin_ref, cache_ref, o_ref, buf, sem, *, layer):
    b = pl.program_id(0)
    n_pages = pt_ref.shape[1]
    unroll = _page_unroll(n_pages)

    def page_copy(j):
        return pltpu.make_async_copy(cache_ref.at[layer, pt_ref[b, j]], buf.at[j], sem)

    def start(j, _):
        page_copy(j).start()
        return 0

    def wait(j, _):
        page_copy(j).wait()
        return 0

    lax.fori_loop(0, n_pages, start, 0)
    lax.fori_loop(0, n_pages, wait, 0)
    qit = qit_ref[0]
    w = w_ref[0]
    qb = [jnp.broadcast_to(qit[:, h:h + 1], (IDX_DIM, PAGE)) for h in range(N_IDX_HEADS)]

    def page_scores(tile):
        s = jnp.zeros((1, PAGE), jnp.float32)
        for h in range(N_IDX_HEADS):
            dots = jnp.sum(tile * qb[h], axis=0, keepdims=True)
            s = s + w[:, h:h + 1] * jnp.maximum(dots, 0.0)
        return s

    def body(jo, _):
        for u in range(unroll):
            j = jo * unroll + u
            o_ref[0, pl.ds(j, 1), :] = page_scores(buf[j])
        return 0

    lax.fori_loop(0, n_pages // unroll, body, 0)
    d_new = jnp.sum(qit * kin_ref[0], axis=0, keepdims=True)
    s_new = jnp.sum(w * jnp.maximum(d_new, 0.0), axis=1, keepdims=True)
    lane = lax.broadcasted_iota(jnp.int32, (1, LANES), 1)
    o_ref[0, n_pages:n_pages + 1, :] = jnp.where(lane == 0, s_new, 0.0)


def _dsa_s_select_kernel(s_ref, o_ref, *, n_top, n_keys, idx_bits):
    s = s_ref[...]
    lane = lax.broadcasted_iota(jnp.int32, s.shape, 1)
    key = _sort_key(s, lane < n_keys)

    def thr_step(t, cur):
        cand = cur + lax.shift_left(jnp.int32(1), 31 - t)
        cnt = jnp.sum(jnp.where(key >= cand, 1.0, 0.0), axis=1, keepdims=True)
        return jnp.where(cnt >= n_top, cand, cur)

    thr = lax.fori_loop(0, 32, thr_step, jnp.full((s.shape[0], 1), INT_MIN, jnp.int32))
    need = n_top - jnp.sum(jnp.where(key > thr, 1.0, 0.0), axis=1, keepdims=True)

    def cut_step(t, cur):
        cand = cur + lax.shift_left(jnp.int32(1), idx_bits - 1 - t)
        cnt = jnp.sum(jnp.where(key == thr, jnp.where(lane < cand, 1.0, 0.0), 0.0), axis=1, keepdims=True)
        return jnp.where(cnt <= need, cand, cur)

    cut = lax.fori_loop(0, idx_bits, cut_step, jnp.zeros((s.shape[0], 1), jnp.int32))
    cut = jnp.where(thr == INT_MIN, 0, cut)
    sel = jnp.where(key > thr, 1, jnp.where(key == thr, jnp.where(lane < cut, 1, 0), 0))
    o_ref[...] = jnp.where(sel == 1, 0.0, NEG).astype(jnp.float32)


def _dsa_s_attn_kernel(pt_ref, qt_ref, bias_ref, kn_ref, vn_ref, ck_ref, cv_ref, o_ref,
                       kbuf, vbuf, sscr, sem_k, sem_v, *, layer):
    b = pl.program_id(0)
    n_pages = pt_ref.shape[1]
    unroll = _page_unroll(n_pages)

    def k_copy(j):
        return pltpu.make_async_copy(ck_ref.at[layer, pt_ref[b, j]], kbuf.at[j], sem_k)

    def v_copy(j):
        return pltpu.make_async_copy(cv_ref.at[layer, pt_ref[b, j]], vbuf.at[j], sem_v)

    def start(j, _):
        k_copy(j).start()
        v_copy(j).start()
        return 0

    def wait_k(j, _):
        k_copy(j).wait()
        return 0

    def wait_v(j, _):
        v_copy(j).wait()
        return 0

    lax.fori_loop(0, n_pages, start, 0)
    qt = qt_ref[0]
    qb = [jnp.broadcast_to(qt[:, h:h + 1], (HEAD_DIM, PAGE)) for h in range(N_HEADS)]
    rows_of = lambda h: slice((h // HEADS_PER_KV) * HEAD_DIM, (h // HEADS_PER_KV + 1) * HEAD_DIM)

    def page_s(tile, brow):
        rows = [jnp.sum(tile[rows_of(h), :] * qb[h], axis=0, keepdims=True) for h in range(N_HEADS)]
        return jnp.concatenate(rows, axis=0) + brow

    lax.fori_loop(0, n_pages, wait_k, 0)

    def pass1(jo, mx):
        for u in range(unroll):
            j = jo * unroll + u
            s = page_s(kbuf[j], bias_ref[0, pl.ds(j, 1), :])
            sscr[j] = s
            mx = jnp.maximum(mx, s)
        return mx

    mx = lax.fori_loop(0, n_pages // unroll, pass1, jnp.full((N_HEADS, PAGE), NEG, jnp.float32))
    s_new = page_s(kn_ref[0], bias_ref[0, n_pages:n_pages + 1, :])
    m = jnp.max(jnp.maximum(mx, s_new), axis=1, keepdims=True)

    def pass2(jo, lacc):
        for u in range(unroll):
            j = jo * unroll + u
            p = jnp.exp(sscr[j] - m)
            sscr[j] = p
            lacc = lacc + p
        return lacc

    p_new = jnp.exp(s_new - m)
    lacc = lax.fori_loop(0, n_pages // unroll, pass2, p_new)
    sscr[n_pages] = p_new
    lax.fori_loop(0, n_pages, wait_v, 0)

    cols = []
    for h in range(N_HEADS):
        def pass3(jo, acc):
            for u in range(unroll):
                j = jo * unroll + u
                acc = acc + vbuf[j, rows_of(h), :] * sscr[j, h:h + 1, :]
            return acc

        acc = lax.fori_loop(0, n_pages // unroll, pass3, vn_ref[0, rows_of(h), :] * p_new[h:h + 1, :])
        l = jnp.sum(lacc[h:h + 1, :], axis=1, keepdims=True)
        cols.append(jnp.sum(acc, axis=1, keepdims=True) / l)
    o_ref[0] = jnp.concatenate(cols, axis=1)


def _dsa_sample(h_att, wi, cache_kt, cache_vt, cache_kit, page_table, layer):
    nb = h_att.shape[0]
    n_pages = page_table.shape[1]
    n_keys = n_pages * PAGE + 1
    n_top = min(TOPK_MAX, n_keys // 4)
    lk = (n_pages + 1) * LANES
    q = h_att[:, :Q_W] * (HEAD_DIM ** -0.5)
    k_new = h_att[:, Q_W:Q_W + KV_W]
    v_new = h_att[:, Q_W + KV_W:Q_W + 2 * KV_W]
    qi = h_att[:, Q_W + 2 * KV_W:Q_W + 2 * KV_W + QI_W]
    ki_new = h_att[:, ATT_W - LANES:ATT_W - LANES + IDX_DIM]

    qit = jnp.pad(qi.reshape(nb, N_IDX_HEADS, IDX_DIM), ((0, 0), (0, SUBLANES - N_IDX_HEADS), (0, 0)))
    qit = qit.transpose(0, 2, 1)
    w8 = jnp.pad(wi, ((0, 0), (0, SUBLANES - N_IDX_HEADS)))[:, None, :]
    grid_spec = pltpu.PrefetchScalarGridSpec(
        num_scalar_prefetch=1, grid=(nb,),
        in_specs=[pl.BlockSpec((1, IDX_DIM, SUBLANES), lambda b, pt: (b, 0, 0)),
                  pl.BlockSpec((1, 1, SUBLANES), lambda b, pt: (b, 0, 0)),
                  pl.BlockSpec((1, IDX_DIM, 1), lambda b, pt: (b, 0, 0)),
                  pl.BlockSpec(memory_space=pl.ANY)],
        out_specs=pl.BlockSpec((1, n_pages + 1, LANES), lambda b, pt: (b, 0, 0)),
        scratch_shapes=[pltpu.VMEM((n_pages, IDX_DIM, PAGE), jnp.float32), pltpu.SemaphoreType.DMA])
    scores = pl.pallas_call(
        functools.partial(_dsa_s_score_kernel, layer=layer), grid_spec=grid_spec,
        out_shape=jax.ShapeDtypeStruct((nb, n_pages + 1, LANES), jnp.float32),
        compiler_params=_cparams(("arbitrary",)), name="dsa_sample_score")(
            page_table, qit, w8, ki_new[:, :, None], cache_kit)

    idx_bits = max(1, (lk + 1).bit_length())
    bias = pl.pallas_call(
        functools.partial(_dsa_s_select_kernel, n_top=n_top, n_keys=n_keys, idx_bits=idx_bits),
        grid=(1,), in_specs=[_const_spec((nb, lk))], out_specs=_const_spec((nb, lk)),
        out_shape=jax.ShapeDtypeStruct((nb, lk), jnp.float32),
        compiler_params=_cparams(("arbitrary",)), name="dsa_sample_select")(scores.reshape(nb, lk))

    qt = q.reshape(nb, N_HEADS, HEAD_DIM).transpose(0, 2, 1)
    first_col = ((0, 0), (0, 0), (0, PAGE - 1))
    grid_spec = pltpu.PrefetchScalarGridSpec(
        num_scalar_prefetch=1, grid=(nb,),
        in_specs=[pl.BlockSpec((1, HEAD_DIM, N_HEADS), lambda b, pt: (b, 0, 0)),
                  pl.BlockSpec((1, n_pages + 1, LANES), lambda b, pt: (b, 0, 0)),
                  pl.BlockSpec((1, KV_W, PAGE), lambda b, pt: (b, 0, 0)),
                  pl.BlockSpec((1, KV_W, PAGE), lambda b, pt: (b, 0, 0)),
                  pl.BlockSpec(memory_space=pl.ANY), pl.BlockSpec(memory_space=pl.ANY)],
        out_specs=pl.BlockSpec((1, HEAD_DIM, N_HEADS), lambda b, pt: (b, 0, 0)),
        scratch_shapes=[pltpu.VMEM((n_pages, KV_W, PAGE), jnp.float32),
                        pltpu.VMEM((n_pages, KV_W, PAGE), jnp.float32),
                        pltpu.VMEM((n_pages + 1, N_HEADS, PAGE), jnp.float32),
                        pltpu.SemaphoreType.DMA, pltpu.SemaphoreType.DMA])
    o_t = pl.pallas_call(
        functools.partial(_dsa_s_attn_kernel, layer=layer), grid_spec=grid_spec,
        out_shape=jax.ShapeDtypeStruct((nb, HEAD_DIM, N_HEADS), jnp.float32),
        compiler_params=_cparams(("arbitrary",)), name="dsa_sample_attn")(
            page_table, qt, bias.reshape(nb, n_pages + 1, LANES),
            jnp.pad(k_new[:, :, None], first_col), jnp.pad(v_new[:, :, None], first_col),
            cache_kt, cache_vt)
    return o_t.transpose(0, 2, 1).reshape(nb, Q_W)


def _post_kernel(m_ref, yd_ref, g3_ref, x_ref, wd_ref, wo_ref, lg_ref, lb_ref, o_ref, ob_ref, *, alpha):
    pd = jnp.dot(yd_ref[...], wd_ref[...], preferred_element_type=jnp.float32)
    merged = m_ref[...] + g3_ref[...].astype(jnp.float32) * pd
    mix = jnp.dot(merged.astype(jnp.bfloat16), wo_ref[...], preferred_element_type=jnp.float32)
    y = _ln(alpha * x_ref[...] + mix, lg_ref[...], lb_ref[...])
    o_ref[...] = y
    ob_ref[...] = y.astype(ob_ref.dtype)


def _post(merged_abc, yd_b, gsig, x, lp, alpha, tm):
    n, d = x.shape
    row = lambda w, col=0: pl.BlockSpec((tm, w), lambda i, col=col: (i, col))
    consts = [lp["w_br_d"], lp["w_o"], lp["ln1_g"], lp["ln1_b"]]
    return pl.pallas_call(
        functools.partial(_post_kernel, alpha=alpha), grid=(n // tm,),
        in_specs=[row(d), row(Q_W), row(d, N_BRANCHES - 1), row(d)] + [_const_spec(c.shape) for c in consts],
        out_specs=[row(d), row(d)],
        out_shape=[jax.ShapeDtypeStruct((n, d), jnp.float32), jax.ShapeDtypeStruct((n, d), jnp.bfloat16)],
        compiler_params=_cparams(("parallel",)), name="post")(merged_abc, yd_b, gsig, x, *consts)


def _ffn_kernel(xb_ref, x_ref, wg_ref, wu_ref, wd_ref, lg_ref, lb_ref, o_ref, ob_ref, acc, *, alpha):
    f = pl.program_id(1)

    @pl.when(f == 0)
    def _():
        acc[...] = jnp.zeros_like(acc)

    xb = xb_ref[...]
    hg = jnp.dot(xb, wg_ref[...], preferred_element_type=jnp.float32)
    hu = jnp.dot(xb, wu_ref[...], preferred_element_type=jnp.float32)
    h = (hg * jax.nn.sigmoid(hg) * hu).astype(jnp.bfloat16)
    acc[...] += jnp.dot(h, wd_ref[...], preferred_element_type=jnp.float32)

    @pl.when(f == pl.num_programs(1) - 1)
    def _():
        y = _ln(alpha * x_ref[...] + acc[...], lg_ref[...], lb_ref[...])
        o_ref[...] = y
        ob_ref[...] = y.astype(ob_ref.dtype)


def _ffn(xb, x, wg, wu, wd, lg, lb, alpha, tm):
    n, d = x.shape
    ff = wg.shape[1]
    tf = next(c for c in (512, 256, 128, ff) if ff % c == 0)
    row = pl.BlockSpec((tm, d), lambda i, f: (i, 0))
    return pl.pallas_call(
        functools.partial(_ffn_kernel, alpha=alpha), grid=(n // tm, ff // tf),
        in_specs=[row, row, pl.BlockSpec((d, tf), lambda i, f: (0, f)), pl.BlockSpec((d, tf), lambda i, f: (0, f)),
                  pl.BlockSpec((tf, d), lambda i, f: (f, 0)), _const_spec(lg.shape), _const_spec(lb.shape)],
        out_specs=[row, row],
        out_shape=[jax.ShapeDtypeStruct((n, d), jnp.float32), jax.ShapeDtypeStruct((n, d), jnp.bfloat16)],
        scratch_shapes=[pltpu.VMEM((tm, d), jnp.float32)],
        compiler_params=_cparams(("parallel", "arbitrary")), name="ffn")(xb, x, wg, wu, wd, lg, lb)


def _router_kernel(x_ref, r_ref, g_ref):
    lane = lax.broadcasted_iota(jnp.int32, g_ref.shape, 1)
    lanef = lane.astype(jnp.float32)
    logits = jnp.dot(x_ref[...], r_ref[...], preferred_element_type=jnp.float32, precision=lax.Precision.HIGHEST)
    logits = jnp.where(lane < N_EXPERTS, logits, -jnp.inf)
    m1 = jnp.max(logits, axis=1, keepdims=True)
    i1 = jnp.min(jnp.where(logits == m1, lanef, float(LANES)), axis=1, keepdims=True)
    rest = jnp.where(lanef == i1, -jnp.inf, logits)
    m2 = jnp.max(rest, axis=1, keepdims=True)
    i2 = jnp.min(jnp.where(rest == m2, lanef, float(LANES)), axis=1, keepdims=True)
    e2 = jnp.exp(m2 - m1)
    g_ref[...] = jnp.where(lanef == i1, 1.0 / (1.0 + e2), jnp.where(lanef == i2, e2 / (1.0 + e2), 0.0))


def _moe_kernel(cnt_ref, xb_ref, g_ref, rank_ref, rankt_ref, wg_ref, wu_ref, wd_ref, o_ref, xc, yc, *, rc):
    i, e, f = pl.program_id(0), pl.program_id(1), pl.program_id(2)
    tm, d = xb_ref.shape
    nck = (cnt_ref[i, e] + rc - 1) // rc
    half = d // 2 if d % (2 * LANES) == 0 else d

    @pl.when((e == 0) & (f == 0))
    def _():
        o_ref[...] = jnp.zeros_like(o_ref)

    @pl.when(f == 0)
    def _():
        rrow = rankt_ref[pl.ds(e, 1), :]
        riota = lax.broadcasted_iota(jnp.int32, (rc, tm), 0).astype(jnp.float32)

        def compact(k, _):
            r0 = pl.multiple_of(k * rc, rc)
            sel = jnp.where(rrow - r0.astype(jnp.float32) == riota, 1.0, 0.0).astype(jnp.bfloat16)
            xc[pl.ds(r0, rc), :] = jnp.dot(sel, xb_ref[...], preferred_element_type=jnp.float32).astype(xc.dtype)
            yc[pl.ds(r0, rc), :] = jnp.zeros((rc, d), jnp.float32)
            return 0

        lax.fori_loop(0, nck, compact, 0)

    def expert(k, _):
        r0 = pl.multiple_of(k * rc, rc)
        xk = xc[pl.ds(r0, rc), :]
        hg = jnp.dot(xk, wg_ref[0], preferred_element_type=jnp.float32)
        hu = jnp.dot(xk, wu_ref[0], preferred_element_type=jnp.float32)
        h = (hg * jax.nn.sigmoid(hg) * hu).astype(jnp.bfloat16)
        yc[pl.ds(r0, rc), :] += jnp.dot(h, wd_ref[0], preferred_element_type=jnp.float32)
        return 0

    lax.fori_loop(0, nck, expert, 0)

    @pl.when(f == pl.num_programs(2) - 1)
    def _():
        lane = lax.broadcasted_iota(jnp.int32, g_ref.shape, 1)
        rcol = jnp.sum(jnp.where(lane == e, rank_ref[...], 0.0), axis=1, keepdims=True)
        gcol = jnp.sum(jnp.where(lane == e, g_ref[...], 0.0), axis=1, keepdims=True)
        ciota = lax.broadcasted_iota(jnp.int32, (tm, rc), 1).astype(jnp.float32)

        def scatter(k, _):
            r0 = pl.multiple_of(k * rc, rc)
            selt = jnp.where(rcol - r0.astype(jnp.float32) == ciota, 1.0, 0.0).astype(jnp.bfloat16)
            for c0 in range(0, d, half):
                yk = yc[pl.ds(r0, rc), c0:c0 + half].astype(jnp.bfloat16)
                o_ref[:, c0:c0 + half] += gcol * jnp.dot(selt, yk, preferred_element_type=jnp.float32)
            return 0

        lax.fori_loop(0, nck, scatter, 0)


def _add_ln_kernel(x_ref, y_ref, lg_ref, lb_ref, o_ref, ob_ref, *, alpha):
    y = _ln(alpha * x_ref[...] + y_ref[...], lg_ref[...], lb_ref[...])
    o_ref[...] = y
    ob_ref[...] = y.astype(ob_ref.dtype)


def _moe(xb, x, router_p, wg, wu, wd, lg, lb, alpha):
    n, d = x.shape
    ne, _, ff = wg.shape
    tr = _tile(n, 1024)
    gate = pl.pallas_call(
        _router_kernel, grid=(n // tr,),
        in_specs=[pl.BlockSpec((tr, d), lambda i: (i, 0)), _const_spec(router_p.shape)],
        out_specs=pl.BlockSpec((tr, LANES), lambda i: (i, 0)),
        out_shape=jax.ShapeDtypeStruct((n, LANES), jnp.float32),
        compiler_params=_cparams(("parallel",)), name="moe_router")(x, router_p)

    tm = _tile(n, 2048)
    rc = -(-(tm * TOP_K * 9) // (ne * 8 * 2 * 16)) * 16
    n_rows = -(-tm // rc) * rc
    nt = n // tm
    routed = gate > 0.0
    rank = jnp.cumsum(routed.reshape(nt, tm, LANES).astype(jnp.int32), axis=1) - 1
    rank = jnp.where(routed.reshape(nt, tm, LANES), rank, -1).astype(jnp.float32)
    cnt = jnp.sum(routed.reshape(nt, tm, LANES)[:, :, :ne], axis=1).astype(jnp.int32)
    rank_t = rank[:, :, :SUBLANES].transpose(0, 2, 1).reshape(nt * SUBLANES, tm)
    rank = rank.reshape(n, LANES)

    tf = next(c for c in (512, 256, 128, ff) if ff % c == 0)
    row = lambda w: pl.BlockSpec((tm, w), lambda i, e, f, c: (i, 0))
    grid_spec = pltpu.PrefetchScalarGridSpec(
        num_scalar_prefetch=1, grid=(nt, ne, ff // tf),
        in_specs=[row(d), row(LANES), row(LANES),
                  pl.BlockSpec((SUBLANES, tm), lambda i, e, f, c: (i, 0)),
                  pl.BlockSpec((1, d, tf), lambda i, e, f, c: (e, 0, f)),
                  pl.BlockSpec((1, d, tf), lambda i, e, f, c: (e, 0, f)),
                  pl.BlockSpec((1, tf, d), lambda i, e, f, c: (e, f, 0))],
        out_specs=row(d),
        scratch_shapes=[pltpu.VMEM((n_rows, d), jnp.bfloat16), pltpu.VMEM((n_rows, d), jnp.float32)])
    y = pl.pallas_call(
        functools.partial(_moe_kernel, rc=rc), grid_spec=grid_spec,
        out_shape=jax.ShapeDtypeStruct((n, d), jnp.float32),
        compiler_params=_cparams(("arbitrary", "arbitrary", "arbitrary")), name="moe")(
            cnt, xb, gate, rank, rank_t, wg, wu, wd)

    ta = _tile(n, 1024)
    rowa = pl.BlockSpec((ta, d), lambda i: (i, 0))
    return pl.pallas_call(
        functools.partial(_add_ln_kernel, alpha=alpha), grid=(n // ta,),
        in_specs=[rowa, rowa, _const_spec(lg.shape), _const_spec(lb.shape)], out_specs=[rowa, rowa],
        out_shape=[jax.ShapeDtypeStruct((n, d), jnp.float32), jax.ShapeDtypeStruct((n, d), jnp.bfloat16)],
        compiler_params=_cparams(("parallel",)), name="add_ln")(x, y, lg, lb)


def _rope_tables(pos):
    rot = HEAD_DIM // ROPE_FRAC
    half = rot // 2
    freqs = jnp.power(ROPE_THETA, -jnp.arange(half, dtype=jnp.float32) / half)
    ang = pos.astype(jnp.float32)[:, None] * freqs
    cos, sin = jnp.cos(ang), jnp.sin(ang)
    t = pos.shape[0]
    ones = jnp.ones((t, HEAD_DIM - rot), jnp.float32)
    zeros = jnp.zeros((t, HEAD_DIM - rot), jnp.float32)
    zh = jnp.zeros((t, half), jnp.float32)
    c = jnp.concatenate([cos, cos, ones], axis=1)
    a = jnp.concatenate([-sin, zh, zeros], axis=1)
    b = jnp.concatenate([zh, sin, zeros], axis=1)
    rep = LANES // HEAD_DIM
    return jnp.tile(c, (1, rep)), jnp.tile(a, (1, rep)), jnp.tile(b, (1, rep))


def _layer_params(l, w_in, gmlp_ln_g, gmlp_ln_b, gmlp_ws, gmlp_bs, conv_b_w, conv_c_w, conv_c_bias, conf_ln_g,
                  conf_ln_b, w_br_a, w_br_b, w_br_c, w_br_d, w_o, ln1_g, ln1_b, ln2_g, ln2_b):
    bf = jnp.bfloat16
    w = w_in[l]
    d = w.shape[0]
    att_end = MIX_W + Q_W + 2 * KV_W + QI_W + IDX_DIM + N_IDX_HEADS
    w_att = jnp.pad(w[:, MIX_W:att_end], ((0, 0), (0, ATT_W - (att_end - MIX_W))))
    row = lambda a: a[l][None, :]
    gw = D_A // G_A
    return {
        "w_mix": w[:, :MIX_W].astype(bf), "w_att": w_att.astype(bf), "w_gat": w[:, att_end:].astype(bf),
        "gmlp_ln_g": row(gmlp_ln_g), "gmlp_ln_b": row(gmlp_ln_b), "gmlp_ws": gmlp_ws[l],
        "gmlp_bias_full": jnp.repeat(gmlp_bs[l].T, gw, axis=1),
        "gmlp_ws0": jnp.repeat(gmlp_ws[l][:, 0, 0], gw)[None, :],
        "gmlp_bs0": jnp.repeat(gmlp_bs[l][:, 0], gw)[None, :],
        "conv_b_w": conv_b_w[l], "conv_c_w": conv_c_w[l], "conv_c_bias": row(conv_c_bias),
        "conf_ln_g": row(conf_ln_g), "conf_ln_b": row(conf_ln_b),
        "w_br_a": w_br_a[l].astype(bf), "w_br_b": w_br_b[l].astype(bf), "w_br_c": w_br_c[l].astype(bf),
        "w_br_d": w_br_d[l].astype(bf), "w_o": w_o[l].astype(bf),
        "ln1_g": row(ln1_g), "ln1_b": row(ln1_b), "ln2_g": row(ln2_g), "ln2_b": row(ln2_b),
        "d": d,
    }


def _channel_mixer(l, xb, x, lp, ffn_w, moe_w, alpha, tm):
    if l % 2 == 0:
        wg, wu, wd = ffn_w
        j = l // 2
        return _ffn(xb, x, wg[j], wu[j], wd[j], lp["ln2_g"], lp["ln2_b"], alpha, tm)
    router, wg, wu, wd = moe_w
    j = l // 2
    return _moe(xb, x, router[j], wg[j], wu[j], wd[j], lp["ln2_g"], lp["ln2_b"], alpha)


def kernel(x_prompt, x_sample, cache_k, cache_v, cache_idx_k, state_conv_b, state_conv_c, page_table,
           w_in, gmlp_ln_g, gmlp_ln_b, gmlp_ws, gmlp_bs, conv_b_w, conv_c_w, conv_c_bias, conf_ln_g, conf_ln_b,
           w_br_a, w_br_b, w_br_c, w_br_d, w_o, ln1_g, ln1_b, ln2_g, ln2_b,
           ffn_w_gate, ffn_w_up, ffn_w_down, moe_router, moe_w_gate, moe_w_up, moe_w_down):
    bf = jnp.bfloat16
    nb, t, d = x_prompt.shape
    ns, ts, _ = x_sample.shape
    assert ts == 1 and t % Q_BLOCK == 0
    depth = w_in.shape[0]
    alpha = float((2 * depth) ** 0.25)
    past = page_table.shape[1] * PAGE

    ffn_w = (ffn_w_gate.astype(bf), ffn_w_up.astype(bf), ffn_w_down.astype(bf))
    router_p = jnp.pad(moe_router, ((0, 0), (0, 0), (0, LANES - N_EXPERTS)))
    moe_w = (router_p, moe_w_gate.astype(bf), moe_w_up.astype(bf), moe_w_down.astype(bf))

    rope_p = _rope_tables(jnp.arange(t, dtype=jnp.int32))
    rope_s = tuple(jnp.tile(r, (ns, 1)) for r in _rope_tables(past + jnp.arange(1, dtype=jnp.int32)))
    n_pool = cache_k.shape[1]
    cache_kt = cache_k.transpose(0, 1, 3, 4, 2).reshape(depth, n_pool, KV_W, PAGE)
    cache_vt = cache_v.transpose(0, 1, 3, 4, 2).reshape(depth, n_pool, KV_W, PAGE)
    cache_kit = cache_idx_k.transpose(0, 1, 3, 2)

    n = nb * t
    tm_p = _tile(t, 512)
    tm_f = _tile(n, 1024)
    xp = x_prompt.reshape(n, d)
    xs = x_sample.reshape(ns, d)
    xp_b, xs_b = xp.astype(bf), xs.astype(bf)
    outs = {k: [] for k in ("kp", "vp", "kip", "cbp", "ccp", "ks", "vs", "kis", "cbs", "ccs", "gvs")}
    for l in range(depth):
        lp = _layer_params(l, w_in, gmlp_ln_g, gmlp_ln_b, gmlp_ws, gmlp_bs, conv_b_w, conv_c_w, conv_c_bias,
                           conf_ln_g, conf_ln_b, w_br_a, w_br_b, w_br_c, w_br_d, w_o, ln1_g, ln1_b, ln2_g, ln2_b)
        mix, gsig, h_att, q_b, kv_b, qi_b, ki_b = _in_projection(
            xp_b, lp["w_mix"], lp["w_att"], lp["w_gat"], *rope_p, t // tm_p, tm_p)
        merged_abc, cb, cc = _mix_prompt(mix, gsig, lp, nb, t, tm_p)
        wi = h_att[:, ATT_W - LANES + IDX_DIM:ATT_W - LANES + IDX_DIM + N_IDX_HEADS]
        y_d = _dsa_prompt(q_b, kv_b, qi_b, ki_b, wi, nb, t)
        x1, x1_b = _post(merged_abc, y_d, gsig, xp, lp, alpha, tm_p)
        xp, xp_b = _channel_mixer(l, x1_b, x1, lp, ffn_w, moe_w, alpha, tm_f)
        outs["kp"].append(h_att[:, Q_W:Q_W + KV_W].reshape(nb, t, N_KV, HEAD_DIM))
        outs["vp"].append(h_att[:, Q_W + KV_W:Q_W + 2 * KV_W].reshape(nb, t, N_KV, HEAD_DIM))
        outs["kip"].append(h_att[:, ATT_W - LANES:ATT_W - LANES + IDX_DIM].reshape(nb, t, IDX_DIM))
        outs["cbp"].append(cb)
        outs["ccp"].append(cc)
        mix, gsig, h_att, _, _, _, _ = _in_projection(
            xs_b, lp["w_mix"], lp["w_att"], lp["w_gat"], *rope_s, 1, ns)
        merged_abc, v_rows, zb, hc = _mix_sample(mix, gsig, state_conv_b[l], state_conv_c[l], lp)
        wi = h_att[:, ATT_W - LANES + IDX_DIM:ATT_W - LANES + IDX_DIM + N_IDX_HEADS]
        y_d = _dsa_sample(h_att, wi, cache_kt, cache_vt, cache_kit, page_table, l)
        x1, x1_b = _post(merged_abc, y_d.astype(bf), gsig, xs, lp, alpha, ns)
        xs, xs_b = _channel_mixer(l, x1_b, x1, lp, ffn_w, moe_w, alpha, ns)
        outs["ks"].append(h_att[:, Q_W:Q_W + KV_W].reshape(ns, 1, N_KV, HEAD_DIM))
        outs["vs"].append(h_att[:, Q_W + KV_W:Q_W + 2 * KV_W].reshape(ns, 1, N_KV, HEAD_DIM))
        outs["kis"].append(h_att[:, ATT_W - LANES:ATT_W - LANES + IDX_DIM].reshape(ns, 1, IDX_DIM))
        outs["cbs"].append(jnp.concatenate([state_conv_b[l][:, 1:], zb[:, None, :]], axis=1))
        outs["ccs"].append(jnp.concatenate([state_conv_c[l][:, 1:], hc[:, None, :]], axis=1))
        outs["gvs"].append(v_rows[:, None, :])
    st = lambda k: jnp.stack(outs[k])
    return (xp.reshape(nb, t, d), xs.reshape(ns, 1, d), st("kp"), st("vp"), st("kip"), st("cbp"), st("ccp"),
            st("ks"), st("vs"), st("kis"), st("cbs"), st("ccs"), st("gvs"))
```

```python
import functools

import jax
import jax.numpy as jnp
from jax import lax
from jax.experimental import pallas as pl
from jax.experimental.pallas import tpu as pltpu

PAGE = 128
CHUNK = 128
D_A, G_A = 256, 4
D_B, W_B = 256, 3
D_C, W_C = 256, 31
N_HEADS, N_KV, HEAD_DIM = 8, 2, 64
N_IDX_HEADS, IDX_DIM = 4, 64
TOPK_MAX = 256
Q_BLOCK = 128
ROPE_THETA = 500000.0
ROPE_FRAC = 4
N_EXPERTS, TOP_K = 8, 2
N_BRANCHES = 4
LN_EPS = 1e-5

LANES = 128
SUBLANES = 8
VMEM_LIMIT = 56 * 1024 * 1024

MIX_W = 2 * D_A + 3 * D_B + 2 * D_C
Q_W, KV_W, QI_W = N_HEADS * HEAD_DIM, N_KV * HEAD_DIM, N_IDX_HEADS * IDX_DIM
ATT_W = Q_W + 2 * KV_W + QI_W + LANES
INT_MIN = -2 ** 31
NEG = -1e30

_NT = (((1,), (1,)), ((), ()))


def _tile(n, pref):
    t = min(pref, n)
    while t >= 16:
        if n % t == 0 and t % 16 == 0:
            return t
        t -= 16
    return n


def _cparams(sem):
    return pltpu.CompilerParams(dimension_semantics=sem, vmem_limit_bytes=VMEM_LIMIT)


def _ln(x, g, b):
    mu = jnp.mean(x, axis=-1, keepdims=True)
    xc = x - mu
    var = jnp.mean(xc * xc, axis=-1, keepdims=True)
    return xc * lax.rsqrt(var + LN_EPS) * g + b


def _const_spec(shape):
    nd = len(shape)
    return pl.BlockSpec(shape, lambda *_: (0,) * nd)


def _proj_mix_kernel(x_ref, w_ref, o_ref):
    o_ref[...] = lax.dot_general(x_ref[...], w_ref[...], _NT, preferred_element_type=jnp.float32)


def _proj_gates_kernel(x_ref, w_ref, o_ref):
    h = lax.dot_general(x_ref[...], w_ref[...], _NT, preferred_element_type=jnp.float32)
    o_ref[...] = jax.nn.sigmoid(h).astype(o_ref.dtype)


def _proj_attn_kernel(x_ref, w_ref, c_ref, a_ref, b_ref, h_ref, q_ref, kv_ref, qi_ref, ki_ref):
    h = lax.dot_general(x_ref[...], w_ref[...], _NT, preferred_element_type=jnp.float32)
    cc, aa, bb = c_ref[...], a_ref[...], b_ref[...]
    lane = lax.broadcasted_iota(jnp.int32, cc.shape, 1)
    n_grp = ATT_W // LANES
    v_grp = (Q_W + KV_W) // LANES
    outs = []
    for gi in range(n_grp):
        xg = h[:, gi * LANES:(gi + 1) * LANES]
        if gi == v_grp:
            outs.append(xg)
            continue
        rot = xg * cc + pltpu.roll(xg, LANES - 8, 1) * aa + pltpu.roll(xg, 8, 1) * bb
        if gi == n_grp - 1:
            wi_scale = float(N_IDX_HEADS * IDX_DIM) ** -0.5
            rot = jnp.where(lane < IDX_DIM, rot, xg * wi_scale)
        outs.append(rot)
    hr = jnp.concatenate(outs, axis=1)
    h_ref[...] = hr
    q_ref[...] = (hr[:, :Q_W] * (HEAD_DIM ** -0.5)).astype(q_ref.dtype)
    kv_ref[...] = hr[:, Q_W:Q_W + 2 * KV_W].astype(kv_ref.dtype)
    qi_ref[...] = hr[:, Q_W + 2 * KV_W:Q_W + 2 * KV_W + QI_W].astype(qi_ref.dtype)
    ki_ref[...] = hr[:, ATT_W - LANES:].astype(ki_ref.dtype)


def _in_projection(xb, w_mix, w_att, w_gat, rope_c, rope_a, rope_b, n_pos_tiles, tm):
    n, d = xb.shape
    grid = (n // tm,)
    row = lambda w: pl.BlockSpec((tm, w), lambda i: (i, 0))
    mix = pl.pallas_call(
        _proj_mix_kernel, grid=grid,
        in_specs=[row(d), _const_spec(w_mix.shape)], out_specs=row(MIX_W),
        out_shape=jax.ShapeDtypeStruct((n, MIX_W), jnp.float32),
        compiler_params=_cparams(("parallel",)), name="proj_mix")(xb, w_mix)
    gw = w_gat.shape[0]
    gsig = pl.pallas_call(
        _proj_gates_kernel, grid=grid,
        in_specs=[row(d), _const_spec(w_gat.shape)], out_specs=row(gw),
        out_shape=jax.ShapeDtypeStruct((n, gw), jnp.bfloat16),
        compiler_params=_cparams(("parallel",)), name="proj_gates")(xb, w_gat)
    tab = pl.BlockSpec((tm, LANES), lambda i: (i % n_pos_tiles, 0))
    h_att, q_b, kv_b, qi_b, ki_b = pl.pallas_call(
        _proj_attn_kernel, grid=grid,
        in_specs=[row(d), _const_spec(w_att.shape), tab, tab, tab],
        out_specs=[row(ATT_W), row(Q_W), row(2 * KV_W), row(QI_W), row(LANES)],
        out_shape=[jax.ShapeDtypeStruct((n, ATT_W), jnp.float32),
                   jax.ShapeDtypeStruct((n, Q_W), jnp.bfloat16),
                   jax.ShapeDtypeStruct((n, 2 * KV_W), jnp.bfloat16),
                   jax.ShapeDtypeStruct((n, QI_W), jnp.bfloat16),
                   jax.ShapeDtypeStruct((n, LANES), jnp.bfloat16)],
        compiler_params=_cparams(("parallel",)), name="proj_attn")(xb, w_att, rope_c, rope_a, rope_b)
    return mix, gsig, h_att, q_b, kv_b, qi_b, ki_b


HALO_B, HALO_C = 8, 32
CONV_ROWS = 64


def _mix_prompt_kernel(mix_ref, g0_ref, g1_ref, g2_ref, lag_ref, lab_ref, ws_ref, bias_ref,
                       wcb_ref, wcc_ref, bdw_ref, lcg_ref, lcb_ref, wa_ref, wb_ref, wc_ref,
                       out_ref, cb_ref, cc_ref, ext_b, ext_c, conv_b, conv_c):
    tm = mix_ref.shape[0]
    j = pl.program_id(1)

    @pl.when(j == 0)
    def _():
        ext_b[0:HALO_B, :] = jnp.zeros((HALO_B, D_B), jnp.float32)
        ext_c[0:HALO_C, :] = jnp.zeros((HALO_C, D_C), jnp.float32)

    mix = mix_ref[...]
    ua, va = mix[:, 0:D_A], mix[:, D_A:2 * D_A]
    o = 2 * D_A
    bx, cx, xin = mix[:, o:o + D_B], mix[:, o + D_B:o + 2 * D_B], mix[:, o + 2 * D_B:o + 3 * D_B]
    o += 3 * D_B
    ca, cgate = mix[:, o:o + D_C], mix[:, o + D_C:o + 2 * D_C]

    v = _ln(va, lag_ref[...], lab_ref[...])
    vb = v.astype(jnp.bfloat16)
    lane = lax.broadcasted_iota(jnp.int32, (CHUNK, D_A), 1)
    r_i = lax.broadcasted_iota(jnp.int32, (CHUNK, CHUNK), 0)
    c_i = lax.broadcasted_iota(jnp.int32, (CHUNK, CHUNK), 1)
    gw = D_A // G_A
    s_rows = []
    for c in range(tm // CHUNK):
        vc = vb[c * CHUNK:(c + 1) * CHUNK, :]
        sc = bias_ref[...]
        for g in range(G_A):
            wsg = jnp.where(c_i <= r_i, ws_ref[g], 0.0).astype(jnp.bfloat16)
            sg = jnp.dot(wsg, vc, preferred_element_type=jnp.float32)
            sc = sc + jnp.where((lane >= g * gw) & (lane < (g + 1) * gw), sg, 0.0)
        s_rows.append(sc)
    y_a = ua * jnp.concatenate(s_rows, axis=0)

    zb = cx * xin
    hc = ca * jax.nn.sigmoid(cgate)
    ext_b[HALO_B:HALO_B + tm, :] = zb
    ext_c[HALO_C:HALO_C + tm, :] = hc

    for r in range(tm // CONV_ROWS):
        r0 = r * CONV_ROWS
        acc = jnp.zeros((CONV_ROWS, D_B), jnp.float32)
        for d in range(W_B):
            acc = acc + wcb_ref[W_B - 1 - d:W_B - d, :] * ext_b[r0 + HALO_B - d:r0 + HALO_B - d + CONV_ROWS, :]
        conv_b[r0:r0 + CONV_ROWS, :] = acc
        acc = jnp.zeros((CONV_ROWS, D_C), jnp.float32)
        for d in range(W_C):
            acc = acc + wcc_ref[W_C - 1 - d:W_C - d, :] * ext_c[r0 + HALO_C - d:r0 + HALO_C - d + CONV_ROWS, :]
        conv_c[r0:r0 + CONV_ROWS, :] = acc
    y_b = bx * conv_b[...]
    yc = _ln(conv_c[...] + bdw_ref[...], lcg_ref[...], lcb_ref[...])
    y_c = yc * jax.nn.sigmoid(yc)

    ext_b[0:HALO_B, :] = zb[tm - HALO_B:, :]
    ext_c[0:HALO_C, :] = hc[tm - HALO_C:, :]
    cb_ref[0] = zb[tm - (W_B - 1):, :]
    cc_ref[0] = hc[tm - (W_C - 1):, :]

    pa = jnp.dot(y_a.astype(jnp.bfloat16), wa_ref[...], preferred_element_type=jnp.float32)
    pb = jnp.dot(y_b.astype(jnp.bfloat16), wb_ref[...], preferred_element_type=jnp.float32)
    pc = jnp.dot(y_c.astype(jnp.bfloat16), wc_ref[...], preferred_element_type=jnp.float32)
    out_ref[...] = (g0_ref[...].astype(jnp.float32) * pa + g1_ref[...].astype(jnp.float32) * pb
                    + g2_ref[...].astype(jnp.float32) * pc)


def _mix_prompt(mix, gsig, lp, nb, t, tm):
    n = mix.shape[0]
    d = lp["w_br_a"].shape[1]
    nt = t // tm
    row = lambda w, col=0: pl.BlockSpec((tm, w), lambda b, j, col=col: (b * nt + j, col))
    consts = [lp["gmlp_ln_g"], lp["gmlp_ln_b"], lp["gmlp_ws"], lp["gmlp_bias_full"], lp["conv_b_w"],
              lp["conv_c_w"], lp["conv_c_bias"], lp["conf_ln_g"], lp["conf_ln_b"],
              lp["w_br_a"], lp["w_br_b"], lp["w_br_c"]]
    out, cb, cc = pl.pallas_call(
        _mix_prompt_kernel, grid=(nb, nt),
        in_specs=[row(MIX_W), row(d, 0), row(d, 1), row(d, 2)] + [_const_spec(c.shape) for c in consts],
        out_specs=[row(d),
                   pl.BlockSpec((1, W_B - 1, D_B), lambda b, j: (b, 0, 0)),
                   pl.BlockSpec((1, W_C - 1, D_C), lambda b, j: (b, 0, 0))],
        out_shape=[jax.ShapeDtypeStruct((n, d), jnp.float32),
                   jax.ShapeDtypeStruct((nb, W_B - 1, D_B), jnp.float32),
                   jax.ShapeDtypeStruct((nb, W_C - 1, D_C), jnp.float32)],
        scratch_shapes=[pltpu.VMEM((HALO_B + tm, D_B), jnp.float32),
                        pltpu.VMEM((HALO_C + tm, D_C), jnp.float32),
                        pltpu.VMEM((tm, D_B), jnp.float32),
                        pltpu.VMEM((tm, D_C), jnp.float32)],
        compiler_params=_cparams(("arbitrary", "arbitrary")), name="mix_prompt")(
            mix, gsig, gsig, gsig, *consts)
    return out, cb, cc


def _mix_sample_kernel(mix_ref, g0_ref, g1_ref, g2_ref, pb_ref, pc_ref, lag_ref, lab_ref, ws0_ref, bs0_ref,
                       wcb_ref, wcc_ref, bdw_ref, lcg_ref, lcb_ref, wa_ref, wb_ref, wc_ref,
                       out_ref, v_ref, zb_ref, hc_ref):
    mix = mix_ref[...]
    ua, va = mix[:, 0:D_A], mix[:, D_A:2 * D_A]
    o = 2 * D_A
    bx, cx, xin = mix[:, o:o + D_B], mix[:, o + D_B:o + 2 * D_B], mix[:, o + 2 * D_B:o + 3 * D_B]
    o += 3 * D_B
    ca, cgate = mix[:, o:o + D_C], mix[:, o + D_C:o + 2 * D_C]

    v = _ln(va, lag_ref[...], lab_ref[...])
    v_ref[...] = v
    y_a = ua * (ws0_ref[...] * v + bs0_ref[...])

    zb = cx * xin
    zb_ref[...] = zb
    acc = wcb_ref[W_B - 1:W_B, :] * zb
    for k in range(W_B - 1):
        acc = acc + wcb_ref[k:k + 1, :] * pb_ref[k]
    y_b = bx * acc

    hc = ca * jax.nn.sigmoid(cgate)
    hc_ref[...] = hc
    acc = wcc_ref[W_C - 1:W_C, :] * hc
    for k in range(W_C - 1):
        acc = acc + wcc_ref[k:k + 1, :] * pc_ref[k]
    yc = _ln(acc + bdw_ref[...], lcg_ref[...], lcb_ref[...])
    y_c = yc * jax.nn.sigmoid(yc)

    pa = jnp.dot(y_a.astype(jnp.bfloat16), wa_ref[...], preferred_element_type=jnp.float32)
    pb = jnp.dot(y_b.astype(jnp.bfloat16), wb_ref[...], preferred_element_type=jnp.float32)
    pc = jnp.dot(y_c.astype(jnp.bfloat16), wc_ref[...], preferred_element_type=jnp.float32)
    out_ref[...] = (g0_ref[...].astype(jnp.float32) * pa + g1_ref[...].astype(jnp.float32) * pb
                    + g2_ref[...].astype(jnp.float32) * pc)


def _mix_sample(mix, gsig, prev_b, prev_c, lp):
    n = mix.shape[0]
    d = lp["w_br_a"].shape[1]
    gcol = lambda col: pl.BlockSpec((n, d), lambda i, col=col: (0, col))
    pbt = jnp.transpose(prev_b, (1, 0, 2))
    pct = jnp.transpose(prev_c, (1, 0, 2))
    consts = [lp["gmlp_ln_g"], lp["gmlp_ln_b"], lp["gmlp_ws0"], lp["gmlp_bs0"], lp["conv_b_w"],
              lp["conv_c_w"], lp["conv_c_bias"], lp["conf_ln_g"], lp["conf_ln_b"],
              lp["w_br_a"], lp["w_br_b"], lp["w_br_c"]]
    return pl.pallas_call(
        _mix_sample_kernel, grid=(1,),
        in_specs=[_const_spec(mix.shape), gcol(0), gcol(1), gcol(2), _const_spec(pbt.shape),
                  _const_spec(pct.shape)] + [_const_spec(c.shape) for c in consts],
        out_specs=[_const_spec((n, d)), _const_spec((n, D_A)), _const_spec((n, D_B)), _const_spec((n, D_C))],
        out_shape=[jax.ShapeDtypeStruct((n, d), jnp.float32), jax.ShapeDtypeStruct((n, D_A), jnp.float32),
                   jax.ShapeDtypeStruct((n, D_B), jnp.float32), jax.ShapeDtypeStruct((n, D_C), jnp.float32)],
        compiler_params=_cparams(("arbitrary",)), name="mix_sample")(
            mix, gsig, gsig, gsig, pbt, pct, *consts)


def _sort_key(score, visible):
    score = jnp.where(score == 0.0, 0.0, score)
    bits = pltpu.bitcast(score, jnp.int32)
    key = jnp.where(bits < 0, bits ^ jnp.int32(0x7FFFFFFF), bits)
    return jnp.where(visible, key, jnp.int32(INT_MIN))


KEY_CHUNK = 512


COUNT_ROWS = 64
HEADS_PER_KV = N_HEADS // N_KV
COUNT16_ROWS = 128
I16_MIN = -2 ** 15


def _dsa_prompt_kernel(q_ref, qi_ref, wt_ref, ki_ref, k_ref, v_ref, o_ref,
                       skey, hkey, lkey, bias, satt, mscr, lscr, oscr, *, n_top):
    i = pl.program_id(1)
    sub = KEY_CHUNK // Q_BLOCK
    nch = (i + sub) // sub
    rowck = lax.broadcasted_iota(jnp.int32, (KEY_CHUNK, Q_BLOCK), 0)
    qpos = i * Q_BLOCK + lax.broadcasted_iota(jnp.int32, (1, Q_BLOCK), 1)

    def for_chunks(body, init):
        carry = lax.fori_loop(0, nch // 2, lambda c, x: body(2 * c + 1, body(2 * c, x)), init)
        return lax.cond(nch % 2 == 1, lambda x: body(nch - 1, x), lambda x: x, carry)

    def score_chunk(c, carry):
        r0 = pl.multiple_of(c * KEY_CHUNK, KEY_CHUNK)
        kc = ki_ref[0, pl.ds(r0, KEY_CHUNK), :]
        acc = jnp.zeros((KEY_CHUNK, Q_BLOCK), jnp.float32)
        for h in range(N_IDX_HEADS):
            dots = lax.dot_general(kc, qi_ref[0, h], _NT, preferred_element_type=jnp.float32)
            acc = acc + wt_ref[0, h:h + 1, :] * jnp.maximum(dots, 0.0)
        key = _sort_key(acc, rowck <= qpos - r0)
        skey[pl.ds(r0, KEY_CHUNK), :] = key
        hkey[pl.ds(r0, KEY_CHUNK), :] = lax.shift_right_arithmetic(key, 16).astype(jnp.int16)
        return carry

    for_chunks(score_chunk, 0)

    def count(pred_fn, extra=None):
        def body(c, acc):
            r0 = pl.multiple_of(c * KEY_CHUNK, KEY_CHUNK)
            s = skey[pl.ds(r0, KEY_CHUNK), :]
            if extra is not None:
                extra(s, r0)
            m = pred_fn(s)
            return acc + m.reshape(KEY_CHUNK // COUNT_ROWS, COUNT_ROWS, Q_BLOCK).sum(axis=0)
        acc = lax.fori_loop(0, nch, body, jnp.zeros((COUNT_ROWS, Q_BLOCK), jnp.float32))
        return acc.sum(axis=0, keepdims=True)

    def search16(src, kk):
        def count16(cand):
            c16 = cand.astype(jnp.int16)

            def body(c, acc):
                r0 = pl.multiple_of(c * KEY_CHUNK, KEY_CHUNK)
                m = jnp.where(src[pl.ds(r0, KEY_CHUNK), :] >= c16, jnp.int16(1), jnp.int16(0))
                for r in range(0, KEY_CHUNK, COUNT16_ROWS):
                    acc = acc + m[r:r + COUNT16_ROWS, :]
                return acc
            acc = lax.fori_loop(0, nch, body, jnp.zeros((COUNT16_ROWS, Q_BLOCK), jnp.int16))
            return acc.astype(jnp.float32).sum(axis=0, keepdims=True)

        def step(t, cur):
            cand = cur + lax.shift_left(jnp.int32(1), 15 - t)
            return jnp.where(count16(cand) >= kk, cand, cur)

        return lax.fori_loop(0, 16, step, jnp.full((1, Q_BLOCK), I16_MIN, jnp.int32))

    thr_hi = search16(hkey, n_top)

    def store_low_bits(s, r0):
        low = (s & jnp.int32(0xFFFF)) + I16_MIN
        same = lax.shift_right_arithmetic(s, 16) == thr_hi
        lkey[pl.ds(r0, KEY_CHUNK), :] = jnp.where(same, low, I16_MIN).astype(jnp.int16)

    above = count(lambda s: jnp.where(lax.shift_right_arithmetic(s, 16) > thr_hi, 1.0, 0.0), extra=store_low_bits)
    thr_lo = search16(lkey, n_top - above)
    thr = lax.shift_left(thr_hi, 16) | (thr_lo - I16_MIN)
    need = jnp.where(thr == INT_MIN, 0.0, n_top - count(lambda s: jnp.where(s > thr, 1.0, 0.0)))

    row128 = lax.broadcasted_iota(jnp.int32, (Q_BLOCK, Q_BLOCK), 0)
    col128 = lax.broadcasted_iota(jnp.int32, (Q_BLOCK, Q_BLOCK), 1)
    before = jnp.where(col128 < row128, 1.0, 0.0).astype(jnp.bfloat16)

    def bias_chunk(c, seen):
        r0 = pl.multiple_of(c * KEY_CHUNK, KEY_CHUNK)
        s = skey[pl.ds(r0, KEY_CHUNK), :]
        for jj in range(sub):
            sj = s[jj * Q_BLOCK:(jj + 1) * Q_BLOCK, :]
            tie = jnp.where(sj == thr, 1.0, 0.0)
            rank = jnp.dot(before, tie.astype(jnp.bfloat16), preferred_element_type=jnp.float32) + seen
            bt = jnp.where(sj > thr, 0.0, jnp.where(sj == thr, jnp.where(rank < need, 0.0, NEG), NEG))
            bias[c * sub + jj] = bt.T
            seen = seen + jnp.sum(tie, axis=0, keepdims=True)
        return seen

    for_chunks(bias_chunk, jnp.zeros((1, Q_BLOCK), jnp.float32))

    for g in range(N_KV):
        qg = q_ref[0, g, 0]
        mscr[...] = jnp.full(mscr.shape, NEG, jnp.float32)
        lscr[...] = jnp.zeros(lscr.shape, jnp.float32)
        oscr[...] = jnp.zeros(oscr.shape, jnp.float32)

        def pass_a(c, carry):
            r0 = pl.multiple_of(c * KEY_CHUNK, KEY_CHUNK)
            s = lax.dot_general(qg, k_ref[0, g, pl.ds(r0, KEY_CHUNK), :], _NT,
                                preferred_element_type=jnp.float32)
            for hh in range(HEADS_PER_KV):
                mx = mscr[hh]
                for jj in range(sub):
                    sj = s[hh * Q_BLOCK:(hh + 1) * Q_BLOCK, jj * Q_BLOCK:(jj + 1) * Q_BLOCK] + bias[c * sub + jj]
                    satt[c * sub + jj, hh] = sj
                    mx = jnp.maximum(mx, sj)
                mscr[hh] = mx
            return carry

        for_chunks(pass_a, 0)
        ms = [jnp.max(mscr[hh], axis=1, keepdims=True) for hh in range(HEADS_PER_KV)]

        def pass_b(c, carry):
            r0 = pl.multiple_of(c * KEY_CHUNK, KEY_CHUNK)
            rows = []
            for hh in range(HEADS_PER_KV):
                ps = [jnp.exp(satt[c * sub + jj, hh] - ms[hh]) for jj in range(sub)]
                tot = ps[0]
                for p in ps[1:]:
                    tot = tot + p
                lscr[hh] += tot
                rows.append(jnp.concatenate([p.astype(jnp.bfloat16) for p in ps], axis=1))
            pc = jnp.concatenate(rows, axis=0)
            oscr[...] += jnp.dot(pc, v_ref[0, g, pl.ds(r0, KEY_CHUNK), :], preferred_element_type=jnp.float32)
            return carry

        for_chunks(pass_b, 0)
        for hh in range(HEADS_PER_KV):
            l = jnp.sum(lscr[hh], axis=1, keepdims=True)
            o_ref[0, g, 0, hh * Q_BLOCK:(hh + 1) * Q_BLOCK, :] = (
                oscr[hh * Q_BLOCK:(hh + 1) * Q_BLOCK, :] / l).astype(o_ref.dtype)


def _dsa_prompt(q_b, kv_b, qi_b, ki_b, wi, nb, t):
    n = q_b.shape[0]
    n_top = min(TOPK_MAX, t // 4)
    nqb = t // Q_BLOCK
    t_pad = ((t + KEY_CHUNK - 1) // KEY_CHUNK) * KEY_CHUNK
    rows_m = HEADS_PER_KV * Q_BLOCK
    qh = q_b.reshape(nb, nqb, Q_BLOCK, N_KV, HEADS_PER_KV, HEAD_DIM).transpose(0, 3, 1, 4, 2, 5)
    qh = qh.reshape(nb, N_KV, nqb, rows_m, HEAD_DIM)
    qih = qi_b.reshape(nb, t, N_IDX_HEADS, IDX_DIM).transpose(0, 2, 1, 3)
    kv = kv_b.reshape(nb, t, 2, N_KV, HEAD_DIM)
    kh = kv[:, :, 0].transpose(0, 2, 1, 3)
    vh = kv[:, :, 1].transpose(0, 2, 1, 3)
    if t_pad != t:
        padk = ((0, 0), (0, 0), (0, t_pad - t), (0, 0))
        kh, vh = jnp.pad(kh, padk), jnp.pad(vh, padk)
    kib = ki_b[:, :IDX_DIM].reshape(nb, t, IDX_DIM)
    if t_pad != t:
        kib = jnp.pad(kib, ((0, 0), (0, t_pad - t), (0, 0)))
    wt = jnp.pad(wi.reshape(nb, t, N_IDX_HEADS).transpose(0, 2, 1), ((0, 0), (0, SUBLANES - N_IDX_HEADS), (0, 0)))
    kern = functools.partial(_dsa_prompt_kernel, n_top=n_top)
    once = dict(pipeline_mode=pl.Buffered(1))
    nblk = t_pad // Q_BLOCK
    out = pl.pallas_call(
        kern, grid=(nb, nqb),
        in_specs=[pl.BlockSpec((1, N_KV, 1, rows_m, HEAD_DIM), lambda b, i: (b, 0, i, 0, 0)),
                  pl.BlockSpec((1, N_IDX_HEADS, Q_BLOCK, IDX_DIM), lambda b, i: (b, 0, i, 0)),
                  pl.BlockSpec((1, SUBLANES, Q_BLOCK), lambda b, i: (b, 0, i)),
                  pl.BlockSpec((1, t_pad, IDX_DIM), lambda b, i: (b, 0, 0), **once),
                  pl.BlockSpec((1, N_KV, t_pad, HEAD_DIM), lambda b, i: (b, 0, 0, 0), **once),
                  pl.BlockSpec((1, N_KV, t_pad, HEAD_DIM), lambda b, i: (b, 0, 0, 0), **once)],
        out_specs=pl.BlockSpec((1, N_KV, 1, rows_m, HEAD_DIM), lambda b, i: (b, 0, i, 0, 0)),
        out_shape=jax.ShapeDtypeStruct((nb, N_KV, nqb, rows_m, HEAD_DIM), jnp.bfloat16),
        scratch_shapes=[pltpu.VMEM((t_pad, Q_BLOCK), jnp.int32),
                        pltpu.VMEM((t_pad, Q_BLOCK), jnp.int16),
                        pltpu.VMEM((t_pad, Q_BLOCK), jnp.int16),
                        pltpu.VMEM((nblk, Q_BLOCK, Q_BLOCK), jnp.float32),
                        pltpu.VMEM((nblk, HEADS_PER_KV, Q_BLOCK, Q_BLOCK), jnp.float32),
                        pltpu.VMEM((HEADS_PER_KV, Q_BLOCK, Q_BLOCK), jnp.float32),
                        pltpu.VMEM((HEADS_PER_KV, Q_BLOCK, Q_BLOCK), jnp.float32),
                        pltpu.VMEM((rows_m, HEAD_DIM), jnp.float32)],
        compiler_params=_cparams(("arbitrary", "arbitrary")), name="dsa_prompt")(qh, qih, wt, kib, kh, vh)
    out = out.reshape(nb, N_KV, nqb, HEADS_PER_KV, Q_BLOCK, HEAD_DIM).transpose(0, 2, 4, 1, 3, 5)
    return out.reshape(n, Q_W)


def _page_unroll(n_pages):
    return 4 if n_pages % 4 == 0 else 1


def _dsa_s_score_kernel(pt_ref, qit_ref, w_ref, kin_ref, cache_ref, o_ref, buf, sem, *, layer):
    b = pl.program_id(0)
    n_pages = pt_ref.shape[1]
    unroll = _page_unroll(n_pages)

    def page_copy(j):
        return pltpu.make_async_copy(cache_ref.at[layer, pt_ref[b, j]], buf.at[j], sem)

    def start(j, _):
        page_copy(j).start()
        return 0

    def wait(j, _):
        page_copy(j).wait()
        return 0

    lax.fori_loop(0, n_pages, start, 0)
    lax.fori_loop(0, n_pages, wait, 0)
    qit = qit_ref[0]
    w = w_ref[0]
    qb = [jnp.broadcast_to(qit[:, h:h + 1], (IDX_DIM, PAGE)) for h in range(N_IDX_HEADS)]

    def page_scores(tile):
        s = jnp.zeros((1, PAGE), jnp.float32)
        for h in range(N_IDX_HEADS):
            dots = jnp.sum(tile * qb[h], axis=0, keepdims=True)
            s = s + w[:, h:h + 1] * jnp.maximum(dots, 0.0)
        return s

    def body(jo, _):
        for u in range(unroll):
            j = jo * unroll + u
            o_ref[0, pl.ds(j, 1), :] = page_scores(buf[j])
        return 0

    lax.fori_loop(0, n_pages // unroll, body, 0)
    d_new = jnp.sum(qit * kin_ref[0], axis=0, keepdims=True)
    s_new = jnp.sum(w * jnp.maximum(d_new, 0.0), axis=1, keepdims=True)
    lane = lax.broadcasted_iota(jnp.int32, (1, LANES), 1)
    o_ref[0, n_pages:n_pages + 1, :] = jnp.where(lane == 0, s_new, 0.0)


def _dsa_s_select_kernel(s_ref, o_ref, *, n_top, n_keys, idx_bits):
    s = s_ref[...]
    lane = lax.broadcasted_iota(jnp.int32, s.shape, 1)
    key = _sort_key(s, lane < n_keys)

    def thr_step(t, cur):
        cand = cur + lax.shift_left(jnp.int32(1), 31 - t)
        cnt = jnp.sum(jnp.where(key >= cand, 1.0, 0.0), axis=1, keepdims=True)
        return jnp.where(cnt >= n_top, cand, cur)

    thr = lax.fori_loop(0, 32, thr_step, jnp.full((s.shape[0], 1), INT_MIN, jnp.int32))
    need = n_top - jnp.sum(jnp.where(key > thr, 1.0, 0.0), axis=1, keepdims=True)

    def cut_step(t, cur):
        cand = cur + lax.shift_left(jnp.int32(1), idx_bits - 1 - t)
        cnt = jnp.sum(jnp.where(key == thr, jnp.where(lane < cand, 1.0, 0.0), 0.0), axis=1, keepdims=True)
        return jnp.where(cnt <= need, cand, cur)

    cut = lax.fori_loop(0, idx_bits, cut_step, jnp.zeros((s.shape[0], 1), jnp.int32))
    cut = jnp.where(thr == INT_MIN, 0, cut)
    sel = jnp.where(key > thr, 1, jnp.where(key == thr, jnp.where(lane < cut, 1, 0), 0))
    o_ref[...] = jnp.where(sel == 1, 0.0, NEG).astype(jnp.float32)


def _dsa_s_attn_kernel(pt_ref, qt_ref, bias_ref, kn_ref, vn_ref, ck_ref, cv_ref, o_ref,
                       kbuf, vbuf, sscr, sem_k, sem_v, *, layer):
    b = pl.program_id(0)
    n_pages = pt_ref.shape[1]
    unroll = _page_unroll(n_pages)

    def k_copy(j):
        return pltpu.make_async_copy(ck_ref.at[layer, pt_ref[b, j]], kbuf.at[j], sem_k)

    def v_copy(j):
        return pltpu.make_async_copy(cv_ref.at[layer, pt_ref[b, j]], vbuf.at[j], sem_v)

    def start(j, _):
        k_copy(j).start()
        v_copy(j).start()
        return 0

    def wait_k(j, _):
        k_copy(j).wait()
        return 0

    def wait_v(j, _):
        v_copy(j).wait()
        return 0

    lax.fori_loop(0, n_pages, start, 0)
    qt = qt_ref[0]
    qb = [jnp.broadcast_to(qt[:, h:h + 1], (HEAD_DIM, PAGE)) for h in range(N_HEADS)]
    rows_of = lambda h: slice((h // HEADS_PER_KV) * HEAD_DIM, (h // HEADS_PER_KV + 1) * HEAD_DIM)

    def page_s(tile, brow):
        rows = [jnp.sum(tile[rows_of(h), :] * qb[h], axis=0, keepdims=True) for h in range(N_HEADS)]
        return jnp.concatenate(rows, axis=0) + brow

    lax.fori_loop(0, n_pages, wait_k, 0)

    def pass1(jo, mx):
        for u in range(unroll):
            j = jo * unroll + u
            s = page_s(kbuf[j], bias_ref[0, pl.ds(j, 1), :])
            sscr[j] = s
            mx = jnp.maximum(mx, s)
        return mx

    mx = lax.fori_loop(0, n_pages // unroll, pass1, jnp.full((N_HEADS, PAGE), NEG, jnp.float32))
    s_new = page_s(kn_ref[0], bias_ref[0, n_pages:n_pages + 1, :])
    m = jnp.max(jnp.maximum(mx, s_new), axis=1, keepdims=True)

    def pass2(jo, lacc):
        for u in range(unroll):
            j = jo * unroll + u
            p = jnp.exp(sscr[j] - m)
            sscr[j] = p
            lacc = lacc + p
        return lacc

    p_new = jnp.exp(s_new - m)
    lacc = lax.fori_loop(0, n_pages // unroll, pass2, p_new)
    sscr[n_pages] = p_new
    lax.fori_loop(0, n_pages, wait_v, 0)

    cols = []
    for h in range(N_HEADS):
        def pass3(jo, acc):
            for u in range(unroll):
                j = jo * unroll + u
                acc = acc + vbuf[j, rows_of(h), :] * sscr[j, h:h + 1, :]
            return acc

        acc = lax.fori_loop(0, n_pages // unroll, pass3, vn_ref[0, rows_of(h), :] * p_new[h:h + 1, :])
        l = jnp.sum(lacc[h:h + 1, :], axis=1, keepdims=True)
        cols.append(jnp.sum(acc, axis=1, keepdims=True) / l)
    o_ref[0] = jnp.concatenate(cols, axis=1)


def _dsa_sample(h_att, wi, cache_kt, cache_vt, cache_kit, page_table, layer):
    nb = h_att.shape[0]
    n_pages = page_table.shape[1]
    n_keys = n_pages * PAGE + 1
    n_top = min(TOPK_MAX, n_keys // 4)
    lk = (n_pages + 1) * LANES
    q = h_att[:, :Q_W] * (HEAD_DIM ** -0.5)
    k_new = h_att[:, Q_W:Q_W + KV_W]
    v_new = h_att[:, Q_W + KV_W:Q_W + 2 * KV_W]
    qi = h_att[:, Q_W + 2 * KV_W:Q_W + 2 * KV_W + QI_W]
    ki_new = h_att[:, ATT_W - LANES:ATT_W - LANES + IDX_DIM]

    qit = jnp.pad(qi.reshape(nb, N_IDX_HEADS, IDX_DIM), ((0, 0), (0, SUBLANES - N_IDX_HEADS), (0, 0)))
    qit = qit.transpose(0, 2, 1)
    w8 = jnp.pad(wi, ((0, 0), (0, SUBLANES - N_IDX_HEADS)))[:, None, :]
    grid_spec = pltpu.PrefetchScalarGridSpec(
        num_scalar_prefetch=1, grid=(nb,),
        in_specs=[pl.BlockSpec((1, IDX_DIM, SUBLANES), lambda b, pt: (b, 0, 0)),
                  pl.BlockSpec((1, 1, SUBLANES), lambda b, pt: (b, 0, 0)),
                  pl.BlockSpec((1, IDX_DIM, 1), lambda b, pt: (b, 0, 0)),
                  pl.BlockSpec(memory_space=pl.ANY)],
        out_specs=pl.BlockSpec((1, n_pages + 1, LANES), lambda b, pt: (b, 0, 0)),
        scratch_shapes=[pltpu.VMEM((n_pages, IDX_DIM, PAGE), jnp.float32), pltpu.SemaphoreType.DMA])
    scores = pl.pallas_call(
        functools.partial(_dsa_s_score_kernel, layer=layer), grid_spec=grid_spec,
        out_shape=jax.ShapeDtypeStruct((nb, n_pages + 1, LANES), jnp.float32),
        compiler_params=_cparams(("arbitrary",)), name="dsa_sample_score")(
            page_table, qit, w8, ki_new[:, :, None], cache_kit)

    idx_bits = max(1, (lk + 1).bit_length())
    bias = pl.pallas_call(
        functools.partial(_dsa_s_select_kernel, n_top=n_top, n_keys=n_keys, idx_bits=idx_bits),
        grid=(1,), in_specs=[_const_spec((nb, lk))], out_specs=_const_spec((nb, lk)),
        out_shape=jax.ShapeDtypeStruct((nb, lk), jnp.float32),
        compiler_params=_cparams(("arbitrary",)), name="dsa_sample_select")(scores.reshape(nb, lk))

    qt = q.reshape(nb, N_HEADS, HEAD_DIM).transpose(0, 2, 1)
    first_col = ((0, 0), (0, 0), (0, PAGE - 1))
    grid_spec = pltpu.PrefetchScalarGridSpec(
        num_scalar_prefetch=1, grid=(nb,),
        in_specs=[pl.BlockSpec((1, HEAD_DIM, N_HEADS), lambda b, pt: (b, 0, 0)),
                  pl.BlockSpec((1, n_pages + 1, LANES), lambda b, pt: (b, 0, 0)),
                  pl.BlockSpec((1, KV_W, PAGE), lambda b, pt: (b, 0, 0)),
                  pl.BlockSpec((1, KV_W, PAGE), lambda b, pt: (b, 0, 0)),
                  pl.BlockSpec(memory_space=pl.ANY), pl.BlockSpec(memory_space=pl.ANY)],
        out_specs=pl.BlockSpec((1, HEAD_DIM, N_HEADS), lambda b, pt: (b, 0, 0)),
        scratch_shapes=[pltpu.VMEM((n_pages, KV_W, PAGE), jnp.float32),
                        pltpu.VMEM((n_pages, KV_W, PAGE), jnp.float32),
                        pltpu.VMEM((n_pages + 1, N_HEADS, PAGE), jnp.float32),
                        pltpu.SemaphoreType.DMA, pltpu.SemaphoreType.DMA])
    o_t = pl.pallas_call(
        functools.partial(_dsa_s_attn_kernel, layer=layer), grid_spec=grid_spec,
        out_shape=jax.ShapeDtypeStruct((nb, HEAD_DIM, N_HEADS), jnp.float32),
        compiler_params=_cparams(("arbitrary",)), name="dsa_sample_attn")(
            page_table, qt, bias.reshape(nb, n_pages + 1, LANES),
            jnp.pad(k_new[:, :, None], first_col), jnp.pad(v_new[:, :, None], first_col),
            cache_kt, cache_vt)
    return o_t.transpose(0, 2, 1).reshape(nb, Q_W)


def _post_kernel(m_ref, yd_ref, g3_ref, x_ref, wd_ref, wo_ref, lg_ref, lb_ref, o_ref, ob_ref, *, alpha):
    pd = jnp.dot(yd_ref[...], wd_ref[...], preferred_element_type=jnp.float32)
    merged = m_ref[...] + g3_ref[...].astype(jnp.float32) * pd
    mix = jnp.dot(merged.astype(jnp.bfloat16), wo_ref[...], preferred_element_type=jnp.float32)
    y = _ln(alpha * x_ref[...] + mix, lg_ref[...], lb_ref[...])
    o_ref[...] = y
    ob_ref[...] = y.astype(ob_ref.dtype)


def _post(merged_abc, yd_b, gsig, x, lp, alpha, tm):
    n, d = x.shape
    row = lambda w, col=0: pl.BlockSpec((tm, w), lambda i, col=col: (i, col))
    consts = [lp["w_br_d"], lp["w_o"], lp["ln1_g"], lp["ln1_b"]]
    return pl.pallas_call(
        functools.partial(_post_kernel, alpha=alpha), grid=(n // tm,),
        in_specs=[row(d), row(Q_W), row(d, N_BRANCHES - 1), row(d)] + [_const_spec(c.shape) for c in consts],
        out_specs=[row(d), row(d)],
        out_shape=[jax.ShapeDtypeStruct((n, d), jnp.float32), jax.ShapeDtypeStruct((n, d), jnp.bfloat16)],
        compiler_params=_cparams(("parallel",)), name="post")(merged_abc, yd_b, gsig, x, *consts)


def _ffn_kernel(xb_ref, x_ref, wg_ref, wu_ref, wd_ref, lg_ref, lb_ref, o_ref, ob_ref, acc, *, alpha):
    f = pl.program_id(1)

    @pl.when(f == 0)
    def _():
        acc[...] = jnp.zeros_like(acc)

    xb = xb_ref[...]
    hg = jnp.dot(xb, wg_ref[...], preferred_element_type=jnp.float32)
    hu = jnp.dot(xb, wu_ref[...], preferred_element_type=jnp.float32)
    h = (hg * jax.nn.sigmoid(hg) * hu).astype(jnp.bfloat16)
    acc[...] += jnp.dot(h, wd_ref[...], preferred_element_type=jnp.float32)

    @pl.when(f == pl.num_programs(1) - 1)
    def _():
        y = _ln(alpha * x_ref[...] + acc[...], lg_ref[...], lb_ref[...])
        o_ref[...] = y
        ob_ref[...] = y.astype(ob_ref.dtype)


def _ffn(xb, x, wg, wu, wd, lg, lb, alpha, tm):
    n, d = x.shape
    ff = wg.shape[1]
    tf = next(c for c in (512, 256, 128, ff) if ff % c == 0)
    row = pl.BlockSpec((tm, d), lambda i, f: (i, 0))
    return pl.pallas_call(
        functools.partial(_ffn_kernel, alpha=alpha), grid=(n // tm, ff // tf),
        in_specs=[row, row, pl.BlockSpec((d, tf), lambda i, f: (0, f)), pl.BlockSpec((d, tf), lambda i, f: (0, f)),
                  pl.BlockSpec((tf, d), lambda i, f: (f, 0)), _const_spec(lg.shape), _const_spec(lb.shape)],
        out_specs=[row, row],
        out_shape=[jax.ShapeDtypeStruct((n, d), jnp.float32), jax.ShapeDtypeStruct((n, d), jnp.bfloat16)],
        scratch_shapes=[pltpu.VMEM((tm, d), jnp.float32)],
        compiler_params=_cparams(("parallel", "arbitrary")), name="ffn")(xb, x, wg, wu, wd, lg, lb)


def _router_kernel(x_ref, r_ref, g_ref):
    lane = lax.broadcasted_iota(jnp.int32, g_ref.shape, 1)
    lanef = lane.astype(jnp.float32)
    logits = jnp.dot(x_ref[...], r_ref[...], preferred_element_type=jnp.float32, precision=lax.Precision.HIGHEST)
    logits = jnp.where(lane < N_EXPERTS, logits, -jnp.inf)
    m1 = jnp.max(logits, axis=1, keepdims=True)
    i1 = jnp.min(jnp.where(logits == m1, lanef, float(LANES)), axis=1, keepdims=True)
    rest = jnp.where(lanef == i1, -jnp.inf, logits)
    m2 = jnp.max(rest, axis=1, keepdims=True)
    i2 = jnp.min(jnp.where(rest == m2, lanef, float(LANES)), axis=1, keepdims=True)
    e2 = jnp.exp(m2 - m1)
    g_ref[...] = jnp.where(lanef == i1, 1.0 / (1.0 + e2), jnp.where(lanef == i2, e2 / (1.0 + e2), 0.0))


def _moe_kernel(cnt_ref, xb_ref, g_ref, rank_ref, rankt_ref, wg_ref, wu_ref, wd_ref, o_ref, xc, yc, *, rc):
    i, e, f = pl.program_id(0), pl.program_id(1), pl.program_id(2)
    tm, d = xb_ref.shape
    nck = (cnt_ref[i, e] + rc - 1) // rc
    half = d // 2 if d % (2 * LANES) == 0 else d

    @pl.when((e == 0) & (f == 0))
    def _():
        o_ref[...] = jnp.zeros_like(o_ref)

    @pl.when(f == 0)
    def _():
        rrow = rankt_ref[pl.ds(e, 1), :]
        riota = lax.broadcasted_iota(jnp.int32, (rc, tm), 0).astype(jnp.float32)

        def compact(k, _):
            r0 = pl.multiple_of(k * rc, rc)
            sel = jnp.where(rrow - r0.astype(jnp.float32) == riota, 1.0, 0.0).astype(jnp.bfloat16)
            xc[pl.ds(r0, rc), :] = jnp.dot(sel, xb_ref[...], preferred_element_type=jnp.float32).astype(xc.dtype)
            yc[pl.ds(r0, rc), :] = jnp.zeros((rc, d), jnp.float32)
            return 0

        lax.fori_loop(0, nck, compact, 0)

    def expert(k, _):
        r0 = pl.multiple_of(k * rc, rc)
        xk = xc[pl.ds(r0, rc), :]
        hg = jnp.dot(xk, wg_ref[0], preferred_element_type=jnp.float32)
        hu = jnp.dot(xk, wu_ref[0], preferred_element_type=jnp.float32)
        h = (hg * jax.nn.sigmoid(hg) * hu).astype(jnp.bfloat16)
        yc[pl.ds(r0, rc), :] += jnp.dot(h, wd_ref[0], preferred_element_type=jnp.float32)
        return 0

    lax.fori_loop(0, nck, expert, 0)

    @pl.when(f == pl.num_programs(2) - 1)
    def _():
        lane = lax.broadcasted_iota(jnp.int32, g_ref.shape, 1)
        rcol = jnp.sum(jnp.where(lane == e, rank_ref[...], 0.0), axis=1, keepdims=True)
        gcol = jnp.sum(jnp.where(lane == e, g_ref[...], 0.0), axis=1, keepdims=True)
        ciota = lax.broadcasted_iota(jnp.int32, (tm, rc), 1).astype(jnp.float32)

        def scatter(k, _):
            r0 = pl.multiple_of(k * rc, rc)
            selt = jnp.where(rcol - r0.astype(jnp.float32) == ciota, 1.0, 0.0).astype(jnp.bfloat16)
            for c0 in range(0, d, half):
                yk = yc[pl.ds(r0, rc), c0:c0 + half].astype(jnp.bfloat16)
                o_ref[:, c0:c0 + half] += gcol * jnp.dot(selt, yk, preferred_element_type=jnp.float32)
            return 0

        lax.fori_loop(0, nck, scatter, 0)


def _add_ln_kernel(x_ref, y_ref, lg_ref, lb_ref, o_ref, ob_ref, *, alpha):
    y = _ln(alpha * x_ref[...] + y_ref[...], lg_ref[...], lb_ref[...])
    o_ref[...] = y
    ob_ref[...] = y.astype(ob_ref.dtype)


def _moe(xb, x, router_p, wg, wu, wd, lg, lb, alpha):
    n, d = x.shape
    ne, _, ff = wg.shape
    tr = _tile(n, 1024)
    gate = pl.pallas_call(
        _router_kernel, grid=(n // tr,),
        in_specs=[pl.BlockSpec((tr, d), lambda i: (i, 0)), _const_spec(router_p.shape)],
        out_specs=pl.BlockSpec((tr, LANES), lambda i: (i, 0)),
        out_shape=jax.ShapeDtypeStruct((n, LANES), jnp.float32),
        compiler_params=_cparams(("parallel",)), name="moe_router")(x, router_p)

    tm = _tile(n, 2048)
    rc = -(-(tm * TOP_K * 9) // (ne * 8 * 2 * 16)) * 16
    n_rows = -(-tm // rc) * rc
    nt = n // tm
    routed = gate > 0.0
    rank = jnp.cumsum(routed.reshape(nt, tm, LANES).astype(jnp.int32), axis=1) - 1
    rank = jnp.where(routed.reshape(nt, tm, LANES), rank, -1).astype(jnp.float32)
    cnt = jnp.sum(routed.reshape(nt, tm, LANES)[:, :, :ne], axis=1).astype(jnp.int32)
    rank_t = rank[:, :, :SUBLANES].transpose(0, 2, 1).reshape(nt * SUBLANES, tm)
    rank = rank.reshape(n, LANES)

    tf = next(c for c in (512, 256, 128, ff) if ff % c == 0)
    row = lambda w: pl.BlockSpec((tm, w), lambda i, e, f, c: (i, 0))
    grid_spec = pltpu.PrefetchScalarGridSpec(
        num_scalar_prefetch=1, grid=(nt, ne, ff // tf),
        in_specs=[row(d), row(LANES), row(LANES),
                  pl.BlockSpec((SUBLANES, tm), lambda i, e, f, c: (i, 0)),
                  pl.BlockSpec((1, d, tf), lambda i, e, f, c: (e, 0, f)),
                  pl.BlockSpec((1, d, tf), lambda i, e, f, c: (e, 0, f)),
                  pl.BlockSpec((1, tf, d), lambda i, e, f, c: (e, f, 0))],
        out_specs=row(d),
        scratch_shapes=[pltpu.VMEM((n_rows, d), jnp.bfloat16), pltpu.VMEM((n_rows, d), jnp.float32)])
    y = pl.pallas_call(
        functools.partial(_moe_kernel, rc=rc), grid_spec=grid_spec,
        out_shape=jax.ShapeDtypeStruct((n, d), jnp.float32),
        compiler_params=_cparams(("arbitrary", "arbitrary", "arbitrary")), name="moe")(
            cnt, xb, gate, rank, rank_t, wg, wu, wd)

    ta = _tile(n, 1024)
    rowa = pl.BlockSpec((ta, d), lambda i: (i, 0))
    return pl.pallas_call(
        functools.partial(_add_ln_kernel, alpha=alpha), grid=(n // ta,),
        in_specs=[rowa, rowa, _const_spec(lg.shape), _const_spec(lb.shape)], out_specs=[rowa, rowa],
        out_shape=[jax.ShapeDtypeStruct((n, d), jnp.float32), jax.ShapeDtypeStruct((n, d), jnp.bfloat16)],
        compiler_params=_cparams(("parallel",)), name="add_ln")(x, y, lg, lb)


def _rope_tables(pos):
    rot = HEAD_DIM // ROPE_FRAC
    half = rot // 2
    freqs = jnp.power(ROPE_THETA, -jnp.arange(half, dtype=jnp.float32) / half)
    ang = pos.astype(jnp.float32)[:, None] * freqs
    cos, sin = jnp.cos(ang), jnp.sin(ang)
    t = pos.shape[0]
    ones = jnp.ones((t, HEAD_DIM - rot), jnp.float32)
    zeros = jnp.zeros((t, HEAD_DIM - rot), jnp.float32)
    zh = jnp.zeros((t, half), jnp.float32)
    c = jnp.concatenate([cos, cos, ones], axis=1)
    a = jnp.concatenate([-sin, zh, zeros], axis=1)
    b = jnp.concatenate([zh, sin, zeros], axis=1)
    rep = LANES // HEAD_DIM
    return jnp.tile(c, (1, rep)), jnp.tile(a, (1, rep)), jnp.tile(b, (1, rep))


def _layer_params(l, w_in_t, gmlp_ln_g, gmlp_ln_b, gmlp_ws, gmlp_bs, conv_b_w, conv_c_w, conv_c_bias, conf_ln_g,
                  conf_ln_b, w_br_a, w_br_b, w_br_c, w_br_d, w_o, ln1_g, ln1_b, ln2_g, ln2_b):
    bf = jnp.bfloat16
    w = w_in_t[:, l, :]
    att_end = MIX_W + Q_W + 2 * KV_W + QI_W + IDX_DIM + N_IDX_HEADS
    w_att = jnp.pad(w[MIX_W:att_end], ((0, ATT_W - (att_end - MIX_W)), (0, 0)))
    row = lambda a: a[l][None, :]
    gw = D_A // G_A
    return {
        "w_mix": w[:MIX_W], "w_att": w_att, "w_gat": w[att_end:],
        "gmlp_ln_g": row(gmlp_ln_g), "gmlp_ln_b": row(gmlp_ln_b), "gmlp_ws": gmlp_ws[l],
        "gmlp_bias_full": jnp.repeat(gmlp_bs[l].T, gw, axis=1),
        "gmlp_ws0": jnp.repeat(gmlp_ws[l][:, 0, 0], gw)[None, :],
        "gmlp_bs0": jnp.repeat(gmlp_bs[l][:, 0], gw)[None, :],
        "conv_b_w": conv_b_w[l], "conv_c_w": conv_c_w[l], "conv_c_bias": row(conv_c_bias),
        "conf_ln_g": row(conf_ln_g), "conf_ln_b": row(conf_ln_b),
        "w_br_a": w_br_a[l].astype(bf), "w_br_b": w_br_b[l].astype(bf), "w_br_c": w_br_c[l].astype(bf),
        "w_br_d": w_br_d[l].astype(bf), "w_o": w_o[l].astype(bf),
        "ln1_g": row(ln1_g), "ln1_b": row(ln1_b), "ln2_g": row(ln2_g), "ln2_b": row(ln2_b),
    }


def _channel_mixer(l, xb, x, lp, ffn_w, moe_w, alpha, tm):
    if l % 2 == 0:
        wg, wu, wd = ffn_w
        j = l // 2
        return _ffn(xb, x, wg[j], wu[j], wd[j], lp["ln2_g"], lp["ln2_b"], alpha, tm)
    router, wg, wu, wd = moe_w
    j = l // 2
    return _moe(xb, x, router[j], wg[j], wu[j], wd[j], lp["ln2_g"], lp["ln2_b"], alpha)


def kernel(x_prompt, x_sample, cache_k, cache_v, cache_idx_k, state_conv_b, state_conv_c, page_table,
           w_in, gmlp_ln_g, gmlp_ln_b, gmlp_ws, gmlp_bs, conv_b_w, conv_c_w, conv_c_bias, conf_ln_g, conf_ln_b,
           w_br_a, w_br_b, w_br_c, w_br_d, w_o, ln1_g, ln1_b, ln2_g, ln2_b,
           ffn_w_gate, ffn_w_up, ffn_w_down, moe_router, moe_w_gate, moe_w_up, moe_w_down):
    bf = jnp.bfloat16
    nb, t, d = x_prompt.shape
    ns, ts, _ = x_sample.shape
    assert ts == 1 and t % Q_BLOCK == 0
    depth = w_in.shape[0]
    alpha = float((2 * depth) ** 0.25)
    past = page_table.shape[1] * PAGE

    ffn_w = (ffn_w_gate.astype(bf), ffn_w_up.astype(bf), ffn_w_down.astype(bf))
    router_p = jnp.pad(moe_router, ((0, 0), (0, 0), (0, LANES - N_EXPERTS)))
    moe_w = (router_p, moe_w_gate.astype(bf), moe_w_up.astype(bf), moe_w_down.astype(bf))

    rope_p = _rope_tables(jnp.arange(t, dtype=jnp.int32))
    rope_s = tuple(jnp.tile(r, (ns, 1)) for r in _rope_tables(past + jnp.arange(1, dtype=jnp.int32)))
    n_pool = cache_k.shape[1]
    cache_kt = cache_k.transpose(0, 1, 3, 4, 2).reshape(depth, n_pool, KV_W, PAGE)
    cache_vt = cache_v.transpose(0, 1, 3, 4, 2).reshape(depth, n_pool, KV_W, PAGE)
    cache_kit = cache_idx_k.transpose(0, 1, 3, 2)

    n = nb * t
    tm_p = _tile(t, 512)
    tm_f = _tile(n, 1024)
    xp = x_prompt.reshape(n, d)
    xs = x_sample.reshape(ns, d)
    xp_b, xs_b = xp.astype(bf), xs.astype(bf)
    outs = {k: [] for k in ("kp", "vp", "kip", "cbp", "ccp", "ks", "vs", "kis", "cbs", "ccs", "gvs")}
    w_in_t = w_in.transpose(2, 0, 1).astype(bf)
    for l in range(depth):
        lp = _layer_params(l, w_in_t, gmlp_ln_g, gmlp_ln_b, gmlp_ws, gmlp_bs, conv_b_w, conv_c_w, conv_c_bias,
                           conf_ln_g, conf_ln_b, w_br_a, w_br_b, w_br_c, w_br_d, w_o, ln1_g, ln1_b, ln2_g, ln2_b)
        mix, gsig, h_att, q_b, kv_b, qi_b, ki_b = _in_projection(
            xp_b, lp["w_mix"], lp["w_att"], lp["w_gat"], *rope_p, t // tm_p, tm_p)
        merged_abc, cb, cc = _mix_prompt(mix, gsig, lp, nb, t, tm_p)
        wi = h_att[:, ATT_W - LANES + IDX_DIM:ATT_W - LANES + IDX_DIM + N_IDX_HEADS]
        y_d = _dsa_prompt(q_b, kv_b, qi_b, ki_b, wi, nb, t)
        x1, x1_b = _post(merged_abc, y_d, gsig, xp, lp, alpha, tm_p)
        xp, xp_b = _channel_mixer(l, x1_b, x1, lp, ffn_w, moe_w, alpha, tm_f)
        outs["kp"].append(h_att[:, Q_W:Q_W + KV_W].reshape(nb, t, N_KV, HEAD_DIM))
        outs["vp"].append(h_att[:, Q_W + KV_W:Q_W + 2 * KV_W].reshape(nb, t, N_KV, HEAD_DIM))
        outs["kip"].append(h_att[:, ATT_W - LANES:ATT_W - LANES + IDX_DIM].reshape(nb, t, IDX_DIM))
        outs["cbp"].append(cb)
        outs["ccp"].append(cc)
        mix, gsig, h_att, _, _, _, _ = _in_projection(
            xs_b, lp["w_mix"], lp["w_att"], lp["w_gat"], *rope_s, 1, ns)
        merged_abc, v_rows, zb, hc = _mix_sample(mix, gsig, state_conv_b[l], state_conv_c[l], lp)
        wi = h_att[:, ATT_W - LANES + IDX_DIM:ATT_W - LANES + IDX_DIM + N_IDX_HEADS]
        y_d = _dsa_sample(h_att, wi, cache_kt, cache_vt, cache_kit, page_table, l)
        x1, x1_b = _post(merged_abc, y_d.astype(bf), gsig, xs, lp, alpha, ns)
        xs, xs_b = _channel_mixer(l, x1_b, x1, lp, ffn_w, moe_w, alpha, ns)
        outs["ks"].append(h_att[:, Q_W:Q_W + KV_W].reshape(ns, 1, N_KV, HEAD_DIM))
        outs["vs"].append(h_att[:, Q_W + KV_W:Q_W + 2 * KV_W].reshape(ns, 1, N_KV, HEAD_DIM))
        outs["kis"].append(h_att[:, ATT_W - LANES:ATT_W - LANES + IDX_DIM].reshape(ns, 1, IDX_DIM))
        outs["cbs"].append(jnp.concatenate([state_conv_b[l][:, 1:], zb[:, None, :]], axis=1))
        outs["ccs"].append(jnp.concatenate([state_conv_c[l][:, 1:], hc[:, None, :]], axis=1))
        outs["gvs"].append(v_rows[:, None, :])
    st = lambda k: jnp.stack(outs[k])
    return (xp.reshape(nb, t, d), xs.reshape(ns, 1, d), st("kp"), st("vp"), st("kip"), st("cbp"), st("ccp"),
            st("ks"), st("vs"), st("kis"), st("cbs"), st("ccs"), st("gvs"))
```

```python
import functools

import jax
import jax.numpy as jnp
from jax import lax
from jax.experimental import pallas as pl
from jax.experimental.pallas import tpu as pltpu

PAGE = 128
CHUNK = 128
D_A, G_A = 256, 4
D_B, W_B = 256, 3
D_C, W_C = 256, 31
N_HEADS, N_KV, HEAD_DIM = 8, 2, 64
N_IDX_HEADS, IDX_DIM = 4, 64
TOPK_MAX = 256
Q_BLOCK = 128
ROPE_THETA = 500000.0
ROPE_FRAC = 4
N_EXPERTS, TOP_K = 8, 2
N_BRANCHES = 4
LN_EPS = 1e-5

LANES = 128
SUBLANES = 8
VMEM_LIMIT = 56 * 1024 * 1024

MIX_W = 2 * D_A + 3 * D_B + 2 * D_C
Q_W, KV_W, QI_W = N_HEADS * HEAD_DIM, N_KV * HEAD_DIM, N_IDX_HEADS * IDX_DIM
ATT_W = Q_W + 2 * KV_W + QI_W + LANES
INT_MIN = -2 ** 31
NEG = -1e30

_NT = (((1,), (1,)), ((), ()))


def _tile(n, pref):
    t = min(pref, n)
    while t >= 16:
        if n % t == 0 and t % 16 == 0:
            return t
        t -= 16
    return n


def _cparams(sem):
    return pltpu.CompilerParams(dimension_semantics=sem, vmem_limit_bytes=VMEM_LIMIT)


def _ln(x, g, b):
    mu = jnp.mean(x, axis=-1, keepdims=True)
    xc = x - mu
    var = jnp.mean(xc * xc, axis=-1, keepdims=True)
    return xc * lax.rsqrt(var + LN_EPS) * g + b


def _const_spec(shape):
    nd = len(shape)
    return pl.BlockSpec(shape, lambda *_: (0,) * nd)


def _proj_mix_kernel(x_ref, w_ref, o_ref, *, precision):
    o_ref[...] = lax.dot_general(x_ref[...], w_ref[...], _NT, preferred_element_type=jnp.float32,
                                 precision=precision)


def _proj_gates_kernel(x_ref, w_ref, o_ref, *, precision):
    h = lax.dot_general(x_ref[...], w_ref[...], _NT, preferred_element_type=jnp.float32, precision=precision)
    o_ref[...] = jax.nn.sigmoid(h).astype(o_ref.dtype)


def _proj_attn_kernel(x_ref, w_ref, c_ref, a_ref, b_ref, h_ref, q_ref, kv_ref, qi_ref, ki_ref, *, precision):
    h = lax.dot_general(x_ref[...], w_ref[...], _NT, preferred_element_type=jnp.float32,
                        precision=precision)
    cc, aa, bb = c_ref[...], a_ref[...], b_ref[...]
    lane = lax.broadcasted_iota(jnp.int32, cc.shape, 1)
    n_grp = ATT_W // LANES
    v_grp = (Q_W + KV_W) // LANES
    outs = []
    for gi in range(n_grp):
        xg = h[:, gi * LANES:(gi + 1) * LANES]
        if gi == v_grp:
            outs.append(xg)
            continue
        rot = xg * cc + pltpu.roll(xg, LANES - 8, 1) * aa + pltpu.roll(xg, 8, 1) * bb
        if gi == n_grp - 1:
            wi_scale = float(N_IDX_HEADS * IDX_DIM) ** -0.5
            rot = jnp.where(lane < IDX_DIM, rot, xg * wi_scale)
        outs.append(rot)
    hr = jnp.concatenate(outs, axis=1)
    h_ref[...] = hr
    q_ref[...] = (hr[:, :Q_W] * (HEAD_DIM ** -0.5)).astype(q_ref.dtype)
    kv_ref[...] = hr[:, Q_W:Q_W + 2 * KV_W].astype(kv_ref.dtype)
    qi_ref[...] = hr[:, Q_W + 2 * KV_W:Q_W + 2 * KV_W + QI_W].astype(qi_ref.dtype)
    ki_ref[...] = hr[:, ATT_W - LANES:].astype(ki_ref.dtype)


def _in_projection(xb, w_mix, w_att, w_gat, rope_c, rope_a, rope_b, n_pos_tiles, tm, precise=False):
    n, d = xb.shape
    grid = (n // tm,)
    precision = lax.Precision.HIGHEST if precise else None
    row = lambda w: pl.BlockSpec((tm, w), lambda i: (i, 0))
    mix = pl.pallas_call(
        functools.partial(_proj_mix_kernel, precision=precision), grid=grid,
        in_specs=[row(d), _const_spec(w_mix.shape)], out_specs=row(MIX_W),
        out_shape=jax.ShapeDtypeStruct((n, MIX_W), jnp.float32),
        compiler_params=_cparams(("parallel",)), name="proj_mix")(xb, w_mix)
    gw = w_gat.shape[0]
    gsig = pl.pallas_call(
        functools.partial(_proj_gates_kernel, precision=precision), grid=grid,
        in_specs=[row(d), _const_spec(w_gat.shape)], out_specs=row(gw),
        out_shape=jax.ShapeDtypeStruct((n, gw), jnp.float32 if precise else jnp.bfloat16),
        compiler_params=_cparams(("parallel",)), name="proj_gates")(xb, w_gat)
    tab = pl.BlockSpec((tm, LANES), lambda i: (i % n_pos_tiles, 0))
    h_att, q_b, kv_b, qi_b, ki_b = pl.pallas_call(
        functools.partial(_proj_attn_kernel, precision=precision), grid=grid,
        in_specs=[row(d), _const_spec(w_att.shape), tab, tab, tab],
        out_specs=[row(ATT_W), row(Q_W), row(2 * KV_W), row(QI_W), row(LANES)],
        out_shape=[jax.ShapeDtypeStruct((n, ATT_W), jnp.float32),
                   jax.ShapeDtypeStruct((n, Q_W), jnp.bfloat16),
                   jax.ShapeDtypeStruct((n, 2 * KV_W), jnp.bfloat16),
                   jax.ShapeDtypeStruct((n, QI_W), jnp.bfloat16),
                   jax.ShapeDtypeStruct((n, LANES), jnp.bfloat16)],
        compiler_params=_cparams(("parallel",)), name="proj_attn")(xb, w_att, rope_c, rope_a, rope_b)
    return mix, gsig, h_att, q_b, kv_b, qi_b, ki_b


HALO_B, HALO_C = 8, 32
CONV_ROWS = 64


def _mix_prompt_kernel(mix_ref, g0_ref, g1_ref, g2_ref, lag_ref, lab_ref, ws_ref, bias_ref,
                       wcb_ref, wcc_ref, bdw_ref, lcg_ref, lcb_ref, wa_ref, wb_ref, wc_ref,
                       out_ref, cb_ref, cc_ref, ext_b, ext_c, conv_b, conv_c):
    tm = mix_ref.shape[0]
    j = pl.program_id(1)

    @pl.when(j == 0)
    def _():
        ext_b[0:HALO_B, :] = jnp.zeros((HALO_B, D_B), jnp.float32)
        ext_c[0:HALO_C, :] = jnp.zeros((HALO_C, D_C), jnp.float32)

    mix = mix_ref[...]
    ua, va = mix[:, 0:D_A], mix[:, D_A:2 * D_A]
    o = 2 * D_A
    bx, cx, xin = mix[:, o:o + D_B], mix[:, o + D_B:o + 2 * D_B], mix[:, o + 2 * D_B:o + 3 * D_B]
    o += 3 * D_B
    ca, cgate = mix[:, o:o + D_C], mix[:, o + D_C:o + 2 * D_C]

    v = _ln(va, lag_ref[...], lab_ref[...])
    vb = v.astype(jnp.bfloat16)
    lane = lax.broadcasted_iota(jnp.int32, (CHUNK, D_A), 1)
    r_i = lax.broadcasted_iota(jnp.int32, (CHUNK, CHUNK), 0)
    c_i = lax.broadcasted_iota(jnp.int32, (CHUNK, CHUNK), 1)
    gw = D_A // G_A
    s_rows = []
    for c in range(tm // CHUNK):
        vc = vb[c * CHUNK:(c + 1) * CHUNK, :]
        sc = bias_ref[...]
        for g in range(G_A):
            wsg = jnp.where(c_i <= r_i, ws_ref[g], 0.0).astype(jnp.bfloat16)
            sg = jnp.dot(wsg, vc, preferred_element_type=jnp.float32)
            sc = sc + jnp.where((lane >= g * gw) & (lane < (g + 1) * gw), sg, 0.0)
        s_rows.append(sc)
    y_a = ua * jnp.concatenate(s_rows, axis=0)

    zb = cx * xin
    hc = ca * jax.nn.sigmoid(cgate)
    ext_b[HALO_B:HALO_B + tm, :] = zb
    ext_c[HALO_C:HALO_C + tm, :] = hc

    for r in range(tm // CONV_ROWS):
        r0 = r * CONV_ROWS
        acc = jnp.zeros((CONV_ROWS, D_B), jnp.float32)
        for d in range(W_B):
            acc = acc + wcb_ref[W_B - 1 - d:W_B - d, :] * ext_b[r0 + HALO_B - d:r0 + HALO_B - d + CONV_ROWS, :]
        conv_b[r0:r0 + CONV_ROWS, :] = acc
        acc = jnp.zeros((CONV_ROWS, D_C), jnp.float32)
        for d in range(W_C):
            acc = acc + wcc_ref[W_C - 1 - d:W_C - d, :] * ext_c[r0 + HALO_C - d:r0 + HALO_C - d + CONV_ROWS, :]
        conv_c[r0:r0 + CONV_ROWS, :] = acc
    y_b = bx * conv_b[...]
    yc = _ln(conv_c[...] + bdw_ref[...], lcg_ref[...], lcb_ref[...])
    y_c = yc * jax.nn.sigmoid(yc)

    ext_b[0:HALO_B, :] = zb[tm - HALO_B:, :]
    ext_c[0:HALO_C, :] = hc[tm - HALO_C:, :]
    cb_ref[0] = zb[tm - (W_B - 1):, :]
    cc_ref[0] = hc[tm - (W_C - 1):, :]

    pa = jnp.dot(y_a.astype(jnp.bfloat16), wa_ref[...], preferred_element_type=jnp.float32)
    pb = jnp.dot(y_b.astype(jnp.bfloat16), wb_ref[...], preferred_element_type=jnp.float32)
    pc = jnp.dot(y_c.astype(jnp.bfloat16), wc_ref[...], preferred_element_type=jnp.float32)
    out_ref[...] = (g0_ref[...].astype(jnp.float32) * pa + g1_ref[...].astype(jnp.float32) * pb
                    + g2_ref[...].astype(jnp.float32) * pc)


def _mix_prompt(mix, gsig, lp, nb, t, tm):
    n = mix.shape[0]
    d = lp["w_br_a"].shape[1]
    nt = t // tm
    row = lambda w, col=0: pl.BlockSpec((tm, w), lambda b, j, col=col: (b * nt + j, col))
    consts = [lp["gmlp_ln_g"], lp["gmlp_ln_b"], lp["gmlp_ws"], lp["gmlp_bias_full"], lp["conv_b_w"],
              lp["conv_c_w"], lp["conv_c_bias"], lp["conf_ln_g"], lp["conf_ln_b"],
              lp["w_br_a"], lp["w_br_b"], lp["w_br_c"]]
    out, cb, cc = pl.pallas_call(
        _mix_prompt_kernel, grid=(nb, nt),
        in_specs=[row(MIX_W), row(d, 0), row(d, 1), row(d, 2)] + [_const_spec(c.shape) for c in consts],
        out_specs=[row(d),
                   pl.BlockSpec((1, W_B - 1, D_B), lambda b, j: (b, 0, 0)),
                   pl.BlockSpec((1, W_C - 1, D_C), lambda b, j: (b, 0, 0))],
        out_shape=[jax.ShapeDtypeStruct((n, d), jnp.float32),
                   jax.ShapeDtypeStruct((nb, W_B - 1, D_B), jnp.float32),
                   jax.ShapeDtypeStruct((nb, W_C - 1, D_C), jnp.float32)],
        scratch_shapes=[pltpu.VMEM((HALO_B + tm, D_B), jnp.float32),
                        pltpu.VMEM((HALO_C + tm, D_C), jnp.float32),
                        pltpu.VMEM((tm, D_B), jnp.float32),
                        pltpu.VMEM((tm, D_C), jnp.float32)],
        compiler_params=_cparams(("arbitrary", "arbitrary")), name="mix_prompt")(
            mix, gsig, gsig, gsig, *consts)
    return out, cb, cc


def _mix_sample_kernel(mix_ref, g0_ref, g1_ref, g2_ref, pb_ref, pc_ref, lag_ref, lab_ref, ws0_ref, bs0_ref,
                       wcb_ref, wcc_ref, bdw_ref, lcg_ref, lcb_ref, wa_ref, wb_ref, wc_ref,
                       out_ref, v_ref, zb_ref, hc_ref):
    mix = mix_ref[...]
    ua, va = mix[:, 0:D_A], mix[:, D_A:2 * D_A]
    o = 2 * D_A
    bx, cx, xin = mix[:, o:o + D_B], mix[:, o + D_B:o + 2 * D_B], mix[:, o + 2 * D_B:o + 3 * D_B]
    o += 3 * D_B
    ca, cgate = mix[:, o:o + D_C], mix[:, o + D_C:o + 2 * D_C]

    v = _ln(va, lag_ref[...], lab_ref[...])
    v_ref[...] = v
    y_a = ua * (ws0_ref[...] * v + bs0_ref[...])

    zb = cx * xin
    zb_ref[...] = zb
    acc = wcb_ref[W_B - 1:W_B, :] * zb
    for k in range(W_B - 1):
        acc = acc + wcb_ref[k:k + 1, :] * pb_ref[k]
    y_b = bx * acc

    hc = ca * jax.nn.sigmoid(cgate)
    hc_ref[...] = hc
    acc = wcc_ref[W_C - 1:W_C, :] * hc
    for k in range(W_C - 1):
        acc = acc + wcc_ref[k:k + 1, :] * pc_ref[k]
    yc = _ln(acc + bdw_ref[...], lcg_ref[...], lcb_ref[...])
    y_c = yc * jax.nn.sigmoid(yc)

    full = dict(preferred_element_type=jnp.float32, precision=lax.Precision.HIGHEST)
    pa = jnp.dot(y_a, wa_ref[...], **full)
    pb = jnp.dot(y_b, wb_ref[...], **full)
    pc = jnp.dot(y_c, wc_ref[...], **full)
    out_ref[...] = g0_ref[...] * pa + g1_ref[...] * pb + g2_ref[...] * pc


def _mix_sample(mix, gsig, prev_b, prev_c, lp):
    n = mix.shape[0]
    d = lp["w_br_a"].shape[1]
    gcol = lambda col: pl.BlockSpec((n, d), lambda i, col=col: (0, col))
    pbt = jnp.transpose(prev_b, (1, 0, 2))
    pct = jnp.transpose(prev_c, (1, 0, 2))
    consts = [lp["gmlp_ln_g"], lp["gmlp_ln_b"], lp["gmlp_ws0"], lp["gmlp_bs0"], lp["conv_b_w"],
              lp["conv_c_w"], lp["conv_c_bias"], lp["conf_ln_g"], lp["conf_ln_b"],
              lp["w_br_a"], lp["w_br_b"], lp["w_br_c"]]
    return pl.pallas_call(
        _mix_sample_kernel, grid=(1,),
        in_specs=[_const_spec(mix.shape), gcol(0), gcol(1), gcol(2), _const_spec(pbt.shape),
                  _const_spec(pct.shape)] + [_const_spec(c.shape) for c in consts],
        out_specs=[_const_spec((n, d)), _const_spec((n, D_A)), _const_spec((n, D_B)), _const_spec((n, D_C))],
        out_shape=[jax.ShapeDtypeStruct((n, d), jnp.float32), jax.ShapeDtypeStruct((n, D_A), jnp.float32),
                   jax.ShapeDtypeStruct((n, D_B), jnp.float32), jax.ShapeDtypeStruct((n, D_C), jnp.float32)],
        compiler_params=_cparams(("arbitrary",)), name="mix_sample")(
            mix, gsig, gsig, gsig, pbt, pct, *consts)


def _sort_key(score, visible):
    score = jnp.where(score == 0.0, 0.0, score)
    bits = pltpu.bitcast(score, jnp.int32)
    key = jnp.where(bits < 0, bits ^ jnp.int32(0x7FFFFFFF), bits)
    return jnp.where(visible, key, jnp.int32(INT_MIN))


KEY_CHUNK = 512


COUNT_ROWS = 64
HEADS_PER_KV = N_HEADS // N_KV
COUNT16_ROWS = 128
I16_MIN = -2 ** 15


def _dsa_prompt_kernel(q_ref, qi_ref, wt_ref, ki_ref, k_ref, v_ref, o_ref,
                       skey, hkey, lkey, bias, satt, mscr, lscr, oscr, *, n_top):
    i = pl.program_id(1)
    sub = KEY_CHUNK // Q_BLOCK
    nch = (i + sub) // sub
    rowck = lax.broadcasted_iota(jnp.int32, (KEY_CHUNK, Q_BLOCK), 0)
    qpos = i * Q_BLOCK + lax.broadcasted_iota(jnp.int32, (1, Q_BLOCK), 1)

    def for_chunks(body, init):
        carry = lax.fori_loop(0, nch // 2, lambda c, x: body(2 * c + 1, body(2 * c, x)), init)
        return lax.cond(nch % 2 == 1, lambda x: body(nch - 1, x), lambda x: x, carry)

    def score_chunk(c, carry):
        r0 = pl.multiple_of(c * KEY_CHUNK, KEY_CHUNK)
        kc = ki_ref[0, pl.ds(r0, KEY_CHUNK), :]
        acc = jnp.zeros((KEY_CHUNK, Q_BLOCK), jnp.float32)
        for h in range(N_IDX_HEADS):
            dots = jnp.dot(kc, qi_ref[0, h], preferred_element_type=jnp.float32)
            acc = acc + wt_ref[0, h:h + 1, :] * jnp.maximum(dots, 0.0)
        key = _sort_key(acc, rowck <= qpos - r0)
        skey[pl.ds(r0, KEY_CHUNK), :] = key
        hkey[pl.ds(r0, KEY_CHUNK), :] = lax.shift_right_arithmetic(key, 16).astype(jnp.int16)
        return carry

    for_chunks(score_chunk, 0)

    def count(pred_fn, extra=None):
        def body(c, acc):
            r0 = pl.multiple_of(c * KEY_CHUNK, KEY_CHUNK)
            s = skey[pl.ds(r0, KEY_CHUNK), :]
            if extra is not None:
                extra(s, r0)
            m = pred_fn(s)
            return acc + m.reshape(KEY_CHUNK // COUNT_ROWS, COUNT_ROWS, Q_BLOCK).sum(axis=0)
        acc = lax.fori_loop(0, nch, body, jnp.zeros((COUNT_ROWS, Q_BLOCK), jnp.float32))
        return acc.sum(axis=0, keepdims=True)

    def search16(src, kk):
        def count16(cand):
            c16 = cand.astype(jnp.int16)

            def body(c, acc):
                r0 = pl.multiple_of(c * KEY_CHUNK, KEY_CHUNK)
                m = jnp.where(src[pl.ds(r0, KEY_CHUNK), :] >= c16, jnp.int16(1), jnp.int16(0))
                for r in range(0, KEY_CHUNK, COUNT16_ROWS):
                    acc = acc + m[r:r + COUNT16_ROWS, :]
                return acc
            acc = lax.fori_loop(0, nch, body, jnp.zeros((COUNT16_ROWS, Q_BLOCK), jnp.int16))
            return acc.astype(jnp.float32).sum(axis=0, keepdims=True)

        def step(t, cur):
            cand = cur + lax.shift_left(jnp.int32(1), 15 - t)
            return jnp.where(count16(cand) >= kk, cand, cur)

        return lax.fori_loop(0, 16, step, jnp.full((1, Q_BLOCK), I16_MIN, jnp.int32))

    thr_hi = search16(hkey, n_top)

    def store_low_bits(s, r0):
        low = (s & jnp.int32(0xFFFF)) + I16_MIN
        same = lax.shift_right_arithmetic(s, 16) == thr_hi
        lkey[pl.ds(r0, KEY_CHUNK), :] = jnp.where(same, low, I16_MIN).astype(jnp.int16)

    above = count(lambda s: jnp.where(lax.shift_right_arithmetic(s, 16) > thr_hi, 1.0, 0.0), extra=store_low_bits)
    thr_lo = search16(lkey, n_top - above)
    thr = lax.shift_left(thr_hi, 16) | (thr_lo - I16_MIN)
    need = jnp.where(thr == INT_MIN, 0.0, n_top - count(lambda s: jnp.where(s > thr, 1.0, 0.0)))

    row128 = lax.broadcasted_iota(jnp.int32, (Q_BLOCK, Q_BLOCK), 0)
    col128 = lax.broadcasted_iota(jnp.int32, (Q_BLOCK, Q_BLOCK), 1)
    before = jnp.where(col128 < row128, 1.0, 0.0).astype(jnp.bfloat16)

    def bias_chunk(c, seen):
        r0 = pl.multiple_of(c * KEY_CHUNK, KEY_CHUNK)
        s = skey[pl.ds(r0, KEY_CHUNK), :]
        for jj in range(sub):
            sj = s[jj * Q_BLOCK:(jj + 1) * Q_BLOCK, :]
            tie = jnp.where(sj == thr, 1.0, 0.0)
            rank = jnp.dot(before, tie.astype(jnp.bfloat16), preferred_element_type=jnp.float32) + seen
            bt = jnp.where(sj > thr, 0.0, jnp.where(sj == thr, jnp.where(rank < need, 0.0, NEG), NEG))
            bias[c * sub + jj] = bt.T
            seen = seen + jnp.sum(tie, axis=0, keepdims=True)
        return seen

    for_chunks(bias_chunk, jnp.zeros((1, Q_BLOCK), jnp.float32))

    for g in range(N_KV):
        qg = q_ref[0, g, 0]
        mscr[...] = jnp.full(mscr.shape, NEG, jnp.float32)
        lscr[...] = jnp.zeros(lscr.shape, jnp.float32)
        oscr[...] = jnp.zeros(oscr.shape, jnp.float32)

        def pass_a(c, carry):
            s = jnp.dot(qg, k_ref[0, g, c], preferred_element_type=jnp.float32)
            for hh in range(HEADS_PER_KV):
                mx = mscr[hh]
                for jj in range(sub):
                    sj = s[hh * Q_BLOCK:(hh + 1) * Q_BLOCK, jj * Q_BLOCK:(jj + 1) * Q_BLOCK] + bias[c * sub + jj]
                    satt[c * sub + jj, hh] = sj
                    mx = jnp.maximum(mx, sj)
                mscr[hh] = mx
            return carry

        for_chunks(pass_a, 0)
        ms = [jnp.max(mscr[hh], axis=1, keepdims=True) for hh in range(HEADS_PER_KV)]

        def pass_b(c, carry):
            r0 = pl.multiple_of(c * KEY_CHUNK, KEY_CHUNK)
            rows = []
            for hh in range(HEADS_PER_KV):
                ps = [jnp.exp(satt[c * sub + jj, hh] - ms[hh]) for jj in range(sub)]
                tot = ps[0]
                for p in ps[1:]:
                    tot = tot + p
                lscr[hh] += tot
                rows.append(jnp.concatenate([p.astype(jnp.bfloat16) for p in ps], axis=1))
            pc = jnp.concatenate(rows, axis=0)
            oscr[...] += jnp.dot(pc, v_ref[0, g, pl.ds(r0, KEY_CHUNK), :], preferred_element_type=jnp.float32)
            return carry

        for_chunks(pass_b, 0)
        for hh in range(HEADS_PER_KV):
            l = jnp.sum(lscr[hh], axis=1, keepdims=True)
            o_ref[0, g, 0, hh * Q_BLOCK:(hh + 1) * Q_BLOCK, :] = (
                oscr[hh * Q_BLOCK:(hh + 1) * Q_BLOCK, :] / l).astype(o_ref.dtype)


def _dsa_prompt(q_b, kv_b, qi_b, ki_b, wi, nb, t):
    n = q_b.shape[0]
    n_top = min(TOPK_MAX, t // 4)
    nqb = t // Q_BLOCK
    t_pad = ((t + KEY_CHUNK - 1) // KEY_CHUNK) * KEY_CHUNK
    rows_m = HEADS_PER_KV * Q_BLOCK
    qh = q_b.reshape(nb, nqb, Q_BLOCK, N_KV, HEADS_PER_KV, HEAD_DIM).transpose(0, 3, 1, 4, 2, 5)
    qh = qh.reshape(nb, N_KV, nqb, rows_m, HEAD_DIM)
    qih = qi_b.reshape(nb, t, N_IDX_HEADS, IDX_DIM).transpose(0, 2, 3, 1)
    kv = kv_b.reshape(nb, t, 2, N_KV, HEAD_DIM)
    kh = kv[:, :, 0].transpose(0, 2, 1, 3)
    vh = kv[:, :, 1].transpose(0, 2, 1, 3)
    if t_pad != t:
        padk = ((0, 0), (0, 0), (0, t_pad - t), (0, 0))
        kh, vh = jnp.pad(kh, padk), jnp.pad(vh, padk)
    n_chunks = t_pad // KEY_CHUNK
    kh = kh.reshape(nb, N_KV, n_chunks, KEY_CHUNK, HEAD_DIM).transpose(0, 1, 2, 4, 3)
    kib = ki_b[:, :IDX_DIM].reshape(nb, t, IDX_DIM)
    if t_pad != t:
        kib = jnp.pad(kib, ((0, 0), (0, t_pad - t), (0, 0)))
    wt = jnp.pad(wi.reshape(nb, t, N_IDX_HEADS).transpose(0, 2, 1), ((0, 0), (0, SUBLANES - N_IDX_HEADS), (0, 0)))
    kern = functools.partial(_dsa_prompt_kernel, n_top=n_top)
    once = dict(pipeline_mode=pl.Buffered(1))
    nblk = t_pad // Q_BLOCK
    out = pl.pallas_call(
        kern, grid=(nb, nqb),
        in_specs=[pl.BlockSpec((1, N_KV, 1, rows_m, HEAD_DIM), lambda b, i: (b, 0, i, 0, 0)),
                  pl.BlockSpec((1, N_IDX_HEADS, IDX_DIM, Q_BLOCK), lambda b, i: (b, 0, 0, i)),
                  pl.BlockSpec((1, SUBLANES, Q_BLOCK), lambda b, i: (b, 0, i)),
                  pl.BlockSpec((1, t_pad, IDX_DIM), lambda b, i: (b, 0, 0), **once),
                  pl.BlockSpec((1, N_KV, n_chunks, HEAD_DIM, KEY_CHUNK), lambda b, i: (b, 0, 0, 0, 0), **once),
                  pl.BlockSpec((1, N_KV, t_pad, HEAD_DIM), lambda b, i: (b, 0, 0, 0), **once)],
        out_specs=pl.BlockSpec((1, N_KV, 1, rows_m, HEAD_DIM), lambda b, i: (b, 0, i, 0, 0)),
        out_shape=jax.ShapeDtypeStruct((nb, N_KV, nqb, rows_m, HEAD_DIM), jnp.bfloat16),
        scratch_shapes=[pltpu.VMEM((t_pad, Q_BLOCK), jnp.int32),
                        pltpu.VMEM((t_pad, Q_BLOCK), jnp.int16),
                        pltpu.VMEM((t_pad, Q_BLOCK), jnp.int16),
                        pltpu.VMEM((nblk, Q_BLOCK, Q_BLOCK), jnp.float32),
                        pltpu.VMEM((nblk, HEADS_PER_KV, Q_BLOCK, Q_BLOCK), jnp.float32),
                        pltpu.VMEM((HEADS_PER_KV, Q_BLOCK, Q_BLOCK), jnp.float32),
                        pltpu.VMEM((HEADS_PER_KV, Q_BLOCK, Q_BLOCK), jnp.float32),
                        pltpu.VMEM((rows_m, HEAD_DIM), jnp.float32)],
        compiler_params=_cparams(("arbitrary", "arbitrary")), name="dsa_prompt")(qh, qih, wt, kib, kh, vh)
    out = out.reshape(nb, N_KV, nqb, HEADS_PER_KV, Q_BLOCK, HEAD_DIM).transpose(0, 2, 4, 1, 3, 5)
    return out.reshape(n, Q_W)


def _page_unroll(n_pages):
    return 4 if n_pages % 4 == 0 else 1


def _dsa_s_score_kernel(pt_ref, qit_ref, w_ref, kin_ref, cache_ref, o_ref, buf, sem, *, layer):
    b = pl.program_id(0)
    n_pages = pt_ref.shape[1]
    unroll = _page_unroll(n_pages)

    def page_copy(j):
        return pltpu.make_async_copy(cache_ref.at[layer, pt_ref[b, j]], buf.at[j], sem)

    def start(j, _):
        page_copy(j).start()
        return 0

    def wait(j, _):
        page_copy(j).wait()
        return 0

    lax.fori_loop(0, n_pages, start, 0)
    lax.fori_loop(0, n_pages, wait, 0)
    qit = qit_ref[0]
    w = w_ref[0]
    qb = [jnp.broadcast_to(qit[:, h:h + 1], (IDX_DIM, PAGE)) for h in range(N_IDX_HEADS)]

    def page_scores(tile):
        s = jnp.zeros((1, PAGE), jnp.float32)
        for h in range(N_IDX_HEADS):
            dots = jnp.sum(tile * qb[h], axis=0, keepdims=True)
            s = s + w[:, h:h + 1] * jnp.maximum(dots, 0.0)
        return s

    def body(jo, _):
        for u in range(unroll):
            j = jo * unroll + u
            o_ref[0, pl.ds(j, 1), :] = page_scores(buf[j])
        return 0

    lax.fori_loop(0, n_pages // unroll, body, 0)
    d_new = jnp.sum(qit * kin_ref[0], axis=0, keepdims=True)
    s_new = jnp.sum(w * jnp.maximum(d_new, 0.0), axis=1, keepdims=True)
    lane = lax.broadcasted_iota(jnp.int32, (1, LANES), 1)
    o_ref[0, n_pages:n_pages + 1, :] = jnp.where(lane == 0, s_new, 0.0)


def _dsa_s_select_kernel(s_ref, o_ref, *, n_top, n_keys, idx_bits):
    s = s_ref[...]
    lane = lax.broadcasted_iota(jnp.int32, s.shape, 1)
    key = _sort_key(s, lane < n_keys)

    def thr_step(t, cur):
        cand = cur + lax.shift_left(jnp.int32(1), 31 - t)
        cnt = jnp.sum(jnp.where(key >= cand, 1.0, 0.0), axis=1, keepdims=True)
        return jnp.where(cnt >= n_top, cand, cur)

    thr = lax.fori_loop(0, 32, thr_step, jnp.full((s.shape[0], 1), INT_MIN, jnp.int32))
    need = n_top - jnp.sum(jnp.where(key > thr, 1.0, 0.0), axis=1, keepdims=True)

    def cut_step(t, cur):
        cand = cur + lax.shift_left(jnp.int32(1), idx_bits - 1 - t)
        cnt = jnp.sum(jnp.where(key == thr, jnp.where(lane < cand, 1.0, 0.0), 0.0), axis=1, keepdims=True)
        return jnp.where(cnt <= need, cand, cur)

    cut = lax.fori_loop(0, idx_bits, cut_step, jnp.zeros((s.shape[0], 1), jnp.int32))
    cut = jnp.where(thr == INT_MIN, 0, cut)
    sel = jnp.where(key > thr, 1, jnp.where(key == thr, jnp.where(lane < cut, 1, 0), 0))
    o_ref[...] = jnp.where(sel == 1, 0.0, NEG).astype(jnp.float32)


def _dsa_s_attn_kernel(pt_ref, qt_ref, bias_ref, kn_ref, vn_ref, ck_ref, cv_ref, o_ref,
                       kbuf, vbuf, sscr, sem_k, sem_v, *, layer):
    b = pl.program_id(0)
    n_pages = pt_ref.shape[1]
    unroll = _page_unroll(n_pages)

    def k_copy(j):
        return pltpu.make_async_copy(ck_ref.at[layer, pt_ref[b, j]], kbuf.at[j], sem_k)

    def v_copy(j):
        return pltpu.make_async_copy(cv_ref.at[layer, pt_ref[b, j]], vbuf.at[j], sem_v)

    def start(j, _):
        k_copy(j).start()
        v_copy(j).start()
        return 0

    def wait_k(j, _):
        k_copy(j).wait()
        return 0

    def wait_v(j, _):
        v_copy(j).wait()
        return 0

    lax.fori_loop(0, n_pages, start, 0)
    qt = qt_ref[0]
    qb = [jnp.broadcast_to(qt[:, h:h + 1], (HEAD_DIM, PAGE)) for h in range(N_HEADS)]
    rows_of = lambda h: slice((h // HEADS_PER_KV) * HEAD_DIM, (h // HEADS_PER_KV + 1) * HEAD_DIM)

    def page_s(tile, brow):
        rows = [jnp.sum(tile[rows_of(h), :] * qb[h], axis=0, keepdims=True) for h in range(N_HEADS)]
        return jnp.concatenate(rows, axis=0) + brow

    lax.fori_loop(0, n_pages, wait_k, 0)

    def pass1(jo, mx):
        for u in range(unroll):
            j = jo * unroll + u
            s = page_s(kbuf[j], bias_ref[0, pl.ds(j, 1), :])
            sscr[j] = s
            mx = jnp.maximum(mx, s)
        return mx

    mx = lax.fori_loop(0, n_pages // unroll, pass1, jnp.full((N_HEADS, PAGE), NEG, jnp.float32))
    s_new = page_s(kn_ref[0], bias_ref[0, n_pages:n_pages + 1, :])
    m = jnp.max(jnp.maximum(mx, s_new), axis=1, keepdims=True)

    def pass2(jo, lacc):
        for u in range(unroll):
            j = jo * unroll + u
            p = jnp.exp(sscr[j] - m)
            sscr[j] = p
            lacc = lacc + p
        return lacc

    p_new = jnp.exp(s_new - m)
    lacc = lax.fori_loop(0, n_pages // unroll, pass2, p_new)
    sscr[n_pages] = p_new
    lax.fori_loop(0, n_pages, wait_v, 0)

    cols = []
    for h in range(N_HEADS):
        def pass3(jo, acc):
            for u in range(unroll):
                j = jo * unroll + u
                acc = acc + vbuf[j, rows_of(h), :] * sscr[j, h:h + 1, :]
            return acc

        acc = lax.fori_loop(0, n_pages // unroll, pass3, vn_ref[0, rows_of(h), :] * p_new[h:h + 1, :])
        l = jnp.sum(lacc[h:h + 1, :], axis=1, keepdims=True)
        cols.append(jnp.sum(acc, axis=1, keepdims=True) / l)
    o_ref[0] = jnp.concatenate(cols, axis=1)


def _dsa_sample(h_att, wi, cache_kt, cache_vt, cache_kit, page_table, layer):
    nb = h_att.shape[0]
    n_pages = page_table.shape[1]
    n_keys = n_pages * PAGE + 1
    n_top = min(TOPK_MAX, n_keys // 4)
    lk = (n_pages + 1) * LANES
    q = h_att[:, :Q_W] * (HEAD_DIM ** -0.5)
    k_new = h_att[:, Q_W:Q_W + KV_W]
    v_new = h_att[:, Q_W + KV_W:Q_W + 2 * KV_W]
    qi = h_att[:, Q_W + 2 * KV_W:Q_W + 2 * KV_W + QI_W]
    ki_new = h_att[:, ATT_W - LANES:ATT_W - LANES + IDX_DIM]

    qit = jnp.pad(qi.reshape(nb, N_IDX_HEADS, IDX_DIM), ((0, 0), (0, SUBLANES - N_IDX_HEADS), (0, 0)))
    qit = qit.transpose(0, 2, 1)
    w8 = jnp.pad(wi, ((0, 0), (0, SUBLANES - N_IDX_HEADS)))[:, None, :]
    grid_spec = pltpu.PrefetchScalarGridSpec(
        num_scalar_prefetch=1, grid=(nb,),
        in_specs=[pl.BlockSpec((1, IDX_DIM, SUBLANES), lambda b, pt: (b, 0, 0)),
                  pl.BlockSpec((1, 1, SUBLANES), lambda b, pt: (b, 0, 0)),
                  pl.BlockSpec((1, IDX_DIM, 1), lambda b, pt: (b, 0, 0)),
                  pl.BlockSpec(memory_space=pl.ANY)],
        out_specs=pl.BlockSpec((1, n_pages + 1, LANES), lambda b, pt: (b, 0, 0)),
        scratch_shapes=[pltpu.VMEM((n_pages, IDX_DIM, PAGE), jnp.float32), pltpu.SemaphoreType.DMA])
    scores = pl.pallas_call(
        functools.partial(_dsa_s_score_kernel, layer=layer), grid_spec=grid_spec,
        out_shape=jax.ShapeDtypeStruct((nb, n_pages + 1, LANES), jnp.float32),
        compiler_params=_cparams(("arbitrary",)), name="dsa_sample_score")(
            page_table, qit, w8, ki_new[:, :, None], cache_kit)

    idx_bits = max(1, (lk + 1).bit_length())
    bias = pl.pallas_call(
        functools.partial(_dsa_s_select_kernel, n_top=n_top, n_keys=n_keys, idx_bits=idx_bits),
        grid=(1,), in_specs=[_const_spec((nb, lk))], out_specs=_const_spec((nb, lk)),
        out_shape=jax.ShapeDtypeStruct((nb, lk), jnp.float32),
        compiler_params=_cparams(("arbitrary",)), name="dsa_sample_select")(scores.reshape(nb, lk))

    qt = q.reshape(nb, N_HEADS, HEAD_DIM).transpose(0, 2, 1)
    first_col = ((0, 0), (0, 0), (0, PAGE - 1))
    grid_spec = pltpu.PrefetchScalarGridSpec(
        num_scalar_prefetch=1, grid=(nb,),
        in_specs=[pl.BlockSpec((1, HEAD_DIM, N_HEADS), lambda b, pt: (b, 0, 0)),
                  pl.BlockSpec((1, n_pages + 1, LANES), lambda b, pt: (b, 0, 0)),
                  pl.BlockSpec((1, KV_W, PAGE), lambda b, pt: (b, 0, 0)),
                  pl.BlockSpec((1, KV_W, PAGE), lambda b, pt: (b, 0, 0)),
                  pl.BlockSpec(memory_space=pl.ANY), pl.BlockSpec(memory_space=pl.ANY)],
        out_specs=pl.BlockSpec((1, HEAD_DIM, N_HEADS), lambda b, pt: (b, 0, 0)),
        scratch_shapes=[pltpu.VMEM((n_pages, KV_W, PAGE), jnp.float32),
                        pltpu.VMEM((n_pages, KV_W, PAGE), jnp.float32),
                        pltpu.VMEM((n_pages + 1, N_HEADS, PAGE), jnp.float32),
                        pltpu.SemaphoreType.DMA, pltpu.SemaphoreType.DMA])
    o_t = pl.pallas_call(
        functools.partial(_dsa_s_attn_kernel, layer=layer), grid_spec=grid_spec,
        out_shape=jax.ShapeDtypeStruct((nb, HEAD_DIM, N_HEADS), jnp.float32),
        compiler_params=_cparams(("arbitrary",)), name="dsa_sample_attn")(
            page_table, qt, bias.reshape(nb, n_pages + 1, LANES),
            jnp.pad(k_new[:, :, None], first_col), jnp.pad(v_new[:, :, None], first_col),
            cache_kt, cache_vt)
    return o_t.transpose(0, 2, 1).reshape(nb, Q_W)


def _post_kernel(m_ref, yd_ref, g3_ref, x_ref, wd_ref, wo_ref, lg_ref, lb_ref, o_ref, ob_ref, *, alpha, precision):
    pd = jnp.dot(yd_ref[...], wd_ref[...], preferred_element_type=jnp.float32, precision=precision)
    merged = m_ref[...] + g3_ref[...].astype(jnp.float32) * pd
    mix = jnp.dot(merged.astype(wo_ref.dtype), wo_ref[...], preferred_element_type=jnp.float32,
                  precision=precision)
    y = _ln(alpha * x_ref[...] + mix, lg_ref[...], lb_ref[...])
    o_ref[...] = y
    ob_ref[...] = y.astype(ob_ref.dtype)


def _post(merged_abc, yd, gsig, x, lp, alpha, tm, precise=False):
    n, d = x.shape
    row = lambda w, col=0: pl.BlockSpec((tm, w), lambda i, col=col: (i, col))
    consts = [lp["w_br_d"], lp["w_o"], lp["ln1_g"], lp["ln1_b"]]
    precision = lax.Precision.HIGHEST if precise else None
    return pl.pallas_call(
        functools.partial(_post_kernel, alpha=alpha, precision=precision), grid=(n // tm,),
        in_specs=[row(d), row(Q_W), row(d, N_BRANCHES - 1), row(d)] + [_const_spec(c.shape) for c in consts],
        out_specs=[row(d), row(d)],
        out_shape=[jax.ShapeDtypeStruct((n, d), jnp.float32), jax.ShapeDtypeStruct((n, d), jnp.bfloat16)],
        compiler_params=_cparams(("parallel",)), name="post")(merged_abc, yd, gsig, x, *consts)


def _ffn_kernel(xb_ref, x_ref, wg_ref, wu_ref, wd_ref, lg_ref, lb_ref, o_ref, ob_ref, acc, *, alpha, precision):
    f = pl.program_id(1)

    @pl.when(f == 0)
    def _():
        acc[...] = jnp.zeros_like(acc)

    xb = xb_ref[...]
    hg = jnp.dot(xb, wg_ref[...], preferred_element_type=jnp.float32, precision=precision)
    hu = jnp.dot(xb, wu_ref[...], preferred_element_type=jnp.float32, precision=precision)
    h = (hg * jax.nn.sigmoid(hg) * hu).astype(wd_ref.dtype)
    acc[...] += jnp.dot(h, wd_ref[...], preferred_element_type=jnp.float32, precision=precision)

    @pl.when(f == pl.num_programs(1) - 1)
    def _():
        y = _ln(alpha * x_ref[...] + acc[...], lg_ref[...], lb_ref[...])
        o_ref[...] = y
        ob_ref[...] = y.astype(ob_ref.dtype)


def _ffn(xb, x, wg, wu, wd, lg, lb, alpha, tm, precise=False):
    n, d = x.shape
    ff = wg.shape[1]
    tf = next(c for c in (512, 256, 128, ff) if ff % c == 0)
    row = pl.BlockSpec((tm, d), lambda i, f: (i, 0))
    precision = lax.Precision.HIGHEST if precise else None
    return pl.pallas_call(
        functools.partial(_ffn_kernel, alpha=alpha, precision=precision), grid=(n // tm, ff // tf),
        in_specs=[row, row, pl.BlockSpec((d, tf), lambda i, f: (0, f)), pl.BlockSpec((d, tf), lambda i, f: (0, f)),
                  pl.BlockSpec((tf, d), lambda i, f: (f, 0)), _const_spec(lg.shape), _const_spec(lb.shape)],
        out_specs=[row, row],
        out_shape=[jax.ShapeDtypeStruct((n, d), jnp.float32), jax.ShapeDtypeStruct((n, d), jnp.bfloat16)],
        scratch_shapes=[pltpu.VMEM((tm, d), jnp.float32)],
        compiler_params=_cparams(("parallel", "arbitrary")), name="ffn")(xb, x, wg, wu, wd, lg, lb)


def _router_kernel(x_ref, r_ref, g_ref):
    lane = lax.broadcasted_iota(jnp.int32, g_ref.shape, 1)
    lanef = lane.astype(jnp.float32)
    logits = jnp.dot(x_ref[...], r_ref[...], preferred_element_type=jnp.float32, precision=lax.Precision.HIGHEST)
    logits = jnp.where(lane < N_EXPERTS, logits, -jnp.inf)
    m1 = jnp.max(logits, axis=1, keepdims=True)
    i1 = jnp.min(jnp.where(logits == m1, lanef, float(LANES)), axis=1, keepdims=True)
    rest = jnp.where(lanef == i1, -jnp.inf, logits)
    m2 = jnp.max(rest, axis=1, keepdims=True)
    i2 = jnp.min(jnp.where(rest == m2, lanef, float(LANES)), axis=1, keepdims=True)
    e2 = jnp.exp(m2 - m1)
    g_ref[...] = jnp.where(lanef == i1, 1.0 / (1.0 + e2), jnp.where(lanef == i2, e2 / (1.0 + e2), 0.0))


def _moe_kernel(cnt_ref, xb_ref, g_ref, rank_ref, rankt_ref, wg_ref, wu_ref, wd_ref, o_ref, xc, yc, *, rc):
    i, e, f = pl.program_id(0), pl.program_id(1), pl.program_id(2)
    tm, d = xb_ref.shape
    nck = (cnt_ref[i, e] + rc - 1) // rc
    half = d // 2 if d % (2 * LANES) == 0 else d

    @pl.when((e == 0) & (f == 0))
    def _():
        o_ref[...] = jnp.zeros_like(o_ref)

    @pl.when(f == 0)
    def _():
        rrow = rankt_ref[pl.ds(e, 1), :]
        riota = lax.broadcasted_iota(jnp.int32, (rc, tm), 0).astype(jnp.float32)

        def compact(k, _):
            r0 = pl.multiple_of(k * rc, rc)
            sel = jnp.where(rrow - r0.astype(jnp.float32) == riota, 1.0, 0.0).astype(jnp.bfloat16)
            xc[pl.ds(r0, rc), :] = jnp.dot(sel, xb_ref[...], preferred_element_type=jnp.float32).astype(xc.dtype)
            yc[pl.ds(r0, rc), :] = jnp.zeros((rc, d), jnp.float32)
            return 0

        lax.fori_loop(0, nck, compact, 0)

    def expert(k, _):
        r0 = pl.multiple_of(k * rc, rc)
        xk = xc[pl.ds(r0, rc), :]
        hg = jnp.dot(xk, wg_ref[0], preferred_element_type=jnp.float32)
        hu = jnp.dot(xk, wu_ref[0], preferred_element_type=jnp.float32)
        h = (hg * jax.nn.sigmoid(hg) * hu).astype(jnp.bfloat16)
        yc[pl.ds(r0, rc), :] += jnp.dot(h, wd_ref[0], preferred_element_type=jnp.float32)
        return 0

    lax.fori_loop(0, nck, expert, 0)

    @pl.when(f == pl.num_programs(2) - 1)
    def _():
        lane = lax.broadcasted_iota(jnp.int32, g_ref.shape, 1)
        rcol = jnp.sum(jnp.where(lane == e, rank_ref[...], 0.0), axis=1, keepdims=True)
        gcol = jnp.sum(jnp.where(lane == e, g_ref[...], 0.0), axis=1, keepdims=True)
        ciota = lax.broadcasted_iota(jnp.int32, (tm, rc), 1).astype(jnp.float32)

        def scatter(k, _):
            r0 = pl.multiple_of(k * rc, rc)
            selt = jnp.where(rcol - r0.astype(jnp.float32) == ciota, 1.0, 0.0).astype(jnp.bfloat16)
            for c0 in range(0, d, half):
                yk = yc[pl.ds(r0, rc), c0:c0 + half].astype(jnp.bfloat16)
                o_ref[:, c0:c0 + half] += gcol * jnp.dot(selt, yk, preferred_element_type=jnp.float32)
            return 0

        lax.fori_loop(0, nck, scatter, 0)


def _add_ln_kernel(x_ref, y_ref, lg_ref, lb_ref, o_ref, ob_ref, *, alpha):
    y = _ln(alpha * x_ref[...] + y_ref[...], lg_ref[...], lb_ref[...])
    o_ref[...] = y
    ob_ref[...] = y.astype(ob_ref.dtype)


def _moe(xb, x, router_p, wg, wu, wd, lg, lb, alpha):
    n, d = x.shape
    ne, _, ff = wg.shape
    tr = _tile(n, 1024)
    gate = pl.pallas_call(
        _router_kernel, grid=(n // tr,),
        in_specs=[pl.BlockSpec((tr, d), lambda i: (i, 0)), _const_spec(router_p.shape)],
        out_specs=pl.BlockSpec((tr, LANES), lambda i: (i, 0)),
        out_shape=jax.ShapeDtypeStruct((n, LANES), jnp.float32),
        compiler_params=_cparams(("parallel",)), name="moe_router")(x, router_p)

    tm = _tile(n, 2048)
    rc = -(-(tm * TOP_K * 9) // (ne * 8 * 2 * 16)) * 16
    n_rows = -(-tm // rc) * rc
    nt = n // tm
    routed = gate > 0.0
    rank = jnp.cumsum(routed.reshape(nt, tm, LANES).astype(jnp.int32), axis=1) - 1
    rank = jnp.where(routed.reshape(nt, tm, LANES), rank, -1).astype(jnp.float32)
    cnt = jnp.sum(routed.reshape(nt, tm, LANES)[:, :, :ne], axis=1).astype(jnp.int32)
    rank_t = rank[:, :, :SUBLANES].transpose(0, 2, 1).reshape(nt * SUBLANES, tm)
    rank = rank.reshape(n, LANES)

    tf = next(c for c in (512, 256, 128, ff) if ff % c == 0)
    row = lambda w: pl.BlockSpec((tm, w), lambda i, e, f, c: (i, 0))
    grid_spec = pltpu.PrefetchScalarGridSpec(
        num_scalar_prefetch=1, grid=(nt, ne, ff // tf),
        in_specs=[row(d), row(LANES), row(LANES),
                  pl.BlockSpec((SUBLANES, tm), lambda i, e, f, c: (i, 0)),
                  pl.BlockSpec((1, d, tf), lambda i, e, f, c: (e, 0, f)),
                  pl.BlockSpec((1, d, tf), lambda i, e, f, c: (e, 0, f)),
                  pl.BlockSpec((1, tf, d), lambda i, e, f, c: (e, f, 0))],
        out_specs=row(d),
        scratch_shapes=[pltpu.VMEM((n_rows, d), jnp.bfloat16), pltpu.VMEM((n_rows, d), jnp.float32)])
    y = pl.pallas_call(
        functools.partial(_moe_kernel, rc=rc), grid_spec=grid_spec,
        out_shape=jax.ShapeDtypeStruct((n, d), jnp.float32),
        compiler_params=_cparams(("arbitrary", "arbitrary", "arbitrary")), name="moe")(
            cnt, xb, gate, rank, rank_t, wg, wu, wd)

    ta = _tile(n, 1024)
    rowa = pl.BlockSpec((ta, d), lambda i: (i, 0))
    return pl.pallas_call(
        functools.partial(_add_ln_kernel, alpha=alpha), grid=(n // ta,),
        in_specs=[rowa, rowa, _const_spec(lg.shape), _const_spec(lb.shape)], out_specs=[rowa, rowa],
        out_shape=[jax.ShapeDtypeStruct((n, d), jnp.float32), jax.ShapeDtypeStruct((n, d), jnp.bfloat16)],
        compiler_params=_cparams(("parallel",)), name="add_ln")(x, y, lg, lb)


def _rope_tables(pos):
    rot = HEAD_DIM // ROPE_FRAC
    half = rot // 2
    freqs = jnp.power(ROPE_THETA, -jnp.arange(half, dtype=jnp.float32) / half)
    ang = pos.astype(jnp.float32)[:, None] * freqs
    cos, sin = jnp.cos(ang), jnp.sin(ang)
    t = pos.shape[0]
    ones = jnp.ones((t, HEAD_DIM - rot), jnp.float32)
    zeros = jnp.zeros((t, HEAD_DIM - rot), jnp.float32)
    zh = jnp.zeros((t, half), jnp.float32)
    c = jnp.concatenate([cos, cos, ones], axis=1)
    a = jnp.concatenate([-sin, zh, zeros], axis=1)
    b = jnp.concatenate([zh, sin, zeros], axis=1)
    rep = LANES // HEAD_DIM
    return jnp.tile(c, (1, rep)), jnp.tile(a, (1, rep)), jnp.tile(b, (1, rep))


def _layer_params(l, w_in_t, gmlp_ln_g, gmlp_ln_b, gmlp_ws, gmlp_bs, conv_b_w, conv_c_w, conv_c_bias, conf_ln_g,
                  conf_ln_b, w_br_a, w_br_b, w_br_c, w_br_d, w_o, ln1_g, ln1_b, ln2_g, ln2_b, dtype):
    bf = dtype
    w = w_in_t[:, l, :].astype(dtype)
    att_end = MIX_W + Q_W + 2 * KV_W + QI_W + IDX_DIM + N_IDX_HEADS
    w_att = jnp.pad(w[MIX_W:att_end], ((0, ATT_W - (att_end - MIX_W)), (0, 0)))
    row = lambda a: a[l][None, :]
    gw = D_A // G_A
    return {
        "w_mix": w[:MIX_W], "w_att": w_att, "w_gat": w[att_end:],
        "gmlp_ln_g": row(gmlp_ln_g), "gmlp_ln_b": row(gmlp_ln_b), "gmlp_ws": gmlp_ws[l],
        "gmlp_bias_full": jnp.repeat(gmlp_bs[l].T, gw, axis=1),
        "gmlp_ws0": jnp.repeat(gmlp_ws[l][:, 0, 0], gw)[None, :],
        "gmlp_bs0": jnp.repeat(gmlp_bs[l][:, 0], gw)[None, :],
        "conv_b_w": conv_b_w[l], "conv_c_w": conv_c_w[l], "conv_c_bias": row(conv_c_bias),
        "conf_ln_g": row(conf_ln_g), "conf_ln_b": row(conf_ln_b),
        "w_br_a": w_br_a[l].astype(bf), "w_br_b": w_br_b[l].astype(bf), "w_br_c": w_br_c[l].astype(bf),
        "w_br_d": w_br_d[l].astype(bf), "w_o": w_o[l].astype(bf),
        "ln1_g": row(ln1_g), "ln1_b": row(ln1_b), "ln2_g": row(ln2_g), "ln2_b": row(ln2_b),
    }


def _channel_mixer(l, xb, x, lp, ffn_w, moe_w, alpha, tm, precise=False):
    j = l // 2
    if l % 2 == 0:
        wg, wu, wd = ffn_w
        return _ffn(x if precise else xb, x, wg[j], wu[j], wd[j], lp["ln2_g"], lp["ln2_b"], alpha, tm, precise)
    router, wg, wu, wd = moe_w
    return _moe(xb, x, router[j], wg[j], wu[j], wd[j], lp["ln2_g"], lp["ln2_b"], alpha)


def kernel(x_prompt, x_sample, cache_k, cache_v, cache_idx_k, state_conv_b, state_conv_c, page_table,
           w_in, gmlp_ln_g, gmlp_ln_b, gmlp_ws, gmlp_bs, conv_b_w, conv_c_w, conv_c_bias, conf_ln_g, conf_ln_b,
           w_br_a, w_br_b, w_br_c, w_br_d, w_o, ln1_g, ln1_b, ln2_g, ln2_b,
           ffn_w_gate, ffn_w_up, ffn_w_down, moe_router, moe_w_gate, moe_w_up, moe_w_down):
    bf = jnp.bfloat16
    nb, t, d = x_prompt.shape
    ns, ts, _ = x_sample.shape
    assert ts == 1 and t % Q_BLOCK == 0
    depth = w_in.shape[0]
    alpha = float((2 * depth) ** 0.25)
    past = page_table.shape[1] * PAGE

    ffn_w = (ffn_w_gate.astype(bf), ffn_w_up.astype(bf), ffn_w_down.astype(bf))
    router_p = jnp.pad(moe_router, ((0, 0), (0, 0), (0, LANES - N_EXPERTS)))
    moe_w = (router_p, moe_w_gate.astype(bf), moe_w_up.astype(bf), moe_w_down.astype(bf))

    rope_p = _rope_tables(jnp.arange(t, dtype=jnp.int32))
    rope_s = tuple(jnp.tile(r, (ns, 1)) for r in _rope_tables(past + jnp.arange(1, dtype=jnp.int32)))
    n_pool = cache_k.shape[1]
    cache_kt = cache_k.transpose(0, 1, 3, 4, 2).reshape(depth, n_pool, KV_W, PAGE)
    cache_vt = cache_v.transpose(0, 1, 3, 4, 2).reshape(depth, n_pool, KV_W, PAGE)
    cache_kit = cache_idx_k.transpose(0, 1, 3, 2)

    n = nb * t
    tm_p = _tile(t, 512)
    tm_f = _tile(n, 1024)
    xp = x_prompt.reshape(n, d)
    xs = x_sample.reshape(ns, d)
    xp_b, xs_b = xp.astype(bf), xs.astype(bf)
    outs = {k: [] for k in ("kp", "vp", "kip", "cbp", "ccp", "ks", "vs", "kis", "cbs", "ccs", "gvs")}
    w_in_t = w_in.transpose(2, 0, 1)
    w_in_tb = w_in_t.astype(bf)
    ffn_w_f = (ffn_w_gate, ffn_w_up, ffn_w_down)
    layer_tensors = (gmlp_ln_g, gmlp_ln_b, gmlp_ws, gmlp_bs, conv_b_w, conv_c_w, conv_c_bias, conf_ln_g, conf_ln_b,
                     w_br_a, w_br_b, w_br_c, w_br_d, w_o, ln1_g, ln1_b, ln2_g, ln2_b)
    for l in range(depth):
        lp = _layer_params(l, w_in_tb, *layer_tensors, bf)
        lp_s = _layer_params(l, w_in_t, *layer_tensors, jnp.float32)
        mix, gsig, h_att, q_b, kv_b, qi_b, ki_b = _in_projection(
            xp_b, lp["w_mix"], lp["w_att"], lp["w_gat"], *rope_p, t // tm_p, tm_p)
        merged_abc, cb, cc = _mix_prompt(mix, gsig, lp, nb, t, tm_p)
        wi = h_att[:, ATT_W - LANES + IDX_DIM:ATT_W - LANES + IDX_DIM + N_IDX_HEADS]
        y_d = _dsa_prompt(q_b, kv_b, qi_b, ki_b, wi, nb, t)
        x1, x1_b = _post(merged_abc, y_d, gsig, xp, lp, alpha, tm_p)
        xp, xp_b = _channel_mixer(l, x1_b, x1, lp, ffn_w, moe_w, alpha, tm_f)
        outs["kp"].append(h_att[:, Q_W:Q_W + KV_W].reshape(nb, t, N_KV, HEAD_DIM))
        outs["vp"].append(h_att[:, Q_W + KV_W:Q_W + 2 * KV_W].reshape(nb, t, N_KV, HEAD_DIM))
        outs["kip"].append(h_att[:, ATT_W - LANES:ATT_W - LANES + IDX_DIM].reshape(nb, t, IDX_DIM))
        outs["cbp"].append(cb)
        outs["ccp"].append(cc)
        mix, gsig, h_att, _, _, _, _ = _in_projection(
            xs, lp_s["w_mix"], lp_s["w_att"], lp_s["w_gat"], *rope_s, 1, ns, precise=True)
        merged_abc, v_rows, zb, hc = _mix_sample(mix, gsig, state_conv_b[l], state_conv_c[l], lp_s)
        wi = h_att[:, ATT_W - LANES + IDX_DIM:ATT_W - LANES + IDX_DIM + N_IDX_HEADS]
        y_d = _dsa_sample(h_att, wi, cache_kt, cache_vt, cache_kit, page_table, l)
        x1, x1_b = _post(merged_abc, y_d, gsig, xs, lp_s, alpha, ns, precise=True)
        xs, _ = _channel_mixer(l, x1_b, x1, lp_s, ffn_w_f, moe_w, alpha, ns, precise=True)
        outs["ks"].append(h_att[:, Q_W:Q_W + KV_W].reshape(ns, 1, N_KV, HEAD_DIM))
        outs["vs"].append(h_att[:, Q_W + KV_W:Q_W + 2 * KV_W].reshape(ns, 1, N_KV, HEAD_DIM))
        outs["kis"].append(h_att[:, ATT_W - LANES:ATT_W - LANES + IDX_DIM].reshape(ns, 1, IDX_DIM))
        outs["cbs"].append(jnp.concatenate([state_conv_b[l][:, 1:], zb[:, None, :]], axis=1))
        outs["ccs"].append(jnp.concatenate([state_conv_c[l][:, 1:], hc[:, None, :]], axis=1))
        outs["gvs"].append(v_rows[:, None, :])
    st = lambda k: jnp.stack(outs[k])
    return (xp.reshape(nb, t, d), xs.reshape(ns, 1, d), st("kp"), st("vp"), st("kip"), st("cbp"), st("ccp"),
            st("ks"), st("vs"), st("kis"), st("cbs"), st("ccs"), st("gvs"))
```

```python
import functools

import jax
import jax.numpy as jnp
from jax import lax
from jax.experimental import pallas as pl
from jax.experimental.pallas import tpu as pltpu

PAGE = 128
CHUNK = 128
D_A, G_A = 256, 4
D_B, W_B = 256, 3
D_C, W_C = 256, 31
N_HEADS, N_KV, HEAD_DIM = 8, 2, 64
N_IDX_HEADS, IDX_DIM = 4, 64
TOPK_MAX = 256
Q_BLOCK = 128
ROPE_THETA = 500000.0
ROPE_FRAC = 4
N_EXPERTS, TOP_K = 8, 2
N_BRANCHES = 4
LN_EPS = 1e-5

LANES = 128
SUBLANES = 8
VMEM_LIMIT = 56 * 1024 * 1024

MIX_W = 2 * D_A + 3 * D_B + 2 * D_C
Q_W, KV_W, QI_W = N_HEADS * HEAD_DIM, N_KV * HEAD_DIM, N_IDX_HEADS * IDX_DIM
ATT_W = Q_W + 2 * KV_W + QI_W + LANES
INT_MIN = -2 ** 31
NEG = -1e30

_NT = (((1,), (1,)), ((), ()))


def _tile(n, pref):
    t = min(pref, n)
    while t >= 16:
        if n % t == 0 and t % 16 == 0:
            return t
        t -= 16
    return n


def _cparams(sem):
    return pltpu.CompilerParams(dimension_semantics=sem, vmem_limit_bytes=VMEM_LIMIT)


def _ln(x, g, b):
    mu = jnp.mean(x, axis=-1, keepdims=True)
    xc = x - mu
    var = jnp.mean(xc * xc, axis=-1, keepdims=True)
    return xc * lax.rsqrt(var + LN_EPS) * g + b


def _const_spec(shape):
    nd = len(shape)
    return pl.BlockSpec(shape, lambda *_: (0,) * nd)


def _proj_mix_kernel(x_ref, w_ref, o_ref, *, precision):
    o_ref[...] = lax.dot_general(x_ref[...], w_ref[...], _NT, preferred_element_type=jnp.float32,
                                 precision=precision)


def _proj_gates_kernel(x_ref, w_ref, o_ref, *, precision):
    h = lax.dot_general(x_ref[...], w_ref[...], _NT, preferred_element_type=jnp.float32, precision=precision)
    o_ref[...] = jax.nn.sigmoid(h).astype(o_ref.dtype)


def _proj_attn_kernel(x_ref, w_ref, c_ref, a_ref, b_ref, h_ref, q_ref, kv_ref, qi_ref, ki_ref, *, precision):
    h = lax.dot_general(x_ref[...], w_ref[...], _NT, preferred_element_type=jnp.float32,
                        precision=precision)
    cc, aa, bb = c_ref[...], a_ref[...], b_ref[...]
    lane = lax.broadcasted_iota(jnp.int32, cc.shape, 1)
    n_grp = ATT_W // LANES
    v_grp = (Q_W + KV_W) // LANES
    outs = []
    for gi in range(n_grp):
        xg = h[:, gi * LANES:(gi + 1) * LANES]
        if gi == v_grp:
            outs.append(xg)
            continue
        rot = xg * cc + pltpu.roll(xg, LANES - 8, 1) * aa + pltpu.roll(xg, 8, 1) * bb
        if gi == n_grp - 1:
            wi_scale = float(N_IDX_HEADS * IDX_DIM) ** -0.5
            rot = jnp.where(lane < IDX_DIM, rot, xg * wi_scale)
        outs.append(rot)
    hr = jnp.concatenate(outs, axis=1)
    h_ref[...] = hr
    q_ref[...] = (hr[:, :Q_W] * (HEAD_DIM ** -0.5)).astype(q_ref.dtype)
    kv_ref[...] = hr[:, Q_W:Q_W + 2 * KV_W].astype(kv_ref.dtype)
    qi_ref[...] = hr[:, Q_W + 2 * KV_W:Q_W + 2 * KV_W + QI_W].astype(qi_ref.dtype)
    ki_ref[...] = hr[:, ATT_W - LANES:].astype(ki_ref.dtype)


def _in_projection(xb, w_mix, w_att, w_gat, rope_c, rope_a, rope_b, n_pos_tiles, tm, precise=False):
    n, d = xb.shape
    grid = (n // tm,)
    precision = lax.Precision.HIGHEST if precise else None
    row = lambda w: pl.BlockSpec((tm, w), lambda i: (i, 0))
    mix = pl.pallas_call(
        functools.partial(_proj_mix_kernel, precision=precision), grid=grid,
        in_specs=[row(d), _const_spec(w_mix.shape)], out_specs=row(MIX_W),
        out_shape=jax.ShapeDtypeStruct((n, MIX_W), jnp.float32),
        compiler_params=_cparams(("parallel",)), name="proj_mix")(xb, w_mix)
    gw = w_gat.shape[0]
    gsig = pl.pallas_call(
        functools.partial(_proj_gates_kernel, precision=precision), grid=grid,
        in_specs=[row(d), _const_spec(w_gat.shape)], out_specs=row(gw),
        out_shape=jax.ShapeDtypeStruct((n, gw), jnp.float32 if precise else jnp.bfloat16),
        compiler_params=_cparams(("parallel",)), name="proj_gates")(xb, w_gat)
    tab = pl.BlockSpec((tm, LANES), lambda i: (i % n_pos_tiles, 0))
    h_att, q_b, kv_b, qi_b, ki_b = pl.pallas_call(
        functools.partial(_proj_attn_kernel, precision=precision), grid=grid,
        in_specs=[row(d), _const_spec(w_att.shape), tab, tab, tab],
        out_specs=[row(ATT_W), row(Q_W), row(2 * KV_W), row(QI_W), row(LANES)],
        out_shape=[jax.ShapeDtypeStruct((n, ATT_W), jnp.float32),
                   jax.ShapeDtypeStruct((n, Q_W), jnp.bfloat16),
                   jax.ShapeDtypeStruct((n, 2 * KV_W), jnp.bfloat16),
                   jax.ShapeDtypeStruct((n, QI_W), jnp.bfloat16),
                   jax.ShapeDtypeStruct((n, LANES), jnp.bfloat16)],
        compiler_params=_cparams(("parallel",)), name="proj_attn")(xb, w_att, rope_c, rope_a, rope_b)
    return mix, gsig, h_att, q_b, kv_b, qi_b, ki_b


HALO_B, HALO_C = 8, 32
CONV_ROWS = 64


def _mix_prompt_kernel(mix_ref, g0_ref, g1_ref, g2_ref, lag_ref, lab_ref, ws_ref, bias_ref,
                       wcb_ref, wcc_ref, bdw_ref, lcg_ref, lcb_ref, wa_ref, wb_ref, wc_ref,
                       out_ref, cb_ref, cc_ref, ext_b, ext_c, conv_b, conv_c):
    tm = mix_ref.shape[0]
    j = pl.program_id(1)

    @pl.when(j == 0)
    def _():
        ext_b[0:HALO_B, :] = jnp.zeros((HALO_B, D_B), jnp.float32)
        ext_c[0:HALO_C, :] = jnp.zeros((HALO_C, D_C), jnp.float32)

    mix = mix_ref[...]
    ua, va = mix[:, 0:D_A], mix[:, D_A:2 * D_A]
    o = 2 * D_A
    bx, cx, xin = mix[:, o:o + D_B], mix[:, o + D_B:o + 2 * D_B], mix[:, o + 2 * D_B:o + 3 * D_B]
    o += 3 * D_B
    ca, cgate = mix[:, o:o + D_C], mix[:, o + D_C:o + 2 * D_C]

    v = _ln(va, lag_ref[...], lab_ref[...])
    vb = v.astype(jnp.bfloat16)
    lane = lax.broadcasted_iota(jnp.int32, (CHUNK, D_A), 1)
    r_i = lax.broadcasted_iota(jnp.int32, (CHUNK, CHUNK), 0)
    c_i = lax.broadcasted_iota(jnp.int32, (CHUNK, CHUNK), 1)
    gw = D_A // G_A
    s_rows = []
    for c in range(tm // CHUNK):
        vc = vb[c * CHUNK:(c + 1) * CHUNK, :]
        sc = bias_ref[...]
        for g in range(G_A):
            wsg = jnp.where(c_i <= r_i, ws_ref[g], 0.0).astype(jnp.bfloat16)
            sg = jnp.dot(wsg, vc, preferred_element_type=jnp.float32)
            sc = sc + jnp.where((lane >= g * gw) & (lane < (g + 1) * gw), sg, 0.0)
        s_rows.append(sc)
    y_a = ua * jnp.concatenate(s_rows, axis=0)

    zb = cx * xin
    hc = ca * jax.nn.sigmoid(cgate)
    ext_b[HALO_B:HALO_B + tm, :] = zb
    ext_c[HALO_C:HALO_C + tm, :] = hc

    for r in range(tm // CONV_ROWS):
        r0 = r * CONV_ROWS
        acc = jnp.zeros((CONV_ROWS, D_B), jnp.float32)
        for d in range(W_B):
            acc = acc + wcb_ref[W_B - 1 - d:W_B - d, :] * ext_b[r0 + HALO_B - d:r0 + HALO_B - d + CONV_ROWS, :]
        conv_b[r0:r0 + CONV_ROWS, :] = acc
        acc = jnp.zeros((CONV_ROWS, D_C), jnp.float32)
        for d in range(W_C):
            acc = acc + wcc_ref[W_C - 1 - d:W_C - d, :] * ext_c[r0 + HALO_C - d:r0 + HALO_C - d + CONV_ROWS, :]
        conv_c[r0:r0 + CONV_ROWS, :] = acc
    y_b = bx * conv_b[...]
    yc = _ln(conv_c[...] + bdw_ref[...], lcg_ref[...], lcb_ref[...])
    y_c = yc * jax.nn.sigmoid(yc)

    ext_b[0:HALO_B, :] = zb[tm - HALO_B:, :]
    ext_c[0:HALO_C, :] = hc[tm - HALO_C:, :]
    cb_ref[0] = zb[tm - (W_B - 1):, :]
    cc_ref[0] = hc[tm - (W_C - 1):, :]

    pa = jnp.dot(y_a.astype(jnp.bfloat16), wa_ref[...], preferred_element_type=jnp.float32)
    pb = jnp.dot(y_b.astype(jnp.bfloat16), wb_ref[...], preferred_element_type=jnp.float32)
    pc = jnp.dot(y_c.astype(jnp.bfloat16), wc_ref[...], preferred_element_type=jnp.float32)
    out_ref[...] = (g0_ref[...].astype(jnp.float32) * pa + g1_ref[...].astype(jnp.float32) * pb
                    + g2_ref[...].astype(jnp.float32) * pc)


def _mix_prompt(mix, gsig, lp, nb, t, tm):
    n = mix.shape[0]
    d = lp["w_br_a"].shape[1]
    nt = t // tm
    row = lambda w, col=0: pl.BlockSpec((tm, w), lambda b, j, col=col: (b * nt + j, col))
    consts = [lp["gmlp_ln_g"], lp["gmlp_ln_b"], lp["gmlp_ws"], lp["gmlp_bias_full"], lp["conv_b_w"],
              lp["conv_c_w"], lp["conv_c_bias"], lp["conf_ln_g"], lp["conf_ln_b"],
              lp["w_br_a"], lp["w_br_b"], lp["w_br_c"]]
    out, cb, cc = pl.pallas_call(
        _mix_prompt_kernel, grid=(nb, nt),
        in_specs=[row(MIX_W), row(d, 0), row(d, 1), row(d, 2)] + [_const_spec(c.shape) for c in consts],
        out_specs=[row(d),
                   pl.BlockSpec((1, W_B - 1, D_B), lambda b, j: (b, 0, 0)),
                   pl.BlockSpec((1, W_C - 1, D_C), lambda b, j: (b, 0, 0))],
        out_shape=[jax.ShapeDtypeStruct((n, d), jnp.float32),
                   jax.ShapeDtypeStruct((nb, W_B - 1, D_B), jnp.float32),
                   jax.ShapeDtypeStruct((nb, W_C - 1, D_C), jnp.float32)],
        scratch_shapes=[pltpu.VMEM((HALO_B + tm, D_B), jnp.float32),
                        pltpu.VMEM((HALO_C + tm, D_C), jnp.float32),
                        pltpu.VMEM((tm, D_B), jnp.float32),
                        pltpu.VMEM((tm, D_C), jnp.float32)],
        compiler_params=_cparams(("arbitrary", "arbitrary")), name="mix_prompt")(
            mix, gsig, gsig, gsig, *consts)
    return out, cb, cc


def _mix_sample_kernel(mix_ref, g0_ref, g1_ref, g2_ref, pb_ref, pc_ref, lag_ref, lab_ref, ws0_ref, bs0_ref,
                       wcb_ref, wcc_ref, bdw_ref, lcg_ref, lcb_ref, wa_ref, wb_ref, wc_ref,
                       out_ref, v_ref, zb_ref, hc_ref):
    mix = mix_ref[...]
    ua, va = mix[:, 0:D_A], mix[:, D_A:2 * D_A]
    o = 2 * D_A
    bx, cx, xin = mix[:, o:o + D_B], mix[:, o + D_B:o + 2 * D_B], mix[:, o + 2 * D_B:o + 3 * D_B]
    o += 3 * D_B
    ca, cgate = mix[:, o:o + D_C], mix[:, o + D_C:o + 2 * D_C]

    v = _ln(va, lag_ref[...], lab_ref[...])
    v_ref[...] = v
    y_a = ua * (ws0_ref[...] * v + bs0_ref[...])

    zb = cx * xin
    zb_ref[...] = zb
    acc = wcb_ref[W_B - 1:W_B, :] * zb
    for k in range(W_B - 1):
        acc = acc + wcb_ref[k:k + 1, :] * pb_ref[k]
    y_b = bx * acc

    hc = ca * jax.nn.sigmoid(cgate)
    hc_ref[...] = hc
    acc = wcc_ref[W_C - 1:W_C, :] * hc
    for k in range(W_C - 1):
        acc = acc + wcc_ref[k:k + 1, :] * pc_ref[k]
    yc = _ln(acc + bdw_ref[...], lcg_ref[...], lcb_ref[...])
    y_c = yc * jax.nn.sigmoid(yc)

    full = dict(preferred_element_type=jnp.float32, precision=lax.Precision.HIGHEST)
    pa = jnp.dot(y_a, wa_ref[...], **full)
    pb = jnp.dot(y_b, wb_ref[...], **full)
    pc = jnp.dot(y_c, wc_ref[...], **full)
    out_ref[...] = g0_ref[...] * pa + g1_ref[...] * pb + g2_ref[...] * pc


def _mix_sample(mix, gsig, prev_b, prev_c, lp):
    n = mix.shape[0]
    d = lp["w_br_a"].shape[1]
    gcol = lambda col: pl.BlockSpec((n, d), lambda i, col=col: (0, col))
    pbt = jnp.transpose(prev_b, (1, 0, 2))
    pct = jnp.transpose(prev_c, (1, 0, 2))
    consts = [lp["gmlp_ln_g"], lp["gmlp_ln_b"], lp["gmlp_ws0"], lp["gmlp_bs0"], lp["conv_b_w"],
              lp["conv_c_w"], lp["conv_c_bias"], lp["conf_ln_g"], lp["conf_ln_b"],
              lp["w_br_a"], lp["w_br_b"], lp["w_br_c"]]
    return pl.pallas_call(
        _mix_sample_kernel, grid=(1,),
        in_specs=[_const_spec(mix.shape), gcol(0), gcol(1), gcol(2), _const_spec(pbt.shape),
                  _const_spec(pct.shape)] + [_const_spec(c.shape) for c in consts],
        out_specs=[_const_spec((n, d)), _const_spec((n, D_A)), _const_spec((n, D_B)), _const_spec((n, D_C))],
        out_shape=[jax.ShapeDtypeStruct((n, d), jnp.float32), jax.ShapeDtypeStruct((n, D_A), jnp.float32),
                   jax.ShapeDtypeStruct((n, D_B), jnp.float32), jax.ShapeDtypeStruct((n, D_C), jnp.float32)],
        compiler_params=_cparams(("arbitrary",)), name="mix_sample")(
            mix, gsig, gsig, gsig, pbt, pct, *consts)


def _masked_score(score, visible):
    return jnp.where(visible, score, -jnp.inf)


def _code_to_float(code):
    return pltpu.bitcast(jnp.where(code < 0, code ^ jnp.int32(0x7FFFFFFF), code), jnp.float32)


def _threshold_search(count_ge, n_top, shape):
    def step(t, cur):
        cand = cur + lax.shift_left(jnp.int32(1), 31 - t)
        return jnp.where(count_ge(_code_to_float(cand)) >= n_top, cand, cur)

    code = lax.fori_loop(0, 32, step, jnp.full(shape, INT_MIN, jnp.int32))
    return code, jnp.where(code == INT_MIN, -jnp.inf, _code_to_float(code))


KEY_CHUNK = 512


COUNT_ROWS = 64
HEADS_PER_KV = N_HEADS // N_KV


def _dsa_prompt_kernel(q_ref, qi_ref, wt_ref, ki_ref, k_ref, v_ref, o_ref,
                       skey, bias, satt, mscr, lscr, oscr, *, n_top):
    i = pl.program_id(1)
    sub = KEY_CHUNK // Q_BLOCK
    nch = (i + sub) // sub
    rowck = lax.broadcasted_iota(jnp.int32, (KEY_CHUNK, Q_BLOCK), 0)
    qpos = i * Q_BLOCK + lax.broadcasted_iota(jnp.int32, (1, Q_BLOCK), 1)

    def for_chunks(body, init):
        carry = lax.fori_loop(0, nch // 2, lambda c, x: body(2 * c + 1, body(2 * c, x)), init)
        return lax.cond(nch % 2 == 1, lambda x: body(nch - 1, x), lambda x: x, carry)

    def score_chunk(c, carry):
        r0 = pl.multiple_of(c * KEY_CHUNK, KEY_CHUNK)
        kc = ki_ref[0, pl.ds(r0, KEY_CHUNK), :]
        acc = jnp.zeros((KEY_CHUNK, Q_BLOCK), jnp.float32)
        for h in range(N_IDX_HEADS):
            dots = jnp.dot(kc, qi_ref[0, h], preferred_element_type=jnp.float32)
            acc = acc + wt_ref[0, h:h + 1, :] * jnp.maximum(dots, 0.0)
        skey[pl.ds(r0, KEY_CHUNK), :] = _masked_score(acc, rowck <= qpos - r0)
        return carry

    for_chunks(score_chunk, 0)

    def count(pred_fn):
        def body(c, acc):
            r0 = pl.multiple_of(c * KEY_CHUNK, KEY_CHUNK)
            m = pred_fn(skey[pl.ds(r0, KEY_CHUNK), :])
            return acc + m.reshape(KEY_CHUNK // COUNT_ROWS, COUNT_ROWS, Q_BLOCK).sum(axis=0)
        acc = lax.fori_loop(0, nch, body, jnp.zeros((COUNT_ROWS, Q_BLOCK), jnp.float32))
        return acc.sum(axis=0, keepdims=True)

    code, thr = _threshold_search(lambda v: count(lambda s: jnp.where(s >= v, 1.0, 0.0)), n_top, (1, Q_BLOCK))
    need = jnp.where(code == INT_MIN, 0.0, n_top - count(lambda s: jnp.where(s > thr, 1.0, 0.0)))

    row128 = lax.broadcasted_iota(jnp.int32, (Q_BLOCK, Q_BLOCK), 0)
    col128 = lax.broadcasted_iota(jnp.int32, (Q_BLOCK, Q_BLOCK), 1)
    before = jnp.where(col128 < row128, 1.0, 0.0).astype(jnp.bfloat16)

    def bias_chunk(c, seen):
        r0 = pl.multiple_of(c * KEY_CHUNK, KEY_CHUNK)
        s = skey[pl.ds(r0, KEY_CHUNK), :]
        for jj in range(sub):
            sj = s[jj * Q_BLOCK:(jj + 1) * Q_BLOCK, :]
            tie = jnp.where(sj == thr, 1.0, 0.0)
            rank = jnp.dot(before, tie.astype(jnp.bfloat16), preferred_element_type=jnp.float32) + seen
            bt = jnp.where(sj > thr, 0.0, jnp.where(sj == thr, jnp.where(rank < need, 0.0, NEG), NEG))
            bias[c * sub + jj] = bt.T
            seen = seen + jnp.sum(tie, axis=0, keepdims=True)
        return seen

    for_chunks(bias_chunk, jnp.zeros((1, Q_BLOCK), jnp.float32))

    for g in range(N_KV):
        qg = q_ref[0, g, 0]
        mscr[...] = jnp.full(mscr.shape, NEG, jnp.float32)
        lscr[...] = jnp.zeros(lscr.shape, jnp.float32)
        oscr[...] = jnp.zeros(oscr.shape, jnp.float32)

        def pass_a(c, carry):
            s = jnp.dot(qg, k_ref[0, g, c], preferred_element_type=jnp.float32)
            for hh in range(HEADS_PER_KV):
                mx = mscr[hh]
                for jj in range(sub):
                    sj = s[hh * Q_BLOCK:(hh + 1) * Q_BLOCK, jj * Q_BLOCK:(jj + 1) * Q_BLOCK] + bias[c * sub + jj]
                    satt[c * sub + jj, hh] = sj
                    mx = jnp.maximum(mx, sj)
                mscr[hh] = mx
            return carry

        for_chunks(pass_a, 0)
        ms = [jnp.max(mscr[hh], axis=1, keepdims=True) for hh in range(HEADS_PER_KV)]

        def pass_b(c, carry):
            r0 = pl.multiple_of(c * KEY_CHUNK, KEY_CHUNK)
            rows = []
            for hh in range(HEADS_PER_KV):
                ps = [jnp.exp(satt[c * sub + jj, hh] - ms[hh]) for jj in range(sub)]
                tot = ps[0]
                for p in ps[1:]:
                    tot = tot + p
                lscr[hh] += tot
                rows.append(jnp.concatenate([p.astype(jnp.bfloat16) for p in ps], axis=1))
            pc = jnp.concatenate(rows, axis=0)
            oscr[...] += jnp.dot(pc, v_ref[0, g, pl.ds(r0, KEY_CHUNK), :], preferred_element_type=jnp.float32)
            return carry

        for_chunks(pass_b, 0)
        for hh in range(HEADS_PER_KV):
            l = jnp.sum(lscr[hh], axis=1, keepdims=True)
            o_ref[0, g, 0, hh * Q_BLOCK:(hh + 1) * Q_BLOCK, :] = (
                oscr[hh * Q_BLOCK:(hh + 1) * Q_BLOCK, :] / l).astype(o_ref.dtype)


def _dsa_prompt(q_b, kv_b, qi_b, ki_b, wi, nb, t):
    n = q_b.shape[0]
    n_top = min(TOPK_MAX, t // 4)
    nqb = t // Q_BLOCK
    t_pad = ((t + KEY_CHUNK - 1) // KEY_CHUNK) * KEY_CHUNK
    rows_m = HEADS_PER_KV * Q_BLOCK
    qh = q_b.reshape(nb, nqb, Q_BLOCK, N_KV, HEADS_PER_KV, HEAD_DIM).transpose(0, 3, 1, 4, 2, 5)
    qh = qh.reshape(nb, N_KV, nqb, rows_m, HEAD_DIM)
    qih = qi_b.reshape(nb, t, N_IDX_HEADS, IDX_DIM).transpose(0, 2, 3, 1)
    kv = kv_b.reshape(nb, t, 2, N_KV, HEAD_DIM)
    kh = kv[:, :, 0].transpose(0, 2, 1, 3)
    vh = kv[:, :, 1].transpose(0, 2, 1, 3)
    if t_pad != t:
        padk = ((0, 0), (0, 0), (0, t_pad - t), (0, 0))
        kh, vh = jnp.pad(kh, padk), jnp.pad(vh, padk)
    n_chunks = t_pad // KEY_CHUNK
    kh = kh.reshape(nb, N_KV, n_chunks, KEY_CHUNK, HEAD_DIM).transpose(0, 1, 2, 4, 3)
    kib = ki_b[:, :IDX_DIM].reshape(nb, t, IDX_DIM)
    if t_pad != t:
        kib = jnp.pad(kib, ((0, 0), (0, t_pad - t), (0, 0)))
    wt = jnp.pad(wi.reshape(nb, t, N_IDX_HEADS).transpose(0, 2, 1), ((0, 0), (0, SUBLANES - N_IDX_HEADS), (0, 0)))
    kern = functools.partial(_dsa_prompt_kernel, n_top=n_top)
    once = dict(pipeline_mode=pl.Buffered(1))
    nblk = t_pad // Q_BLOCK
    out = pl.pallas_call(
        kern, grid=(nb, nqb),
        in_specs=[pl.BlockSpec((1, N_KV, 1, rows_m, HEAD_DIM), lambda b, i: (b, 0, i, 0, 0)),
                  pl.BlockSpec((1, N_IDX_HEADS, IDX_DIM, Q_BLOCK), lambda b, i: (b, 0, 0, i)),
                  pl.BlockSpec((1, SUBLANES, Q_BLOCK), lambda b, i: (b, 0, i)),
                  pl.BlockSpec((1, t_pad, IDX_DIM), lambda b, i: (b, 0, 0), **once),
                  pl.BlockSpec((1, N_KV, n_chunks, HEAD_DIM, KEY_CHUNK), lambda b, i: (b, 0, 0, 0, 0), **once),
                  pl.BlockSpec((1, N_KV, t_pad, HEAD_DIM), lambda b, i: (b, 0, 0, 0), **once)],
        out_specs=pl.BlockSpec((1, N_KV, 1, rows_m, HEAD_DIM), lambda b, i: (b, 0, i, 0, 0)),
        out_shape=jax.ShapeDtypeStruct((nb, N_KV, nqb, rows_m, HEAD_DIM), jnp.bfloat16),
        scratch_shapes=[pltpu.VMEM((t_pad, Q_BLOCK), jnp.float32),
                        pltpu.VMEM((nblk, Q_BLOCK, Q_BLOCK), jnp.float32),
                        pltpu.VMEM((nblk, HEADS_PER_KV, Q_BLOCK, Q_BLOCK), jnp.float32),
                        pltpu.VMEM((HEADS_PER_KV, Q_BLOCK, Q_BLOCK), jnp.float32),
                        pltpu.VMEM((HEADS_PER_KV, Q_BLOCK, Q_BLOCK), jnp.float32),
                        pltpu.VMEM((rows_m, HEAD_DIM), jnp.float32)],
        compiler_params=_cparams(("arbitrary", "arbitrary")), name="dsa_prompt")(qh, qih, wt, kib, kh, vh)
    out = out.reshape(nb, N_KV, nqb, HEADS_PER_KV, Q_BLOCK, HEAD_DIM).transpose(0, 2, 4, 1, 3, 5)
    return out.reshape(n, Q_W)


def _page_unroll(n_pages):
    return 4 if n_pages % 4 == 0 else 1


def _dsa_s_score_kernel(pt_ref, qit_ref, w_ref, kin_ref, cache_ref, o_ref, buf, sem, *, layer):
    b = pl.program_id(0)
    n_pages = pt_ref.shape[1]
    unroll = _page_unroll(n_pages)

    def page_copy(j):
        return pltpu.make_async_copy(cache_ref.at[layer, pt_ref[b, j]], buf.at[j], sem)

    def start(j, _):
        page_copy(j).start()
        return 0

    def wait(j, _):
        page_copy(j).wait()
        return 0

    lax.fori_loop(0, n_pages, start, 0)
    lax.fori_loop(0, n_pages, wait, 0)
    qit = qit_ref[0]
    w = w_ref[0]
    qb = [jnp.broadcast_to(qit[:, h:h + 1], (IDX_DIM, PAGE)) for h in range(N_IDX_HEADS)]

    def page_scores(tile):
        s = jnp.zeros((1, PAGE), jnp.float32)
        for h in range(N_IDX_HEADS):
            dots = jnp.sum(tile * qb[h], axis=0, keepdims=True)
            s = s + w[:, h:h + 1] * jnp.maximum(dots, 0.0)
        return s

    def body(jo, _):
        for u in range(unroll):
            j = jo * unroll + u
            o_ref[0, pl.ds(j, 1), :] = page_scores(buf[j])
        return 0

    lax.fori_loop(0, n_pages // unroll, body, 0)
    d_new = jnp.sum(qit * kin_ref[0], axis=0, keepdims=True)
    s_new = jnp.sum(w * jnp.maximum(d_new, 0.0), axis=1, keepdims=True)
    lane = lax.broadcasted_iota(jnp.int32, (1, LANES), 1)
    o_ref[0, n_pages:n_pages + 1, :] = jnp.where(lane == 0, s_new, 0.0)


def _dsa_s_select_kernel(s_ref, o_ref, *, n_top, n_keys, idx_bits):
    lane = lax.broadcasted_iota(jnp.int32, s_ref.shape, 1)
    s = _masked_score(s_ref[...], lane < n_keys)
    rows = (s.shape[0], 1)
    count_ge = lambda v: jnp.sum(jnp.where(s >= v, 1.0, 0.0), axis=1, keepdims=True)
    code, thr = _threshold_search(count_ge, n_top, rows)
    need = n_top - jnp.sum(jnp.where(s > thr, 1.0, 0.0), axis=1, keepdims=True)

    def cut_step(t, cur):
        cand = cur + lax.shift_left(jnp.int32(1), idx_bits - 1 - t)
        cnt = jnp.sum(jnp.where(s == thr, jnp.where(lane < cand, 1.0, 0.0), 0.0), axis=1, keepdims=True)
        return jnp.where(cnt <= need, cand, cur)

    cut = lax.fori_loop(0, idx_bits, cut_step, jnp.zeros(rows, jnp.int32))
    cut = jnp.where(code == INT_MIN, 0, cut)
    sel = jnp.where(s > thr, 1, jnp.where(s == thr, jnp.where(lane < cut, 1, 0), 0))
    o_ref[...] = jnp.where(sel == 1, 0.0, NEG).astype(jnp.float32)


def _dsa_s_attn_kernel(pt_ref, qt_ref, bias_ref, kn_ref, vn_ref, ck_ref, cv_ref, o_ref,
                       kbuf, vbuf, sscr, sem_k, sem_v, *, layer):
    b = pl.program_id(0)
    n_pages = pt_ref.shape[1]
    unroll = _page_unroll(n_pages)

    def k_copy(j):
        return pltpu.make_async_copy(ck_ref.at[layer, pt_ref[b, j]], kbuf.at[j], sem_k)

    def v_copy(j):
        return pltpu.make_async_copy(cv_ref.at[layer, pt_ref[b, j]], vbuf.at[j], sem_v)

    def start(j, _):
        k_copy(j).start()
        v_copy(j).start()
        return 0

    def wait_k(j, _):
        k_copy(j).wait()
        return 0

    def wait_v(j, _):
        v_copy(j).wait()
        return 0

    lax.fori_loop(0, n_pages, start, 0)
    qt = qt_ref[0]
    qb = [jnp.broadcast_to(qt[:, h:h + 1], (HEAD_DIM, PAGE)) for h in range(N_HEADS)]
    rows_of = lambda h: slice((h // HEADS_PER_KV) * HEAD_DIM, (h // HEADS_PER_KV + 1) * HEAD_DIM)

    def page_s(tile, brow):
        rows = [jnp.sum(tile[rows_of(h), :] * qb[h], axis=0, keepdims=True) for h in range(N_HEADS)]
        return jnp.concatenate(rows, axis=0) + brow

    lax.fori_loop(0, n_pages, wait_k, 0)

    def pass1(jo, mx):
        for u in range(unroll):
            j = jo * unroll + u
            s = page_s(kbuf[j], bias_ref[0, pl.ds(j, 1), :])
            sscr[j] = s
            mx = jnp.maximum(mx, s)
        return mx

    mx = lax.fori_loop(0, n_pages // unroll, pass1, jnp.full((N_HEADS, PAGE), NEG, jnp.float32))
    s_new = page_s(kn_ref[0], bias_ref[0, n_pages:n_pages + 1, :])
    m = jnp.max(jnp.maximum(mx, s_new), axis=1, keepdims=True)

    def pass2(jo, lacc):
        for u in range(unroll):
            j = jo * unroll + u
            p = jnp.exp(sscr[j] - m)
            sscr[j] = p
            lacc = lacc + p
        return lacc

    p_new = jnp.exp(s_new - m)
    lacc = lax.fori_loop(0, n_pages // unroll, pass2, p_new)
    sscr[n_pages] = p_new
    lax.fori_loop(0, n_pages, wait_v, 0)

    cols = []
    for h in range(N_HEADS):
        def pass3(jo, acc):
            for u in range(unroll):
                j = jo * unroll + u
                acc = acc + vbuf[j, rows_of(h), :] * sscr[j, h:h + 1, :]
            return acc

        acc = lax.fori_loop(0, n_pages // unroll, pass3, vn_ref[0, rows_of(h), :] * p_new[h:h + 1, :])
        l = jnp.sum(lacc[h:h + 1, :], axis=1, keepdims=True)
        cols.append(jnp.sum(acc, axis=1, keepdims=True) / l)
    o_ref[0] = jnp.concatenate(cols, axis=1)


def _dsa_sample(h_att, wi, cache_kt, cache_vt, cache_kit, page_table, layer):
    nb = h_att.shape[0]
    n_pages = page_table.shape[1]
    n_keys = n_pages * PAGE + 1
    n_top = min(TOPK_MAX, n_keys // 4)
    lk = (n_pages + 1) * LANES
    q = h_att[:, :Q_W] * (HEAD_DIM ** -0.5)
    k_new = h_att[:, Q_W:Q_W + KV_W]
    v_new = h_att[:, Q_W + KV_W:Q_W + 2 * KV_W]
    qi = h_att[:, Q_W + 2 * KV_W:Q_W + 2 * KV_W + QI_W]
    ki_new = h_att[:, ATT_W - LANES:ATT_W - LANES + IDX_DIM]

    qit = jnp.pad(qi.reshape(nb, N_IDX_HEADS, IDX_DIM), ((0, 0), (0, SUBLANES - N_IDX_HEADS), (0, 0)))
    qit = qit.transpose(0, 2, 1)
    w8 = jnp.pad(wi, ((0, 0), (0, SUBLANES - N_IDX_HEADS)))[:, None, :]
    grid_spec = pltpu.PrefetchScalarGridSpec(
        num_scalar_prefetch=1, grid=(nb,),
        in_specs=[pl.BlockSpec((1, IDX_DIM, SUBLANES), lambda b, pt: (b, 0, 0)),
                  pl.BlockSpec((1, 1, SUBLANES), lambda b, pt: (b, 0, 0)),
                  pl.BlockSpec((1, IDX_DIM, 1), lambda b, pt: (b, 0, 0)),
                  pl.BlockSpec(memory_space=pl.ANY)],
        out_specs=pl.BlockSpec((1, n_pages + 1, LANES), lambda b, pt: (b, 0, 0)),
        scratch_shapes=[pltpu.VMEM((n_pages, IDX_DIM, PAGE), jnp.float32), pltpu.SemaphoreType.DMA])
    scores = pl.pallas_call(
        functools.partial(_dsa_s_score_kernel, layer=layer), grid_spec=grid_spec,
        out_shape=jax.ShapeDtypeStruct((nb, n_pages + 1, LANES), jnp.float32),
        compiler_params=_cparams(("arbitrary",)), name="dsa_sample_score")(
            page_table, qit, w8, ki_new[:, :, None], cache_kit)

    idx_bits = max(1, (lk + 1).bit_length())
    bias = pl.pallas_call(
        functools.partial(_dsa_s_select_kernel, n_top=n_top, n_keys=n_keys, idx_bits=idx_bits),
        grid=(1,), in_specs=[_const_spec((nb, lk))], out_specs=_const_spec((nb, lk)),
        out_shape=jax.ShapeDtypeStruct((nb, lk), jnp.float32),
        compiler_params=_cparams(("arbitrary",)), name="dsa_sample_select")(scores.reshape(nb, lk))

    qt = q.reshape(nb, N_HEADS, HEAD_DIM).transpose(0, 2, 1)
    first_col = ((0, 0), (0, 0), (0, PAGE - 1))
    grid_spec = pltpu.PrefetchScalarGridSpec(
        num_scalar_prefetch=1, grid=(nb,),
        in_specs=[pl.BlockSpec((1, HEAD_DIM, N_HEADS), lambda b, pt: (b, 0, 0)),
                  pl.BlockSpec((1, n_pages + 1, LANES), lambda b, pt: (b, 0, 0)),
                  pl.BlockSpec((1, KV_W, PAGE), lambda b, pt: (b, 0, 0)),
                  pl.BlockSpec((1, KV_W, PAGE), lambda b, pt: (b, 0, 0)),
                  pl.BlockSpec(memory_space=pl.ANY), pl.BlockSpec(memory_space=pl.ANY)],
        out_specs=pl.BlockSpec((1, HEAD_DIM, N_HEADS), lambda b, pt: (b, 0, 0)),
        scratch_shapes=[pltpu.VMEM((n_pages, KV_W, PAGE), jnp.float32),
                        pltpu.VMEM((n_pages, KV_W, PAGE), jnp.float32),
                        pltpu.VMEM((n_pages + 1, N_HEADS, PAGE), jnp.float32),
                        pltpu.SemaphoreType.DMA, pltpu.SemaphoreType.DMA])
    o_t = pl.pallas_call(
        functools.partial(_dsa_s_attn_kernel, layer=layer), grid_spec=grid_spec,
        out_shape=jax.ShapeDtypeStruct((nb, HEAD_DIM, N_HEADS), jnp.float32),
        compiler_params=_cparams(("arbitrary",)), name="dsa_sample_attn")(
            page_table, qt, bias.reshape(nb, n_pages + 1, LANES),
            jnp.pad(k_new[:, :, None], first_col), jnp.pad(v_new[:, :, None], first_col),
            cache_kt, cache_vt)
    return o_t.transpose(0, 2, 1).reshape(nb, Q_W)


def _post_kernel(m_ref, yd_ref, g3_ref, x_ref, wd_ref, wo_ref, lg_ref, lb_ref, o_ref, ob_ref, *, alpha, precision):
    pd = jnp.dot(yd_ref[...], wd_ref[...], preferred_element_type=jnp.float32, precision=precision)
    merged = m_ref[...] + g3_ref[...].astype(jnp.float32) * pd
    mix = jnp.dot(merged.astype(wo_ref.dtype), wo_ref[...], preferred_element_type=jnp.float32,
                  precision=precision)
    y = _ln(alpha * x_ref[...] + mix, lg_ref[...], lb_ref[...])
    o_ref[...] = y
    ob_ref[...] = y.astype(ob_ref.dtype)


def _post(merged_abc, yd, gsig, x, lp, alpha, tm, precise=False):
    n, d = x.shape
    row = lambda w, col=0: pl.BlockSpec((tm, w), lambda i, col=col: (i, col))
    consts = [lp["w_br_d"], lp["w_o"], lp["ln1_g"], lp["ln1_b"]]
    precision = lax.Precision.HIGHEST if precise else None
    return pl.pallas_call(
        functools.partial(_post_kernel, alpha=alpha, precision=precision), grid=(n // tm,),
        in_specs=[row(d), row(Q_W), row(d, N_BRANCHES - 1), row(d)] + [_const_spec(c.shape) for c in consts],
        out_specs=[row(d), row(d)],
        out_shape=[jax.ShapeDtypeStruct((n, d), jnp.float32), jax.ShapeDtypeStruct((n, d), jnp.bfloat16)],
        compiler_params=_cparams(("parallel",)), name="post")(merged_abc, yd, gsig, x, *consts)


def _ffn_kernel(xb_ref, x_ref, wg_ref, wu_ref, wd_ref, lg_ref, lb_ref, o_ref, ob_ref, acc, *, alpha, precision):
    f = pl.program_id(1)

    @pl.when(f == 0)
    def _():
        acc[...] = jnp.zeros_like(acc)

    xb = xb_ref[...]
    hg = jnp.dot(xb, wg_ref[...], preferred_element_type=jnp.float32, precision=precision)
    hu = jnp.dot(xb, wu_ref[...], preferred_element_type=jnp.float32, precision=precision)
    h = (hg * jax.nn.sigmoid(hg) * hu).astype(wd_ref.dtype)
    acc[...] += jnp.dot(h, wd_ref[...], preferred_element_type=jnp.float32, precision=precision)

    @pl.when(f == pl.num_programs(1) - 1)
    def _():
        y = _ln(alpha * x_ref[...] + acc[...], lg_ref[...], lb_ref[...])
        o_ref[...] = y
        ob_ref[...] = y.astype(ob_ref.dtype)


def _ffn(xb, x, wg, wu, wd, lg, lb, alpha, tm, precise=False):
    n, d = x.shape
    ff = wg.shape[1]
    tf = next(c for c in (512, 256, 128, ff) if ff % c == 0)
    row = pl.BlockSpec((tm, d), lambda i, f: (i, 0))
    precision = lax.Precision.HIGHEST if precise else None
    return pl.pallas_call(
        functools.partial(_ffn_kernel, alpha=alpha, precision=precision), grid=(n // tm, ff // tf),
        in_specs=[row, row, pl.BlockSpec((d, tf), lambda i, f: (0, f)), pl.BlockSpec((d, tf), lambda i, f: (0, f)),
                  pl.BlockSpec((tf, d), lambda i, f: (f, 0)), _const_spec(lg.shape), _const_spec(lb.shape)],
        out_specs=[row, row],
        out_shape=[jax.ShapeDtypeStruct((n, d), jnp.float32), jax.ShapeDtypeStruct((n, d), jnp.bfloat16)],
        scratch_shapes=[pltpu.VMEM((tm, d), jnp.float32)],
        compiler_params=_cparams(("parallel", "arbitrary")), name="ffn")(xb, x, wg, wu, wd, lg, lb)


def _router_kernel(x_ref, r_ref, g_ref):
    lane = lax.broadcasted_iota(jnp.int32, g_ref.shape, 1)
    lanef = lane.astype(jnp.float32)
    logits = jnp.dot(x_ref[...], r_ref[...], preferred_element_type=jnp.float32, precision=lax.Precision.HIGHEST)
    logits = jnp.where(lane < N_EXPERTS, logits, -jnp.inf)
    m1 = jnp.max(logits, axis=1, keepdims=True)
    i1 = jnp.min(jnp.where(logits == m1, lanef, float(LANES)), axis=1, keepdims=True)
    rest = jnp.where(lanef == i1, -jnp.inf, logits)
    m2 = jnp.max(rest, axis=1, keepdims=True)
    i2 = jnp.min(jnp.where(rest == m2, lanef, float(LANES)), axis=1, keepdims=True)
    e2 = jnp.exp(m2 - m1)
    g_ref[...] = jnp.where(lanef == i1, 1.0 / (1.0 + e2), jnp.where(lanef == i2, e2 / (1.0 + e2), 0.0))


def _moe_kernel(cnt_ref, xb_ref, g_ref, rank_ref, rankt_ref, wg_ref, wu_ref, wd_ref, o_ref, xc, yc, *, rc):
    i, e, f = pl.program_id(0), pl.program_id(1), pl.program_id(2)
    tm, d = xb_ref.shape
    nck = (cnt_ref[i, e] + rc - 1) // rc
    half = d // 2 if d % (2 * LANES) == 0 else d

    @pl.when((e == 0) & (f == 0))
    def _():
        o_ref[...] = jnp.zeros_like(o_ref)

    @pl.when(f == 0)
    def _():
        rrow = rankt_ref[pl.ds(e, 1), :]
        riota = lax.broadcasted_iota(jnp.int32, (rc, tm), 0).astype(jnp.float32)

        def compact(k, _):
            r0 = pl.multiple_of(k * rc, rc)
            sel = jnp.where(rrow - r0.astype(jnp.float32) == riota, 1.0, 0.0).astype(jnp.bfloat16)
            xc[pl.ds(r0, rc), :] = jnp.dot(sel, xb_ref[...], preferred_element_type=jnp.float32).astype(xc.dtype)
            yc[pl.ds(r0, rc), :] = jnp.zeros((rc, d), jnp.float32)
            return 0

        lax.fori_loop(0, nck, compact, 0)

    def expert(k, _):
        r0 = pl.multiple_of(k * rc, rc)
        xk = xc[pl.ds(r0, rc), :]
        hg = jnp.dot(xk, wg_ref[0], preferred_element_type=jnp.float32)
        hu = jnp.dot(xk, wu_ref[0], preferred_element_type=jnp.float32)
        h = (hg * jax.nn.sigmoid(hg) * hu).astype(jnp.bfloat16)
        yc[pl.ds(r0, rc), :] += jnp.dot(h, wd_ref[0], preferred_element_type=jnp.float32)
        return 0

    lax.fori_loop(0, nck, expert, 0)

    @pl.when(f == pl.num_programs(2) - 1)
    def _():
        lane = lax.broadcasted_iota(jnp.int32, g_ref.shape, 1)
        rcol = jnp.sum(jnp.where(lane == e, rank_ref[...], 0.0), axis=1, keepdims=True)
        gcol = jnp.sum(jnp.where(lane == e, g_ref[...], 0.0), axis=1, keepdims=True)
        ciota = lax.broadcasted_iota(jnp.int32, (tm, rc), 1).astype(jnp.float32)

        def scatter(k, _):
            r0 = pl.multiple_of(k * rc, rc)
            selt = jnp.where(rcol - r0.astype(jnp.float32) == ciota, 1.0, 0.0).astype(jnp.bfloat16)
            for c0 in range(0, d, half):
                yk = yc[pl.ds(r0, rc), c0:c0 + half].astype(jnp.bfloat16)
                o_ref[:, c0:c0 + half] += gcol * jnp.dot(selt, yk, preferred_element_type=jnp.float32)
            return 0

        lax.fori_loop(0, nck, scatter, 0)


def _add_ln_kernel(x_ref, y_ref, lg_ref, lb_ref, o_ref, ob_ref, *, alpha):
    y = _ln(alpha * x_ref[...] + y_ref[...], lg_ref[...], lb_ref[...])
    o_ref[...] = y
    ob_ref[...] = y.astype(ob_ref.dtype)


def _moe(xb, x, router_p, wg, wu, wd, lg, lb, alpha):
    n, d = x.shape
    ne, _, ff = wg.shape
    tr = _tile(n, 1024)
    gate = pl.pallas_call(
        _router_kernel, grid=(n // tr,),
        in_specs=[pl.BlockSpec((tr, d), lambda i: (i, 0)), _const_spec(router_p.shape)],
        out_specs=pl.BlockSpec((tr, LANES), lambda i: (i, 0)),
        out_shape=jax.ShapeDtypeStruct((n, LANES), jnp.float32),
        compiler_params=_cparams(("parallel",)), name="moe_router")(x, router_p)

    tm = _tile(n, 2048)
    rc = -(-(tm * TOP_K * 9) // (ne * 8 * 2 * 16)) * 16
    n_rows = -(-tm // rc) * rc
    nt = n // tm
    routed = gate > 0.0
    rank = jnp.cumsum(routed.reshape(nt, tm, LANES).astype(jnp.int32), axis=1) - 1
    rank = jnp.where(routed.reshape(nt, tm, LANES), rank, -1).astype(jnp.float32)
    cnt = jnp.sum(routed.reshape(nt, tm, LANES)[:, :, :ne], axis=1).astype(jnp.int32)
    rank_t = rank[:, :, :SUBLANES].transpose(0, 2, 1).reshape(nt * SUBLANES, tm)
    rank = rank.reshape(n, LANES)

    tf = next(c for c in (512, 256, 128, ff) if ff % c == 0)
    row = lambda w: pl.BlockSpec((tm, w), lambda i, e, f, c: (i, 0))
    grid_spec = pltpu.PrefetchScalarGridSpec(
        num_scalar_prefetch=1, grid=(nt, ne, ff // tf),
        in_specs=[row(d), row(LANES), row(LANES),
                  pl.BlockSpec((SUBLANES, tm), lambda i, e, f, c: (i, 0)),
                  pl.BlockSpec((1, d, tf), lambda i, e, f, c: (e, 0, f)),
                  pl.BlockSpec((1, d, tf), lambda i, e, f, c: (e, 0, f)),
                  pl.BlockSpec((1, tf, d), lambda i, e, f, c: (e, f, 0))],
        out_specs=row(d),
        scratch_shapes=[pltpu.VMEM((n_rows, d), jnp.bfloat16), pltpu.VMEM((n_rows, d), jnp.float32)])
    y = pl.pallas_call(
        functools.partial(_moe_kernel, rc=rc), grid_spec=grid_spec,
        out_shape=jax.ShapeDtypeStruct((n, d), jnp.float32),
        compiler_params=_cparams(("arbitrary", "arbitrary", "arbitrary")), name="moe")(
            cnt, xb, gate, rank, rank_t, wg, wu, wd)

    ta = _tile(n, 1024)
    rowa = pl.BlockSpec((ta, d), lambda i: (i, 0))
    return pl.pallas_call(
        functools.partial(_add_ln_kernel, alpha=alpha), grid=(n // ta,),
        in_specs=[rowa, rowa, _const_spec(lg.shape), _const_spec(lb.shape)], out_specs=[rowa, rowa],
        out_shape=[jax.ShapeDtypeStruct((n, d), jnp.float32), jax.ShapeDtypeStruct((n, d), jnp.bfloat16)],
        compiler_params=_cparams(("parallel",)), name="add_ln")(x, y, lg, lb)


def _rope_tables(pos):
    rot = HEAD_DIM // ROPE_FRAC
    half = rot // 2
    freqs = jnp.power(ROPE_THETA, -jnp.arange(half, dtype=jnp.float32) / half)
    ang = pos.astype(jnp.float32)[:, None] * freqs
    cos, sin = jnp.cos(ang), jnp.sin(ang)
    t = pos.shape[0]
    ones = jnp.ones((t, HEAD_DIM - rot), jnp.float32)
    zeros = jnp.zeros((t, HEAD_DIM - rot), jnp.float32)
    zh = jnp.zeros((t, half), jnp.float32)
    c = jnp.concatenate([cos, cos, ones], axis=1)
    a = jnp.concatenate([-sin, zh, zeros], axis=1)
    b = jnp.concatenate([zh, sin, zeros], axis=1)
    rep = LANES // HEAD_DIM
    return jnp.tile(c, (1, rep)), jnp.tile(a, (1, rep)), jnp.tile(b, (1, rep))


def _layer_params(l, w_in_t, gmlp_ln_g, gmlp_ln_b, gmlp_ws, gmlp_bs, conv_b_w, conv_c_w, conv_c_bias, conf_ln_g,
                  conf_ln_b, w_br_a, w_br_b, w_br_c, w_br_d, w_o, ln1_g, ln1_b, ln2_g, ln2_b, dtype):
    bf = dtype
    w = w_in_t[:, l, :].astype(dtype)
    att_end = MIX_W + Q_W + 2 * KV_W + QI_W + IDX_DIM + N_IDX_HEADS
    w_att = jnp.pad(w[MIX_W:att_end], ((0, ATT_W - (att_end - MIX_W)), (0, 0)))
    row = lambda a: a[l][None, :]
    gw = D_A // G_A
    return {
        "w_mix": w[:MIX_W], "w_att": w_att, "w_gat": w[att_end:],
        "gmlp_ln_g": row(gmlp_ln_g), "gmlp_ln_b": row(gmlp_ln_b), "gmlp_ws": gmlp_ws[l],
        "gmlp_bias_full": jnp.repeat(gmlp_bs[l].T, gw, axis=1),
        "gmlp_ws0": jnp.repeat(gmlp_ws[l][:, 0, 0], gw)[None, :],
        "gmlp_bs0": jnp.repeat(gmlp_bs[l][:, 0], gw)[None, :],
        "conv_b_w": conv_b_w[l], "conv_c_w": conv_c_w[l], "conv_c_bias": row(conv_c_bias),
        "conf_ln_g": row(conf_ln_g), "conf_ln_b": row(conf_ln_b),
        "w_br_a": w_br_a[l].astype(bf), "w_br_b": w_br_b[l].astype(bf), "w_br_c": w_br_c[l].astype(bf),
        "w_br_d": w_br_d[l].astype(bf), "w_o": w_o[l].astype(bf),
        "ln1_g": row(ln1_g), "ln1_b": row(ln1_b), "ln2_g": row(ln2_g), "ln2_b": row(ln2_b),
    }


def _channel_mixer(l, xb, x, lp, ffn_w, moe_w, alpha, tm, precise=False):
    j = l // 2
    if l % 2 == 0:
        wg, wu, wd = ffn_w
        return _ffn(x if precise else xb, x, wg[j], wu[j], wd[j], lp["ln2_g"], lp["ln2_b"], alpha, tm, precise)
    router, wg, wu, wd = moe_w
    return _moe(xb, x, router[j], wg[j], wu[j], wd[j], lp["ln2_g"], lp["ln2_b"], alpha)


def kernel(x_prompt, x_sample, cache_k, cache_v, cache_idx_k, state_conv_b, state_conv_c, page_table,
           w_in, gmlp_ln_g, gmlp_ln_b, gmlp_ws, gmlp_bs, conv_b_w, conv_c_w, conv_c_bias, conf_ln_g, conf_ln_b,
           w_br_a, w_br_b, w_br_c, w_br_d, w_o, ln1_g, ln1_b, ln2_g, ln2_b,
           ffn_w_gate, ffn_w_up, ffn_w_down, moe_router, moe_w_gate, moe_w_up, moe_w_down):
    bf = jnp.bfloat16
    nb, t, d = x_prompt.shape
    ns, ts, _ = x_sample.shape
    assert ts == 1 and t % Q_BLOCK == 0
    depth = w_in.shape[0]
    alpha = float((2 * depth) ** 0.25)
    past = page_table.shape[1] * PAGE

    ffn_w = (ffn_w_gate.astype(bf), ffn_w_up.astype(bf), ffn_w_down.astype(bf))
    router_p = jnp.pad(moe_router, ((0, 0), (0, 0), (0, LANES - N_EXPERTS)))
    moe_w = (router_p, moe_w_gate.astype(bf), moe_w_up.astype(bf), moe_w_down.astype(bf))

    rope_p = _rope_tables(jnp.arange(t, dtype=jnp.int32))
    rope_s = tuple(jnp.tile(r, (ns, 1)) for r in _rope_tables(past + jnp.arange(1, dtype=jnp.int32)))
    n_pool = cache_k.shape[1]
    cache_kt = cache_k.transpose(0, 1, 3, 4, 2).reshape(depth, n_pool, KV_W, PAGE)
    cache_vt = cache_v.transpose(0, 1, 3, 4, 2).reshape(depth, n_pool, KV_W, PAGE)
    cache_kit = cache_idx_k.transpose(0, 1, 3, 2)

    n = nb * t
    tm_p = _tile(t, 512)
    tm_f = _tile(n, 1024)
    xp = x_prompt.reshape(n, d)
    xs = x_sample.reshape(ns, d)
    xp_b, xs_b = xp.astype(bf), xs.astype(bf)
    outs = {k: [] for k in ("kp", "vp", "kip", "cbp", "ccp", "ks", "vs", "kis", "cbs", "ccs", "gvs")}
    w_in_t = w_in.transpose(2, 0, 1)
    w_in_tb = w_in_t.astype(bf)
    ffn_w_f = (ffn_w_gate, ffn_w_up, ffn_w_down)
    layer_tensors = (gmlp_ln_g, gmlp_ln_b, gmlp_ws, gmlp_bs, conv_b_w, conv_c_w, conv_c_bias, conf_ln_g, conf_ln_b,
                     w_br_a, w_br_b, w_br_c, w_br_d, w_o, ln1_g, ln1_b, ln2_g, ln2_b)
    for l in range(depth):
        lp = _layer_params(l, w_in_tb, *layer_tensors, bf)
        lp_s = _layer_params(l, w_in_t, *layer_tensors, jnp.float32)
        mix, gsig, h_att, q_b, kv_b, qi_b, ki_b = _in_projection(
            xp_b, lp["w_mix"], lp["w_att"], lp["w_gat"], *rope_p, t // tm_p, tm_p)
        merged_abc, cb, cc = _mix_prompt(mix, gsig, lp, nb, t, tm_p)
        wi = h_att[:, ATT_W - LANES + IDX_DIM:ATT_W - LANES + IDX_DIM + N_IDX_HEADS]
        y_d = _dsa_prompt(q_b, kv_b, qi_b, ki_b, wi, nb, t)
        x1, x1_b = _post(merged_abc, y_d, gsig, xp, lp, alpha, tm_p)
        xp, xp_b = _channel_mixer(l, x1_b, x1, lp, ffn_w, moe_w, alpha, tm_f)
        outs["kp"].append(h_att[:, Q_W:Q_W + KV_W].reshape(nb, t, N_KV, HEAD_DIM))
        outs["vp"].append(h_att[:, Q_W + KV_W:Q_W + 2 * KV_W].reshape(nb, t, N_KV, HEAD_DIM))
        outs["kip"].append(h_att[:, ATT_W - LANES:ATT_W - LANES + IDX_DIM].reshape(nb, t, IDX_DIM))
        outs["cbp"].append(cb)
        outs["ccp"].append(cc)
        mix, gsig, h_att, _, _, _, _ = _in_projection(
            xs, lp_s["w_mix"], lp_s["w_att"], lp_s["w_gat"], *rope_s, 1, ns, precise=True)
        merged_abc, v_rows, zb, hc = _mix_sample(mix, gsig, state_conv_b[l], state_conv_c[l], lp_s)
        wi = h_att[:, ATT_W - LANES + IDX_DIM:ATT_W - LANES + IDX_DIM + N_IDX_HEADS]
        y_d = _dsa_sample(h_att, wi, cache_kt, cache_vt, cache_kit, page_table, l)
        x1, x1_b = _post(merged_abc, y_d, gsig, xs, lp_s, alpha, ns, precise=True)
        xs, _ = _channel_mixer(l, x1_b, x1, lp_s, ffn_w_f, moe_w, alpha, ns, precise=True)
        outs["ks"].append(h_att[:, Q_W:Q_W + KV_W].reshape(ns, 1, N_KV, HEAD_DIM))
        outs["vs"].append(h_att[:, Q_W + KV_W:Q_W + 2 * KV_W].reshape(ns, 1, N_KV, HEAD_DIM))
        outs["kis"].append(h_att[:, ATT_W - LANES:ATT_W - LANES + IDX_DIM].reshape(ns, 1, IDX_DIM))
        outs["cbs"].append(jnp.concatenate([state_conv_b[l][:, 1:], zb[:, None, :]], axis=1))
        outs["ccs"].append(jnp.concatenate([state_conv_c[l][:, 1:], hc[:, None, :]], axis=1))
        outs["gvs"].append(v_rows[:, None, :])
    st = lambda k: jnp.stack(outs[k])
    return (xp.reshape(nb, t, d), xs.reshape(ns, 1, d), st("kp"), st("vp"), st("kip"), st("cbp"), st("ccp"),
            st("ks"), st("vs"), st("kis"), st("cbs"), st("ccs"), st("gvs"))
```

```python
import functools

import jax
import jax.numpy as jnp
from jax import lax
from jax.experimental import pallas as pl
from jax.experimental.pallas import tpu as pltpu

PAGE = 128
CHUNK = 128
D_A, G_A = 256, 4
D_B, W_B = 256, 3
D_C, W_C = 256, 31
N_HEADS, N_KV, HEAD_DIM = 8, 2, 64
N_IDX_HEADS, IDX_DIM = 4, 64
TOPK_MAX = 256
Q_BLOCK = 128
ROPE_THETA = 500000.0
ROPE_FRAC = 4
N_EXPERTS, TOP_K = 8, 2
N_BRANCHES = 4
LN_EPS = 1e-5

LANES = 128
SUBLANES = 8
VMEM_LIMIT = 56 * 1024 * 1024

MIX_W = 2 * D_A + 3 * D_B + 2 * D_C
Q_W, KV_W, QI_W = N_HEADS * HEAD_DIM, N_KV * HEAD_DIM, N_IDX_HEADS * IDX_DIM
ATT_W = Q_W + 2 * KV_W + QI_W + LANES
INT_MIN = -2 ** 31
NEG = -1e30

_NT = (((1,), (1,)), ((), ()))


def _tile(n, pref):
    t = min(pref, n)
    while t >= 16:
        if n % t == 0 and t % 16 == 0:
            return t
        t -= 16
    return n


def _cparams(sem):
    return pltpu.CompilerParams(dimension_semantics=sem, vmem_limit_bytes=VMEM_LIMIT)


def _ln(x, g, b):
    mu = jnp.mean(x, axis=-1, keepdims=True)
    xc = x - mu
    var = jnp.mean(xc * xc, axis=-1, keepdims=True)
    return xc * lax.rsqrt(var + LN_EPS) * g + b


def _const_spec(shape):
    nd = len(shape)
    return pl.BlockSpec(shape, lambda *_: (0,) * nd)


def _proj_mix_kernel(x_ref, w_ref, o_ref, *, precision):
    o_ref[...] = lax.dot_general(x_ref[...], w_ref[...], _NT, preferred_element_type=jnp.float32,
                                 precision=precision)


def _proj_gates_kernel(x_ref, w_ref, o_ref, *, precision):
    h = lax.dot_general(x_ref[...], w_ref[...], _NT, preferred_element_type=jnp.float32, precision=precision)
    o_ref[...] = jax.nn.sigmoid(h).astype(o_ref.dtype)


def _proj_attn_kernel(x_ref, w_ref, c_ref, a_ref, b_ref, h_ref, q_ref, kv_ref, qi_ref, ki_ref, *, precision):
    h = lax.dot_general(x_ref[...], w_ref[...], _NT, preferred_element_type=jnp.float32,
                        precision=precision)
    cc, aa, bb = c_ref[...], a_ref[...], b_ref[...]
    lane = lax.broadcasted_iota(jnp.int32, cc.shape, 1)
    n_grp = ATT_W // LANES
    v_grp = (Q_W + KV_W) // LANES
    outs = []
    for gi in range(n_grp):
        xg = h[:, gi * LANES:(gi + 1) * LANES]
        if gi == v_grp:
            outs.append(xg)
            continue
        rot = xg * cc + pltpu.roll(xg, LANES - 8, 1) * aa + pltpu.roll(xg, 8, 1) * bb
        if gi == n_grp - 1:
            wi_scale = float(N_IDX_HEADS * IDX_DIM) ** -0.5
            rot = jnp.where(lane < IDX_DIM, rot, xg * wi_scale)
        outs.append(rot)
    hr = jnp.concatenate(outs, axis=1)
    h_ref[...] = hr
    q_ref[...] = (hr[:, :Q_W] * (HEAD_DIM ** -0.5)).astype(q_ref.dtype)
    kv_ref[...] = hr[:, Q_W:Q_W + 2 * KV_W].astype(kv_ref.dtype)
    qi_ref[...] = hr[:, Q_W + 2 * KV_W:Q_W + 2 * KV_W + QI_W].astype(qi_ref.dtype)
    ki_ref[...] = hr[:, ATT_W - LANES:].astype(ki_ref.dtype)


def _in_projection(xb, w_mix, w_att, w_gat, rope_c, rope_a, rope_b, n_pos_tiles, tm, precise=False):
    n, d = xb.shape
    grid = (n // tm,)
    precision = lax.Precision.HIGHEST if precise else None
    row = lambda w: pl.BlockSpec((tm, w), lambda i: (i, 0))
    mix = pl.pallas_call(
        functools.partial(_proj_mix_kernel, precision=precision), grid=grid,
        in_specs=[row(d), _const_spec(w_mix.shape)], out_specs=row(MIX_W),
        out_shape=jax.ShapeDtypeStruct((n, MIX_W), jnp.float32),
        compiler_params=_cparams(("parallel",)), name="proj_mix")(xb, w_mix)
    gw = w_gat.shape[0]
    gsig = pl.pallas_call(
        functools.partial(_proj_gates_kernel, precision=precision), grid=grid,
        in_specs=[row(d), _const_spec(w_gat.shape)], out_specs=row(gw),
        out_shape=jax.ShapeDtypeStruct((n, gw), jnp.float32 if precise else jnp.bfloat16),
        compiler_params=_cparams(("parallel",)), name="proj_gates")(xb, w_gat)
    tab = pl.BlockSpec((tm, LANES), lambda i: (i % n_pos_tiles, 0))
    h_att, q_b, kv_b, qi_b, ki_b = pl.pallas_call(
        functools.partial(_proj_attn_kernel, precision=precision), grid=grid,
        in_specs=[row(d), _const_spec(w_att.shape), tab, tab, tab],
        out_specs=[row(ATT_W), row(Q_W), row(2 * KV_W), row(QI_W), row(LANES)],
        out_shape=[jax.ShapeDtypeStruct((n, ATT_W), jnp.float32),
                   jax.ShapeDtypeStruct((n, Q_W), jnp.bfloat16),
                   jax.ShapeDtypeStruct((n, 2 * KV_W), jnp.bfloat16),
                   jax.ShapeDtypeStruct((n, QI_W), jnp.bfloat16),
                   jax.ShapeDtypeStruct((n, LANES), jnp.bfloat16)],
        compiler_params=_cparams(("parallel",)), name="proj_attn")(xb, w_att, rope_c, rope_a, rope_b)
    return mix, gsig, h_att, q_b, kv_b, qi_b, ki_b


HALO_B, HALO_C = 8, 32
CONV_ROWS = 64


def _mix_prompt_kernel(mix_ref, g0_ref, g1_ref, g2_ref, lag_ref, lab_ref, ws_ref, bias_ref,
                       wcb_ref, wcc_ref, bdw_ref, lcg_ref, lcb_ref, wa_ref, wb_ref, wc_ref,
                       out_ref, cb_ref, cc_ref, ext_b, ext_c, conv_b, conv_c):
    tm = mix_ref.shape[0]
    j = pl.program_id(1)

    @pl.when(j == 0)
    def _():
        ext_b[0:HALO_B, :] = jnp.zeros((HALO_B, D_B), jnp.float32)
        ext_c[0:HALO_C, :] = jnp.zeros((HALO_C, D_C), jnp.float32)

    mix = mix_ref[...]
    ua, va = mix[:, 0:D_A], mix[:, D_A:2 * D_A]
    o = 2 * D_A
    bx, cx, xin = mix[:, o:o + D_B], mix[:, o + D_B:o + 2 * D_B], mix[:, o + 2 * D_B:o + 3 * D_B]
    o += 3 * D_B
    ca, cgate = mix[:, o:o + D_C], mix[:, o + D_C:o + 2 * D_C]

    v = _ln(va, lag_ref[...], lab_ref[...])
    vb = v.astype(jnp.bfloat16)
    lane = lax.broadcasted_iota(jnp.int32, (CHUNK, D_A), 1)
    r_i = lax.broadcasted_iota(jnp.int32, (CHUNK, CHUNK), 0)
    c_i = lax.broadcasted_iota(jnp.int32, (CHUNK, CHUNK), 1)
    gw = D_A // G_A
    s_rows = []
    for c in range(tm // CHUNK):
        vc = vb[c * CHUNK:(c + 1) * CHUNK, :]
        sc = bias_ref[...]
        for g in range(G_A):
            wsg = jnp.where(c_i <= r_i, ws_ref[g], 0.0).astype(jnp.bfloat16)
            sg = jnp.dot(wsg, vc, preferred_element_type=jnp.float32)
            sc = sc + jnp.where((lane >= g * gw) & (lane < (g + 1) * gw), sg, 0.0)
        s_rows.append(sc)
    y_a = ua * jnp.concatenate(s_rows, axis=0)

    zb = cx * xin
    hc = ca * jax.nn.sigmoid(cgate)
    ext_b[HALO_B:HALO_B + tm, :] = zb
    ext_c[HALO_C:HALO_C + tm, :] = hc

    for r in range(tm // CONV_ROWS):
        r0 = r * CONV_ROWS
        acc = jnp.zeros((CONV_ROWS, D_B), jnp.float32)
        for d in range(W_B):
            acc = acc + wcb_ref[W_B - 1 - d:W_B - d, :] * ext_b[r0 + HALO_B - d:r0 + HALO_B - d + CONV_ROWS, :]
        conv_b[r0:r0 + CONV_ROWS, :] = acc
        acc = jnp.zeros((CONV_ROWS, D_C), jnp.float32)
        for d in range(W_C):
            acc = acc + wcc_ref[W_C - 1 - d:W_C - d, :] * ext_c[r0 + HALO_C - d:r0 + HALO_C - d + CONV_ROWS, :]
        conv_c[r0:r0 + CONV_ROWS, :] = acc
    y_b = bx * conv_b[...]
    yc = _ln(conv_c[...] + bdw_ref[...], lcg_ref[...], lcb_ref[...])
    y_c = yc * jax.nn.sigmoid(yc)

    ext_b[0:HALO_B, :] = zb[tm - HALO_B:, :]
    ext_c[0:HALO_C, :] = hc[tm - HALO_C:, :]
    cb_ref[0] = zb[tm - (W_B - 1):, :]
    cc_ref[0] = hc[tm - (W_C - 1):, :]

    pa = jnp.dot(y_a.astype(jnp.bfloat16), wa_ref[...], preferred_element_type=jnp.float32)
    pb = jnp.dot(y_b.astype(jnp.bfloat16), wb_ref[...], preferred_element_type=jnp.float32)
    pc = jnp.dot(y_c.astype(jnp.bfloat16), wc_ref[...], preferred_element_type=jnp.float32)
    out_ref[...] = (g0_ref[...].astype(jnp.float32) * pa + g1_ref[...].astype(jnp.float32) * pb
                    + g2_ref[...].astype(jnp.float32) * pc)


def _mix_prompt(mix, gsig, lp, nb, t, tm):
    n = mix.shape[0]
    d = lp["w_br_a"].shape[1]
    nt = t // tm
    row = lambda w, col=0: pl.BlockSpec((tm, w), lambda b, j, col=col: (b * nt + j, col))
    consts = [lp["gmlp_ln_g"], lp["gmlp_ln_b"], lp["gmlp_ws"], lp["gmlp_bias_full"], lp["conv_b_w"],
              lp["conv_c_w"], lp["conv_c_bias"], lp["conf_ln_g"], lp["conf_ln_b"],
              lp["w_br_a"], lp["w_br_b"], lp["w_br_c"]]
    out, cb, cc = pl.pallas_call(
        _mix_prompt_kernel, grid=(nb, nt),
        in_specs=[row(MIX_W), row(d, 0), row(d, 1), row(d, 2)] + [_const_spec(c.shape) for c in consts],
        out_specs=[row(d),
                   pl.BlockSpec((1, W_B - 1, D_B), lambda b, j: (b, 0, 0)),
                   pl.BlockSpec((1, W_C - 1, D_C), lambda b, j: (b, 0, 0))],
        out_shape=[jax.ShapeDtypeStruct((n, d), jnp.float32),
                   jax.ShapeDtypeStruct((nb, W_B - 1, D_B), jnp.float32),
                   jax.ShapeDtypeStruct((nb, W_C - 1, D_C), jnp.float32)],
        scratch_shapes=[pltpu.VMEM((HALO_B + tm, D_B), jnp.float32),
                        pltpu.VMEM((HALO_C + tm, D_C), jnp.float32),
                        pltpu.VMEM((tm, D_B), jnp.float32),
                        pltpu.VMEM((tm, D_C), jnp.float32)],
        compiler_params=_cparams(("arbitrary", "arbitrary")), name="mix_prompt")(
            mix, gsig, gsig, gsig, *consts)
    return out, cb, cc


def _mix_sample_kernel(mix_ref, g0_ref, g1_ref, g2_ref, pb_ref, pc_ref, lag_ref, lab_ref, ws0_ref, bs0_ref,
                       wcb_ref, wcc_ref, bdw_ref, lcg_ref, lcb_ref, wa_ref, wb_ref, wc_ref,
                       out_ref, v_ref, zb_ref, hc_ref):
    mix = mix_ref[...]
    ua, va = mix[:, 0:D_A], mix[:, D_A:2 * D_A]
    o = 2 * D_A
    bx, cx, xin = mix[:, o:o + D_B], mix[:, o + D_B:o + 2 * D_B], mix[:, o + 2 * D_B:o + 3 * D_B]
    o += 3 * D_B
    ca, cgate = mix[:, o:o + D_C], mix[:, o + D_C:o + 2 * D_C]

    v = _ln(va, lag_ref[...], lab_ref[...])
    v_ref[...] = v
    y_a = ua * (ws0_ref[...] * v + bs0_ref[...])

    zb = cx * xin
    zb_ref[...] = zb
    acc = wcb_ref[W_B - 1:W_B, :] * zb
    for k in range(W_B - 1):
        acc = acc + wcb_ref[k:k + 1, :] * pb_ref[k]
    y_b = bx * acc

    hc = ca * jax.nn.sigmoid(cgate)
    hc_ref[...] = hc
    acc = wcc_ref[W_C - 1:W_C, :] * hc
    for k in range(W_C - 1):
        acc = acc + wcc_ref[k:k + 1, :] * pc_ref[k]
    yc = _ln(acc + bdw_ref[...], lcg_ref[...], lcb_ref[...])
    y_c = yc * jax.nn.sigmoid(yc)

    full = dict(preferred_element_type=jnp.float32, precision=lax.Precision.HIGHEST)
    pa = jnp.dot(y_a, wa_ref[...], **full)
    pb = jnp.dot(y_b, wb_ref[...], **full)
    pc = jnp.dot(y_c, wc_ref[...], **full)
    out_ref[...] = g0_ref[...] * pa + g1_ref[...] * pb + g2_ref[...] * pc


def _mix_sample(mix, gsig, prev_b, prev_c, lp):
    n = mix.shape[0]
    d = lp["w_br_a"].shape[1]
    gcol = lambda col: pl.BlockSpec((n, d), lambda i, col=col: (0, col))
    pbt = jnp.transpose(prev_b, (1, 0, 2))
    pct = jnp.transpose(prev_c, (1, 0, 2))
    consts = [lp["gmlp_ln_g"], lp["gmlp_ln_b"], lp["gmlp_ws0"], lp["gmlp_bs0"], lp["conv_b_w"],
              lp["conv_c_w"], lp["conv_c_bias"], lp["conf_ln_g"], lp["conf_ln_b"],
              lp["w_br_a"], lp["w_br_b"], lp["w_br_c"]]
    return pl.pallas_call(
        _mix_sample_kernel, grid=(1,),
        in_specs=[_const_spec(mix.shape), gcol(0), gcol(1), gcol(2), _const_spec(pbt.shape),
                  _const_spec(pct.shape)] + [_const_spec(c.shape) for c in consts],
        out_specs=[_const_spec((n, d)), _const_spec((n, D_A)), _const_spec((n, D_B)), _const_spec((n, D_C))],
        out_shape=[jax.ShapeDtypeStruct((n, d), jnp.float32), jax.ShapeDtypeStruct((n, D_A), jnp.float32),
                   jax.ShapeDtypeStruct((n, D_B), jnp.float32), jax.ShapeDtypeStruct((n, D_C), jnp.float32)],
        compiler_params=_cparams(("arbitrary",)), name="mix_sample")(
            mix, gsig, gsig, gsig, pbt, pct, *consts)


def _masked_score(score, visible):
    return jnp.where(visible, score, -jnp.inf)


def _code_to_float(code):
    return pltpu.bitcast(jnp.where(code < 0, code ^ jnp.int32(0x7FFFFFFF), code), jnp.float32)


def _threshold_search(count_ge, n_top, shape):
    def step(t, cur):
        cand = cur + lax.shift_left(jnp.int32(1), 31 - t)
        return jnp.where(count_ge(_code_to_float(cand)) >= n_top, cand, cur)

    code = lax.fori_loop(0, 32, step, jnp.full(shape, INT_MIN, jnp.int32))
    return code, jnp.where(code == INT_MIN, -jnp.inf, _code_to_float(code))


KEY_CHUNK = 512


COUNT_ROWS = 64
HEADS_PER_KV = N_HEADS // N_KV


def _dsa_prompt_kernel(q_ref, qi_ref, wt_ref, ki_ref, k_ref, v_ref, o_ref,
                       skey, bias, satt, mscr, lscr, oscr, *, n_top):
    i = pl.program_id(1)
    sub = KEY_CHUNK // Q_BLOCK
    nch = (i + sub) // sub
    rowck = lax.broadcasted_iota(jnp.int32, (KEY_CHUNK, Q_BLOCK), 0)
    qpos = i * Q_BLOCK + lax.broadcasted_iota(jnp.int32, (1, Q_BLOCK), 1)

    def for_chunks(body, init):
        carry = lax.fori_loop(0, nch // 2, lambda c, x: body(2 * c + 1, body(2 * c, x)), init)
        return lax.cond(nch % 2 == 1, lambda x: body(nch - 1, x), lambda x: x, carry)

    def score_chunk(c, carry):
        r0 = pl.multiple_of(c * KEY_CHUNK, KEY_CHUNK)
        kc = ki_ref[0, pl.ds(r0, KEY_CHUNK), :]
        acc = jnp.zeros((KEY_CHUNK, Q_BLOCK), jnp.float32)
        for h in range(N_IDX_HEADS):
            dots = jnp.dot(kc, qi_ref[0, h], preferred_element_type=jnp.float32)
            acc = acc + wt_ref[0, h:h + 1, :] * jnp.maximum(dots, 0.0)
        skey[pl.ds(r0, KEY_CHUNK), :] = _masked_score(acc, rowck <= qpos - r0)
        return carry

    for_chunks(score_chunk, 0)

    def count(pred_fn):
        def body(c, acc):
            r0 = pl.multiple_of(c * KEY_CHUNK, KEY_CHUNK)
            m = pred_fn(skey[pl.ds(r0, KEY_CHUNK), :])
            return acc + m.reshape(KEY_CHUNK // COUNT_ROWS, COUNT_ROWS, Q_BLOCK).sum(axis=0)
        acc = lax.fori_loop(0, nch, body, jnp.zeros((COUNT_ROWS, Q_BLOCK), jnp.float32))
        return acc.sum(axis=0, keepdims=True)

    code, thr = _threshold_search(lambda v: count(lambda s: jnp.where(s >= v, 1.0, 0.0)), n_top, (1, Q_BLOCK))
    need = jnp.where(code == INT_MIN, 0.0, n_top - count(lambda s: jnp.where(s > thr, 1.0, 0.0)))

    row128 = lax.broadcasted_iota(jnp.int32, (Q_BLOCK, Q_BLOCK), 0)
    col128 = lax.broadcasted_iota(jnp.int32, (Q_BLOCK, Q_BLOCK), 1)
    before = jnp.where(col128 < row128, 1.0, 0.0).astype(jnp.bfloat16)

    def bias_chunk(c, seen):
        r0 = pl.multiple_of(c * KEY_CHUNK, KEY_CHUNK)
        s = skey[pl.ds(r0, KEY_CHUNK), :]
        for jj in range(sub):
            sj = s[jj * Q_BLOCK:(jj + 1) * Q_BLOCK, :]
            tie = jnp.where(sj == thr, 1.0, 0.0)
            rank = jnp.dot(before, tie.astype(jnp.bfloat16), preferred_element_type=jnp.float32) + seen
            bt = jnp.where(sj > thr, 0.0, jnp.where(sj == thr, jnp.where(rank < need, 0.0, NEG), NEG))
            bias[c * sub + jj] = bt.T
            seen = seen + jnp.sum(tie, axis=0, keepdims=True)
        return seen

    for_chunks(bias_chunk, jnp.zeros((1, Q_BLOCK), jnp.float32))

    for g in range(N_KV):
        qg = q_ref[0, g, 0]
        mscr[...] = jnp.full(mscr.shape, NEG, jnp.float32)
        lscr[...] = jnp.zeros(lscr.shape, jnp.float32)
        oscr[...] = jnp.zeros(oscr.shape, jnp.float32)

        def pass_a(c, carry):
            s = jnp.dot(qg, k_ref[0, g, c], preferred_element_type=jnp.float32)
            for hh in range(HEADS_PER_KV):
                mx = mscr[hh]
                for jj in range(sub):
                    sj = s[hh * Q_BLOCK:(hh + 1) * Q_BLOCK, jj * Q_BLOCK:(jj + 1) * Q_BLOCK] + bias[c * sub + jj]
                    satt[c * sub + jj, hh] = sj
                    mx = jnp.maximum(mx, sj)
                mscr[hh] = mx
            return carry

        for_chunks(pass_a, 0)
        ms = [jnp.max(mscr[hh], axis=1, keepdims=True) for hh in range(HEADS_PER_KV)]

        def pass_b(c, carry):
            r0 = pl.multiple_of(c * KEY_CHUNK, KEY_CHUNK)
            rows = []
            for hh in range(HEADS_PER_KV):
                ps = [jnp.exp(satt[c * sub + jj, hh] - ms[hh]) for jj in range(sub)]
                tot = ps[0]
                for p in ps[1:]:
                    tot = tot + p
                lscr[hh] += tot
                rows.append(jnp.concatenate([p.astype(jnp.bfloat16) for p in ps], axis=1))
            pc = jnp.concatenate(rows, axis=0)
            oscr[...] += jnp.dot(pc, v_ref[0, g, pl.ds(r0, KEY_CHUNK), :], preferred_element_type=jnp.float32)
            return carry

        for_chunks(pass_b, 0)
        for hh in range(HEADS_PER_KV):
            l = jnp.sum(lscr[hh], axis=1, keepdims=True)
            o_ref[0, g, 0, hh * Q_BLOCK:(hh + 1) * Q_BLOCK, :] = (
                oscr[hh * Q_BLOCK:(hh + 1) * Q_BLOCK, :] / l).astype(o_ref.dtype)


def _dsa_prompt(q_b, kv_b, qi_b, ki_b, wi, nb, t):
    n = q_b.shape[0]
    n_top = min(TOPK_MAX, t // 4)
    nqb = t // Q_BLOCK
    t_pad = ((t + KEY_CHUNK - 1) // KEY_CHUNK) * KEY_CHUNK
    rows_m = HEADS_PER_KV * Q_BLOCK
    qh = q_b.reshape(nb, nqb, Q_BLOCK, N_KV, HEADS_PER_KV, HEAD_DIM).transpose(0, 3, 1, 4, 2, 5)
    qh = qh.reshape(nb, N_KV, nqb, rows_m, HEAD_DIM)
    qih = qi_b.reshape(nb, t, N_IDX_HEADS, IDX_DIM).transpose(0, 2, 3, 1)
    kv = kv_b.reshape(nb, t, 2, N_KV, HEAD_DIM)
    kh = kv[:, :, 0].transpose(0, 2, 1, 3)
    vh = kv[:, :, 1].transpose(0, 2, 1, 3)
    if t_pad != t:
        padk = ((0, 0), (0, 0), (0, t_pad - t), (0, 0))
        kh, vh = jnp.pad(kh, padk), jnp.pad(vh, padk)
    n_chunks = t_pad // KEY_CHUNK
    kh = kh.reshape(nb, N_KV, n_chunks, KEY_CHUNK, HEAD_DIM).transpose(0, 1, 2, 4, 3)
    kib = ki_b[:, :IDX_DIM].reshape(nb, t, IDX_DIM)
    if t_pad != t:
        kib = jnp.pad(kib, ((0, 0), (0, t_pad - t), (0, 0)))
    wt = jnp.pad(wi.reshape(nb, t, N_IDX_HEADS).transpose(0, 2, 1), ((0, 0), (0, SUBLANES - N_IDX_HEADS), (0, 0)))
    kern = functools.partial(_dsa_prompt_kernel, n_top=n_top)
    once = dict(pipeline_mode=pl.Buffered(1))
    nblk = t_pad // Q_BLOCK
    out = pl.pallas_call(
        kern, grid=(nb, nqb),
        in_specs=[pl.BlockSpec((1, N_KV, 1, rows_m, HEAD_DIM), lambda b, i: (b, 0, i, 0, 0)),
                  pl.BlockSpec((1, N_IDX_HEADS, IDX_DIM, Q_BLOCK), lambda b, i: (b, 0, 0, i)),
                  pl.BlockSpec((1, SUBLANES, Q_BLOCK), lambda b, i: (b, 0, i)),
                  pl.BlockSpec((1, t_pad, IDX_DIM), lambda b, i: (b, 0, 0), **once),
                  pl.BlockSpec((1, N_KV, n_chunks, HEAD_DIM, KEY_CHUNK), lambda b, i: (b, 0, 0, 0, 0), **once),
                  pl.BlockSpec((1, N_KV, t_pad, HEAD_DIM), lambda b, i: (b, 0, 0, 0), **once)],
        out_specs=pl.BlockSpec((1, N_KV, 1, rows_m, HEAD_DIM), lambda b, i: (b, 0, i, 0, 0)),
        out_shape=jax.ShapeDtypeStruct((nb, N_KV, nqb, rows_m, HEAD_DIM), jnp.bfloat16),
        scratch_shapes=[pltpu.VMEM((t_pad, Q_BLOCK), jnp.float32),
                        pltpu.VMEM((nblk, Q_BLOCK, Q_BLOCK), jnp.float32),
                        pltpu.VMEM((nblk, HEADS_PER_KV, Q_BLOCK, Q_BLOCK), jnp.float32),
                        pltpu.VMEM((HEADS_PER_KV, Q_BLOCK, Q_BLOCK), jnp.float32),
                        pltpu.VMEM((HEADS_PER_KV, Q_BLOCK, Q_BLOCK), jnp.float32),
                        pltpu.VMEM((rows_m, HEAD_DIM), jnp.float32)],
        compiler_params=_cparams(("arbitrary", "arbitrary")), name="dsa_prompt")(qh, qih, wt, kib, kh, vh)
    out = out.reshape(nb, N_KV, nqb, HEADS_PER_KV, Q_BLOCK, HEAD_DIM).transpose(0, 2, 4, 1, 3, 5)
    return out.reshape(n, Q_W)


def _page_unroll(n_pages):
    return 4 if n_pages % 4 == 0 else 1


def _dsa_s_score_kernel(pt_ref, qit_ref, w_ref, kin_ref, cache_ref, o_ref, buf, sem, *, layer):
    b = pl.program_id(0)
    n_pages = pt_ref.shape[1]
    unroll = _page_unroll(n_pages)

    def page_copy(j):
        return pltpu.make_async_copy(cache_ref.at[layer, pt_ref[b, j]], buf.at[j], sem)

    def start(j, _):
        page_copy(j).start()
        return 0

    def wait(j, _):
        page_copy(j).wait()
        return 0

    lax.fori_loop(0, n_pages, start, 0)
    lax.fori_loop(0, n_pages, wait, 0)
    qit = qit_ref[0]
    w = w_ref[0]
    qb = [jnp.broadcast_to(qit[:, h:h + 1], (IDX_DIM, PAGE)) for h in range(N_IDX_HEADS)]

    def page_scores(tile):
        s = jnp.zeros((1, PAGE), jnp.float32)
        for h in range(N_IDX_HEADS):
            dots = jnp.sum(tile * qb[h], axis=0, keepdims=True)
            s = s + w[:, h:h + 1] * jnp.maximum(dots, 0.0)
        return s

    def body(jo, _):
        for u in range(unroll):
            j = jo * unroll + u
            o_ref[0, pl.ds(j, 1), :] = page_scores(buf[j])
        return 0

    lax.fori_loop(0, n_pages // unroll, body, 0)
    d_new = jnp.sum(qit * kin_ref[0], axis=0, keepdims=True)
    s_new = jnp.sum(w * jnp.maximum(d_new, 0.0), axis=1, keepdims=True)
    lane = lax.broadcasted_iota(jnp.int32, (1, LANES), 1)
    o_ref[0, n_pages:n_pages + 1, :] = jnp.where(lane == 0, s_new, 0.0)


def _dsa_s_select_kernel(s_ref, o_ref, *, n_top, n_keys, idx_bits):
    lane = lax.broadcasted_iota(jnp.int32, s_ref.shape, 1)
    s = _masked_score(s_ref[...], lane < n_keys)
    rows = (s.shape[0], 1)
    count_ge = lambda v: jnp.sum(jnp.where(s >= v, 1.0, 0.0), axis=1, keepdims=True)
    code, thr = _threshold_search(count_ge, n_top, rows)
    need = n_top - jnp.sum(jnp.where(s > thr, 1.0, 0.0), axis=1, keepdims=True)

    def cut_step(t, cur):
        cand = cur + lax.shift_left(jnp.int32(1), idx_bits - 1 - t)
        cnt = jnp.sum(jnp.where(s == thr, jnp.where(lane < cand, 1.0, 0.0), 0.0), axis=1, keepdims=True)
        return jnp.where(cnt <= need, cand, cur)

    cut = lax.fori_loop(0, idx_bits, cut_step, jnp.zeros(rows, jnp.int32))
    cut = jnp.where(code == INT_MIN, 0, cut)
    sel = jnp.where(s > thr, 1, jnp.where(s == thr, jnp.where(lane < cut, 1, 0), 0))
    o_ref[...] = jnp.where(sel == 1, 0.0, NEG).astype(jnp.float32)


def _dsa_s_attn_kernel(pt_ref, qt_ref, bias_ref, kn_ref, vn_ref, ck_ref, cv_ref, o_ref,
                       kbuf, vbuf, sscr, sem_k, sem_v, *, layer):
    b = pl.program_id(0)
    n_pages = pt_ref.shape[1]
    unroll = _page_unroll(n_pages)
    slot = b % 2

    def k_copy(bb, sl, j):
        return pltpu.make_async_copy(ck_ref.at[layer, pt_ref[bb, j]], kbuf.at[sl, j], sem_k.at[sl])

    def v_copy(bb, sl, j):
        return pltpu.make_async_copy(cv_ref.at[layer, pt_ref[bb, j]], vbuf.at[sl, j], sem_v.at[sl])

    def start_all(bb, sl):
        def start(j, _):
            k_copy(bb, sl, j).start()
            v_copy(bb, sl, j).start()
            return 0
        lax.fori_loop(0, n_pages, start, 0)

    def wait_k(j, _):
        k_copy(b, slot, j).wait()
        return 0

    def wait_v(j, _):
        v_copy(b, slot, j).wait()
        return 0

    @pl.when(b == 0)
    def _():
        start_all(b, slot)

    @pl.when(b + 1 < pl.num_programs(0))
    def _():
        start_all(b + 1, 1 - slot)

    qt = qt_ref[0]
    qb = [jnp.broadcast_to(qt[:, h:h + 1], (HEAD_DIM, PAGE)) for h in range(N_HEADS)]
    rows_of = lambda h: slice((h // HEADS_PER_KV) * HEAD_DIM, (h // HEADS_PER_KV + 1) * HEAD_DIM)

    def page_s(tile, brow):
        rows = [jnp.sum(tile[rows_of(h), :] * qb[h], axis=0, keepdims=True) for h in range(N_HEADS)]
        return jnp.concatenate(rows, axis=0) + brow

    lax.fori_loop(0, n_pages, wait_k, 0)

    def pass1(jo, mx):
        for u in range(unroll):
            j = jo * unroll + u
            s = page_s(kbuf[slot, j], bias_ref[0, pl.ds(j, 1), :])
            sscr[j] = s
            mx = jnp.maximum(mx, s)
        return mx

    mx = lax.fori_loop(0, n_pages // unroll, pass1, jnp.full((N_HEADS, PAGE), NEG, jnp.float32))
    s_new = page_s(kn_ref[0], bias_ref[0, n_pages:n_pages + 1, :])
    m = jnp.max(jnp.maximum(mx, s_new), axis=1, keepdims=True)

    def pass2(jo, lacc):
        for u in range(unroll):
            j = jo * unroll + u
            p = jnp.exp(sscr[j] - m)
            sscr[j] = p
            lacc = lacc + p
        return lacc

    p_new = jnp.exp(s_new - m)
    lacc = lax.fori_loop(0, n_pages // unroll, pass2, p_new)
    sscr[n_pages] = p_new
    lax.fori_loop(0, n_pages, wait_v, 0)

    cols = []
    for h in range(N_HEADS):
        def pass3(jo, acc):
            for u in range(unroll):
                j = jo * unroll + u
                acc = acc + vbuf[slot, j, rows_of(h), :] * sscr[j, h:h + 1, :]
            return acc

        acc = lax.fori_loop(0, n_pages // unroll, pass3, vn_ref[0, rows_of(h), :] * p_new[h:h + 1, :])
        l = jnp.sum(lacc[h:h + 1, :], axis=1, keepdims=True)
        cols.append(jnp.sum(acc, axis=1, keepdims=True) / l)
    o_ref[0] = jnp.concatenate(cols, axis=1)


def _dsa_sample(h_att, wi, cache_kt, cache_vt, cache_kit, page_table, layer):
    nb = h_att.shape[0]
    n_pages = page_table.shape[1]
    n_keys = n_pages * PAGE + 1
    n_top = min(TOPK_MAX, n_keys // 4)
    lk = (n_pages + 1) * LANES
    q = h_att[:, :Q_W] * (HEAD_DIM ** -0.5)
    k_new = h_att[:, Q_W:Q_W + KV_W]
    v_new = h_att[:, Q_W + KV_W:Q_W + 2 * KV_W]
    qi = h_att[:, Q_W + 2 * KV_W:Q_W + 2 * KV_W + QI_W]
    ki_new = h_att[:, ATT_W - LANES:ATT_W - LANES + IDX_DIM]

    qit = jnp.pad(qi.reshape(nb, N_IDX_HEADS, IDX_DIM), ((0, 0), (0, SUBLANES - N_IDX_HEADS), (0, 0)))
    qit = qit.transpose(0, 2, 1)
    w8 = jnp.pad(wi, ((0, 0), (0, SUBLANES - N_IDX_HEADS)))[:, None, :]
    grid_spec = pltpu.PrefetchScalarGridSpec(
        num_scalar_prefetch=1, grid=(nb,),
        in_specs=[pl.BlockSpec((1, IDX_DIM, SUBLANES), lambda b, pt: (b, 0, 0)),
                  pl.BlockSpec((1, 1, SUBLANES), lambda b, pt: (b, 0, 0)),
                  pl.BlockSpec((1, IDX_DIM, 1), lambda b, pt: (b, 0, 0)),
                  pl.BlockSpec(memory_space=pl.ANY)],
        out_specs=pl.BlockSpec((1, n_pages + 1, LANES), lambda b, pt: (b, 0, 0)),
        scratch_shapes=[pltpu.VMEM((n_pages, IDX_DIM, PAGE), jnp.float32), pltpu.SemaphoreType.DMA])
    scores = pl.pallas_call(
        functools.partial(_dsa_s_score_kernel, layer=layer), grid_spec=grid_spec,
        out_shape=jax.ShapeDtypeStruct((nb, n_pages + 1, LANES), jnp.float32),
        compiler_params=_cparams(("arbitrary",)), name="dsa_sample_score")(
            page_table, qit, w8, ki_new[:, :, None], cache_kit)

    idx_bits = max(1, (lk + 1).bit_length())
    bias = pl.pallas_call(
        functools.partial(_dsa_s_select_kernel, n_top=n_top, n_keys=n_keys, idx_bits=idx_bits),
        grid=(1,), in_specs=[_const_spec((nb, lk))], out_specs=_const_spec((nb, lk)),
        out_shape=jax.ShapeDtypeStruct((nb, lk), jnp.float32),
        compiler_params=_cparams(("arbitrary",)), name="dsa_sample_select")(scores.reshape(nb, lk))

    qt = q.reshape(nb, N_HEADS, HEAD_DIM).transpose(0, 2, 1)
    first_col = ((0, 0), (0, 0), (0, PAGE - 1))
    grid_spec = pltpu.PrefetchScalarGridSpec(
        num_scalar_prefetch=1, grid=(nb,),
        in_specs=[pl.BlockSpec((1, HEAD_DIM, N_HEADS), lambda b, pt: (b, 0, 0)),
                  pl.BlockSpec((1, n_pages + 1, LANES), lambda b, pt: (b, 0, 0)),
                  pl.BlockSpec((1, KV_W, PAGE), lambda b, pt: (b, 0, 0)),
                  pl.BlockSpec((1, KV_W, PAGE), lambda b, pt: (b, 0, 0)),
                  pl.BlockSpec(memory_space=pl.ANY), pl.BlockSpec(memory_space=pl.ANY)],
        out_specs=pl.BlockSpec((1, HEAD_DIM, N_HEADS), lambda b, pt: (b, 0, 0)),
        scratch_shapes=[pltpu.VMEM((2, n_pages, KV_W, PAGE), jnp.float32),
                        pltpu.VMEM((2, n_pages, KV_W, PAGE), jnp.float32),
                        pltpu.VMEM((n_pages + 1, N_HEADS, PAGE), jnp.float32),
                        pltpu.SemaphoreType.DMA((2,)), pltpu.SemaphoreType.DMA((2,))])
    o_t = pl.pallas_call(
        functools.partial(_dsa_s_attn_kernel, layer=layer), grid_spec=grid_spec,
        out_shape=jax.ShapeDtypeStruct((nb, HEAD_DIM, N_HEADS), jnp.float32),
        compiler_params=_cparams(("arbitrary",)), name="dsa_sample_attn")(
            page_table, qt, bias.reshape(nb, n_pages + 1, LANES),
            jnp.pad(k_new[:, :, None], first_col), jnp.pad(v_new[:, :, None], first_col),
            cache_kt, cache_vt)
    return o_t.transpose(0, 2, 1).reshape(nb, Q_W)


def _post_kernel(m_ref, yd_ref, g3_ref, x_ref, wd_ref, wo_ref, lg_ref, lb_ref, o_ref, ob_ref, *, alpha, precision):
    pd = jnp.dot(yd_ref[...], wd_ref[...], preferred_element_type=jnp.float32, precision=precision)
    merged = m_ref[...] + g3_ref[...].astype(jnp.float32) * pd
    mix = jnp.dot(merged.astype(wo_ref.dtype), wo_ref[...], preferred_element_type=jnp.float32,
                  precision=precision)
    y = _ln(alpha * x_ref[...] + mix, lg_ref[...], lb_ref[...])
    o_ref[...] = y
    ob_ref[...] = y.astype(ob_ref.dtype)


def _post(merged_abc, yd, gsig, x, lp, alpha, tm, precise=False):
    n, d = x.shape
    row = lambda w, col=0: pl.BlockSpec((tm, w), lambda i, col=col: (i, col))
    consts = [lp["w_br_d"], lp["w_o"], lp["ln1_g"], lp["ln1_b"]]
    precision = lax.Precision.HIGHEST if precise else None
    return pl.pallas_call(
        functools.partial(_post_kernel, alpha=alpha, precision=precision), grid=(n // tm,),
        in_specs=[row(d), row(Q_W), row(d, N_BRANCHES - 1), row(d)] + [_const_spec(c.shape) for c in consts],
        out_specs=[row(d), row(d)],
        out_shape=[jax.ShapeDtypeStruct((n, d), jnp.float32), jax.ShapeDtypeStruct((n, d), jnp.bfloat16)],
        compiler_params=_cparams(("parallel",)), name="post")(merged_abc, yd, gsig, x, *consts)


def _ffn_kernel(xb_ref, x_ref, wg_ref, wu_ref, wd_ref, lg_ref, lb_ref, o_ref, ob_ref, acc, *, alpha, precision):
    f = pl.program_id(1)

    @pl.when(f == 0)
    def _():
        acc[...] = jnp.zeros_like(acc)

    xb = xb_ref[...]
    hg = jnp.dot(xb, wg_ref[...], preferred_element_type=jnp.float32, precision=precision)
    hu = jnp.dot(xb, wu_ref[...], preferred_element_type=jnp.float32, precision=precision)
    h = (hg * jax.nn.sigmoid(hg) * hu).astype(wd_ref.dtype)
    acc[...] += jnp.dot(h, wd_ref[...], preferred_element_type=jnp.float32, precision=precision)

    @pl.when(f == pl.num_programs(1) - 1)
    def _():
        y = _ln(alpha * x_ref[...] + acc[...], lg_ref[...], lb_ref[...])
        o_ref[...] = y
        ob_ref[...] = y.astype(ob_ref.dtype)


def _ffn(xb, x, wg, wu, wd, lg, lb, alpha, tm, precise=False):
    n, d = x.shape
    ff = wg.shape[1]
    tf = next(c for c in (512, 256, 128, ff) if ff % c == 0)
    row = pl.BlockSpec((tm, d), lambda i, f: (i, 0))
    precision = lax.Precision.HIGHEST if precise else None
    return pl.pallas_call(
        functools.partial(_ffn_kernel, alpha=alpha, precision=precision), grid=(n // tm, ff // tf),
        in_specs=[row, row, pl.BlockSpec((d, tf), lambda i, f: (0, f)), pl.BlockSpec((d, tf), lambda i, f: (0, f)),
                  pl.BlockSpec((tf, d), lambda i, f: (f, 0)), _const_spec(lg.shape), _const_spec(lb.shape)],
        out_specs=[row, row],
        out_shape=[jax.ShapeDtypeStruct((n, d), jnp.float32), jax.ShapeDtypeStruct((n, d), jnp.bfloat16)],
        scratch_shapes=[pltpu.VMEM((tm, d), jnp.float32)],
        compiler_params=_cparams(("parallel", "arbitrary")), name="ffn")(xb, x, wg, wu, wd, lg, lb)


def _router_kernel(x_ref, r_ref, g_ref):
    lane = lax.broadcasted_iota(jnp.int32, g_ref.shape, 1)
    lanef = lane.astype(jnp.float32)
    logits = jnp.dot(x_ref[...], r_ref[...], preferred_element_type=jnp.float32, precision=lax.Precision.HIGHEST)
    logits = jnp.where(lane < N_EXPERTS, logits, -jnp.inf)
    m1 = jnp.max(logits, axis=1, keepdims=True)
    i1 = jnp.min(jnp.where(logits == m1, lanef, float(LANES)), axis=1, keepdims=True)
    rest = jnp.where(lanef == i1, -jnp.inf, logits)
    m2 = jnp.max(rest, axis=1, keepdims=True)
    i2 = jnp.min(jnp.where(rest == m2, lanef, float(LANES)), axis=1, keepdims=True)
    e2 = jnp.exp(m2 - m1)
    g_ref[...] = jnp.where(lanef == i1, 1.0 / (1.0 + e2), jnp.where(lanef == i2, e2 / (1.0 + e2), 0.0))


def _moe_kernel(cnt_ref, xb_ref, g_ref, rank_ref, rankt_ref, wg_ref, wu_ref, wd_ref, o_ref, xc, yc, *, rc):
    i, e, f = pl.program_id(0), pl.program_id(1), pl.program_id(2)
    tm, d = xb_ref.shape
    nck = (cnt_ref[i, e] + rc - 1) // rc
    half = d // 2 if d % (2 * LANES) == 0 else d

    @pl.when((e == 0) & (f == 0))
    def _():
        o_ref[...] = jnp.zeros_like(o_ref)

    @pl.when(f == 0)
    def _():
        rrow = rankt_ref[pl.ds(e, 1), :]
        riota = lax.broadcasted_iota(jnp.int32, (rc, tm), 0).astype(jnp.float32)

        def compact(k, _):
            r0 = pl.multiple_of(k * rc, rc)
            sel = jnp.where(rrow - r0.astype(jnp.float32) == riota, 1.0, 0.0).astype(jnp.bfloat16)
            xc[pl.ds(r0, rc), :] = jnp.dot(sel, xb_ref[...], preferred_element_type=jnp.float32).astype(xc.dtype)
            yc[pl.ds(r0, rc), :] = jnp.zeros((rc, d), jnp.float32)
            return 0

        lax.fori_loop(0, nck, compact, 0)

    def expert(k, _):
        r0 = pl.multiple_of(k * rc, rc)
        xk = xc[pl.ds(r0, rc), :]
        hg = jnp.dot(xk, wg_ref[0], preferred_element_type=jnp.float32)
        hu = jnp.dot(xk, wu_ref[0], preferred_element_type=jnp.float32)
        h = (hg * jax.nn.sigmoid(hg) * hu).astype(jnp.bfloat16)
        yc[pl.ds(r0, rc), :] += jnp.dot(h, wd_ref[0], preferred_element_type=jnp.float32)
        return 0

    lax.fori_loop(0, nck, expert, 0)

    @pl.when(f == pl.num_programs(2) - 1)
    def _():
        lane = lax.broadcasted_iota(jnp.int32, g_ref.shape, 1)
        rcol = jnp.sum(jnp.where(lane == e, rank_ref[...], 0.0), axis=1, keepdims=True)
        gcol = jnp.sum(jnp.where(lane == e, g_ref[...], 0.0), axis=1, keepdims=True)
        ciota = lax.broadcasted_iota(jnp.int32, (tm, rc), 1).astype(jnp.float32)

        def scatter(k, _):
            r0 = pl.multiple_of(k * rc, rc)
            selt = jnp.where(rcol - r0.astype(jnp.float32) == ciota, 1.0, 0.0).astype(jnp.bfloat16)
            for c0 in range(0, d, half):
                yk = yc[pl.ds(r0, rc), c0:c0 + half].astype(jnp.bfloat16)
                o_ref[:, c0:c0 + half] += gcol * jnp.dot(selt, yk, preferred_element_type=jnp.float32)
            return 0

        lax.fori_loop(0, nck, scatter, 0)


def _add_ln_kernel(x_ref, y_ref, lg_ref, lb_ref, o_ref, ob_ref, *, alpha):
    y = _ln(alpha * x_ref[...] + y_ref[...], lg_ref[...], lb_ref[...])
    o_ref[...] = y
    ob_ref[...] = y.astype(ob_ref.dtype)


def _moe(xb, x, router_p, wg, wu, wd, lg, lb, alpha):
    n, d = x.shape
    ne, _, ff = wg.shape
    tr = _tile(n, 1024)
    gate = pl.pallas_call(
        _router_kernel, grid=(n // tr,),
        in_specs=[pl.BlockSpec((tr, d), lambda i: (i, 0)), _const_spec(router_p.shape)],
        out_specs=pl.BlockSpec((tr, LANES), lambda i: (i, 0)),
        out_shape=jax.ShapeDtypeStruct((n, LANES), jnp.float32),
        compiler_params=_cparams(("parallel",)), name="moe_router")(x, router_p)

    tm = _tile(n, 2048)
    rc = -(-(tm * TOP_K * 9) // (ne * 8 * 2 * 16)) * 16
    n_rows = -(-tm // rc) * rc
    nt = n // tm
    routed = gate > 0.0
    rank = jnp.cumsum(routed.reshape(nt, tm, LANES).astype(jnp.int32), axis=1) - 1
    rank = jnp.where(routed.reshape(nt, tm, LANES), rank, -1).astype(jnp.float32)
    cnt = jnp.sum(routed.reshape(nt, tm, LANES)[:, :, :ne], axis=1).astype(jnp.int32)
    rank_t = rank[:, :, :SUBLANES].transpose(0, 2, 1).reshape(nt * SUBLANES, tm)
    rank = rank.reshape(n, LANES)

    tf = next(c for c in (512, 256, 128, ff) if ff % c == 0)
    row = lambda w: pl.BlockSpec((tm, w), lambda i, e, f, c: (i, 0))
    grid_spec = pltpu.PrefetchScalarGridSpec(
        num_scalar_prefetch=1, grid=(nt, ne, ff // tf),
        in_specs=[row(d), row(LANES), row(LANES),
                  pl.BlockSpec((SUBLANES, tm), lambda i, e, f, c: (i, 0)),
                  pl.BlockSpec((1, d, tf), lambda i, e, f, c: (e, 0, f)),
                  pl.BlockSpec((1, d, tf), lambda i, e, f, c: (e, 0, f)),
                  pl.BlockSpec((1, tf, d), lambda i, e, f, c: (e, f, 0))],
        out_specs=row(d),
        scratch_shapes=[pltpu.VMEM((n_rows, d), jnp.bfloat16), pltpu.VMEM((n_rows, d), jnp.float32)])
    y = pl.pallas_call(
        functools.partial(_moe_kernel, rc=rc), grid_spec=grid_spec,
        out_shape=jax.ShapeDtypeStruct((n, d), jnp.float32),
        compiler_params=_cparams(("arbitrary", "arbitrary", "arbitrary")), name="moe")(
            cnt, xb, gate, rank, rank_t, wg, wu, wd)

    ta = _tile(n, 1024)
    rowa = pl.BlockSpec((ta, d), lambda i: (i, 0))
    return pl.pallas_call(
        functools.partial(_add_ln_kernel, alpha=alpha), grid=(n // ta,),
        in_specs=[rowa, rowa, _const_spec(lg.shape), _const_spec(lb.shape)], out_specs=[rowa, rowa],
        out_shape=[jax.ShapeDtypeStruct((n, d), jnp.float32), jax.ShapeDtypeStruct((n, d), jnp.bfloat16)],
        compiler_params=_cparams(("parallel",)), name="add_ln")(x, y, lg, lb)


def _rope_tables(pos):
    rot = HEAD_DIM // ROPE_FRAC
    half = rot // 2
    freqs = jnp.power(ROPE_THETA, -jnp.arange(half, dtype=jnp.float32) / half)
    ang = pos.astype(jnp.float32)[:, None] * freqs
    cos, sin = jnp.cos(ang), jnp.sin(ang)
    t = pos.shape[0]
    ones = jnp.ones((t, HEAD_DIM - rot), jnp.float32)
    zeros = jnp.zeros((t, HEAD_DIM - rot), jnp.float32)
    zh = jnp.zeros((t, half), jnp.float32)
    c = jnp.concatenate([cos, cos, ones], axis=1)
    a = jnp.concatenate([-sin, zh, zeros], axis=1)
    b = jnp.concatenate([zh, sin, zeros], axis=1)
    rep = LANES // HEAD_DIM
    return jnp.tile(c, (1, rep)), jnp.tile(a, (1, rep)), jnp.tile(b, (1, rep))


def _layer_params(l, w_in_t, gmlp_ln_g, gmlp_ln_b, gmlp_ws, gmlp_bs, conv_b_w, conv_c_w, conv_c_bias, conf_ln_g,
                  conf_ln_b, w_br_a, w_br_b, w_br_c, w_br_d, w_o, ln1_g, ln1_b, ln2_g, ln2_b, dtype):
    bf = dtype
    w = w_in_t[:, l, :].astype(dtype)
    att_end = MIX_W + Q_W + 2 * KV_W + QI_W + IDX_DIM + N_IDX_HEADS
    w_att = jnp.pad(w[MIX_W:att_end], ((0, ATT_W - (att_end - MIX_W)), (0, 0)))
    row = lambda a: a[l][None, :]
    gw = D_A // G_A
    return {
        "w_mix": w[:MIX_W], "w_att": w_att, "w_gat": w[att_end:],
        "gmlp_ln_g": row(gmlp_ln_g), "gmlp_ln_b": row(gmlp_ln_b), "gmlp_ws": gmlp_ws[l],
        "gmlp_bias_full": jnp.repeat(gmlp_bs[l].T, gw, axis=1),
        "gmlp_ws0": jnp.repeat(gmlp_ws[l][:, 0, 0], gw)[None, :],
        "gmlp_bs0": jnp.repeat(gmlp_bs[l][:, 0], gw)[None, :],
        "conv_b_w": conv_b_w[l], "conv_c_w": conv_c_w[l], "conv_c_bias": row(conv_c_bias),
        "conf_ln_g": row(conf_ln_g), "conf_ln_b": row(conf_ln_b),
        "w_br_a": w_br_a[l].astype(bf), "w_br_b": w_br_b[l].astype(bf), "w_br_c": w_br_c[l].astype(bf),
        "w_br_d": w_br_d[l].astype(bf), "w_o": w_o[l].astype(bf),
        "ln1_g": row(ln1_g), "ln1_b": row(ln1_b), "ln2_g": row(ln2_g), "ln2_b": row(ln2_b),
    }


def _channel_mixer(l, xb, x, lp, ffn_w, moe_w, alpha, tm, precise=False):
    j = l // 2
    if l % 2 == 0:
        wg, wu, wd = ffn_w
        return _ffn(x if precise else xb, x, wg[j], wu[j], wd[j], lp["ln2_g"], lp["ln2_b"], alpha, tm, precise)
    router, wg, wu, wd = moe_w
    return _moe(xb, x, router[j], wg[j], wu[j], wd[j], lp["ln2_g"], lp["ln2_b"], alpha)


def kernel(x_prompt, x_sample, cache_k, cache_v, cache_idx_k, state_conv_b, state_conv_c, page_table,
           w_in, gmlp_ln_g, gmlp_ln_b, gmlp_ws, gmlp_bs, conv_b_w, conv_c_w, conv_c_bias, conf_ln_g, conf_ln_b,
           w_br_a, w_br_b, w_br_c, w_br_d, w_o, ln1_g, ln1_b, ln2_g, ln2_b,
           ffn_w_gate, ffn_w_up, ffn_w_down, moe_router, moe_w_gate, moe_w_up, moe_w_down):
    bf = jnp.bfloat16
    nb, t, d = x_prompt.shape
    ns, ts, _ = x_sample.shape
    assert ts == 1 and t % Q_BLOCK == 0
    depth = w_in.shape[0]
    alpha = float((2 * depth) ** 0.25)
    past = page_table.shape[1] * PAGE

    ffn_w = (ffn_w_gate.astype(bf), ffn_w_up.astype(bf), ffn_w_down.astype(bf))
    router_p = jnp.pad(moe_router, ((0, 0), (0, 0), (0, LANES - N_EXPERTS)))
    moe_w = (router_p, moe_w_gate.astype(bf), moe_w_up.astype(bf), moe_w_down.astype(bf))

    rope_p = _rope_tables(jnp.arange(t, dtype=jnp.int32))
    rope_s = tuple(jnp.tile(r, (ns, 1)) for r in _rope_tables(past + jnp.arange(1, dtype=jnp.int32)))
    n_pool = cache_k.shape[1]
    cache_kt = cache_k.transpose(0, 1, 3, 4, 2).reshape(depth, n_pool, KV_W, PAGE)
    cache_vt = cache_v.transpose(0, 1, 3, 4, 2).reshape(depth, n_pool, KV_W, PAGE)
    cache_kit = cache_idx_k.transpose(0, 1, 3, 2)

    n = nb * t
    tm_p = _tile(t, 512)
    tm_f = _tile(n, 1024)
    xp = x_prompt.reshape(n, d)
    xs = x_sample.reshape(ns, d)
    xp_b, xs_b = xp.astype(bf), xs.astype(bf)
    outs = {k: [] for k in ("kp", "vp", "kip", "cbp", "ccp", "ks", "vs", "kis", "cbs", "ccs", "gvs")}
    w_in_t = w_in.transpose(2, 0, 1)
    w_in_tb = w_in_t.astype(bf)
    ffn_w_f = (ffn_w_gate, ffn_w_up, ffn_w_down)
    layer_tensors = (gmlp_ln_g, gmlp_ln_b, gmlp_ws, gmlp_bs, conv_b_w, conv_c_w, conv_c_bias, conf_ln_g, conf_ln_b,
                     w_br_a, w_br_b, w_br_c, w_br_d, w_o, ln1_g, ln1_b, ln2_g, ln2_b)
    for l in range(depth):
        lp = _layer_params(l, w_in_tb, *layer_tensors, bf)
        lp_s = _layer_params(l, w_in_t, *layer_tensors, jnp.float32)
        mix, gsig, h_att, q_b, kv_b, qi_b, ki_b = _in_projection(
            xp_b, lp["w_mix"], lp["w_att"], lp["w_gat"], *rope_p, t // tm_p, tm_p)
        merged_abc, cb, cc = _mix_prompt(mix, gsig, lp, nb, t, tm_p)
        wi = h_att[:, ATT_W - LANES + IDX_DIM:ATT_W - LANES + IDX_DIM + N_IDX_HEADS]
        y_d = _dsa_prompt(q_b, kv_b, qi_b, ki_b, wi, nb, t)
        x1, x1_b = _post(merged_abc, y_d, gsig, xp, lp, alpha, tm_p)
        xp, xp_b = _channel_mixer(l, x1_b, x1, lp, ffn_w, moe_w, alpha, tm_f)
        outs["kp"].append(h_att[:, Q_W:Q_W + KV_W].reshape(nb, t, N_KV, HEAD_DIM))
        outs["vp"].append(h_att[:, Q_W + KV_W:Q_W + 2 * KV_W].reshape(nb, t, N_KV, HEAD_DIM))
        outs["kip"].append(h_att[:, ATT_W - LANES:ATT_W - LANES + IDX_DIM].reshape(nb, t, IDX_DIM))
        outs["cbp"].append(cb)
        outs["ccp"].append(cc)
        mix, gsig, h_att, _, _, _, _ = _in_projection(
            xs, lp_s["w_mix"], lp_s["w_att"], lp_s["w_gat"], *rope_s, 1, ns, precise=True)
        merged_abc, v_rows, zb, hc = _mix_sample(mix, gsig, state_conv_b[l], state_conv_c[l], lp_s)
        wi = h_att[:, ATT_W - LANES + IDX_DIM:ATT_W - LANES + IDX_DIM + N_IDX_HEADS]
        y_d = _dsa_sample(h_att, wi, cache_kt, cache_vt, cache_kit, page_table, l)
        x1, x1_b = _post(merged_abc, y_d, gsig, xs, lp_s, alpha, ns, precise=True)
        xs, _ = _channel_mixer(l, x1_b, x1, lp_s, ffn_w_f, moe_w, alpha, ns, precise=True)
        outs["ks"].append(h_att[:, Q_W:Q_W + KV_W].reshape(ns, 1, N_KV, HEAD_DIM))
        outs["vs"].append(h_att[:, Q_W + KV_W:Q_W + 2 * KV_W].reshape(ns, 1, N_KV, HEAD_DIM))
        outs["kis"].append(h_att[:, ATT_W - LANES:ATT_W - LANES + IDX_DIM].reshape(ns, 1, IDX_DIM))
        outs["cbs"].append(jnp.concatenate([state_conv_b[l][:, 1:], zb[:, None, :]], axis=1))
        outs["ccs"].append(jnp.concatenate([state_conv_c[l][:, 1:], hc[:, None, :]], axis=1))
        outs["gvs"].append(v_rows[:, None, :])
    st = lambda k: jnp.stack(outs[k])
    return (xp.reshape(nb, t, d), xs.reshape(ns, 1, d), st("kp"), st("vp"), st("kip"), st("cbp"), st("ccp"),
            st("ks"), st("vs"), st("kis"), st("cbs"), st("ccs"), st("gvs"))
```

```python
import functools

import jax
import jax.numpy as jnp
from jax import lax
from jax.experimental import pallas as pl
from jax.experimental.pallas import tpu as pltpu

PAGE = 128
CHUNK = 128
D_A, G_A = 256, 4
D_B, W_B = 256, 3
D_C, W_C = 256, 31
N_HEADS, N_KV, HEAD_DIM = 8, 2, 64
N_IDX_HEADS, IDX_DIM = 4, 64
TOPK_MAX = 256
Q_BLOCK = 128
ROPE_THETA = 500000.0
ROPE_FRAC = 4
N_EXPERTS, TOP_K = 8, 2
N_BRANCHES = 4
LN_EPS = 1e-5

LANES = 128
SUBLANES = 8
VMEM_LIMIT = 56 * 1024 * 1024

MIX_W = 2 * D_A + 3 * D_B + 2 * D_C
Q_W, KV_W, QI_W = N_HEADS * HEAD_DIM, N_KV * HEAD_DIM, N_IDX_HEADS * IDX_DIM
ATT_W = Q_W + 2 * KV_W + QI_W + LANES
INT_MIN = -2 ** 31
NEG = -1e30

_NT = (((1,), (1,)), ((), ()))


def _tile(n, pref):
    t = min(pref, n)
    while t >= 16:
        if n % t == 0 and t % 16 == 0:
            return t
        t -= 16
    return n


def _cparams(sem):
    return pltpu.CompilerParams(dimension_semantics=sem, vmem_limit_bytes=VMEM_LIMIT)


def _ln(x, g, b):
    mu = jnp.mean(x, axis=-1, keepdims=True)
    xc = x - mu
    var = jnp.mean(xc * xc, axis=-1, keepdims=True)
    return xc * lax.rsqrt(var + LN_EPS) * g + b


def _const_spec(shape):
    nd = len(shape)
    return pl.BlockSpec(shape, lambda *_: (0,) * nd)


def _proj_mix_kernel(x_ref, w_ref, o_ref, *, precision):
    o_ref[...] = lax.dot_general(x_ref[...], w_ref[...], _NT, preferred_element_type=jnp.float32,
                                 precision=precision)


def _proj_gates_kernel(x_ref, w_ref, o_ref, *, precision):
    h = lax.dot_general(x_ref[...], w_ref[...], _NT, preferred_element_type=jnp.float32, precision=precision)
    o_ref[...] = jax.nn.sigmoid(h).astype(o_ref.dtype)


def _proj_attn_kernel(x_ref, w_ref, c_ref, a_ref, b_ref, h_ref, q_ref, kv_ref, qi_ref, ki_ref, *, precision):
    h = lax.dot_general(x_ref[...], w_ref[...], _NT, preferred_element_type=jnp.float32,
                        precision=precision)
    cc, aa, bb = c_ref[...], a_ref[...], b_ref[...]
    lane = lax.broadcasted_iota(jnp.int32, cc.shape, 1)
    n_grp = ATT_W // LANES
    v_grp = (Q_W + KV_W) // LANES
    outs = []
    for gi in range(n_grp):
        xg = h[:, gi * LANES:(gi + 1) * LANES]
        if gi == v_grp:
            outs.append(xg)
            continue
        rot = xg * cc + pltpu.roll(xg, LANES - 8, 1) * aa + pltpu.roll(xg, 8, 1) * bb
        if gi == n_grp - 1:
            wi_scale = float(N_IDX_HEADS * IDX_DIM) ** -0.5
            rot = jnp.where(lane < IDX_DIM, rot, xg * wi_scale)
        outs.append(rot)
    hr = jnp.concatenate(outs, axis=1)
    h_ref[...] = hr
    q_ref[...] = (hr[:, :Q_W] * (HEAD_DIM ** -0.5)).astype(q_ref.dtype)
    kv_ref[...] = hr[:, Q_W:Q_W + 2 * KV_W].astype(kv_ref.dtype)
    qi_ref[...] = hr[:, Q_W + 2 * KV_W:Q_W + 2 * KV_W + QI_W].astype(qi_ref.dtype)
    ki_ref[...] = hr[:, ATT_W - LANES:].astype(ki_ref.dtype)


def _in_projection(xb, w_mix, w_att, w_gat, rope_c, rope_a, rope_b, n_pos_tiles, tm, precise=False):
    n, d = xb.shape
    grid = (n // tm,)
    precision = lax.Precision.HIGHEST if precise else None
    row = lambda w: pl.BlockSpec((tm, w), lambda i: (i, 0))
    mix = pl.pallas_call(
        functools.partial(_proj_mix_kernel, precision=precision), grid=grid,
        in_specs=[row(d), _const_spec(w_mix.shape)], out_specs=row(MIX_W),
        out_shape=jax.ShapeDtypeStruct((n, MIX_W), jnp.float32),
        compiler_params=_cparams(("parallel",)), name="proj_mix")(xb, w_mix)
    gw = w_gat.shape[0]
    gsig = pl.pallas_call(
        functools.partial(_proj_gates_kernel, precision=precision), grid=grid,
        in_specs=[row(d), _const_spec(w_gat.shape)], out_specs=row(gw),
        out_shape=jax.ShapeDtypeStruct((n, gw), jnp.float32 if precise else jnp.bfloat16),
        compiler_params=_cparams(("parallel",)), name="proj_gates")(xb, w_gat)
    tab = pl.BlockSpec((tm, LANES), lambda i: (i % n_pos_tiles, 0))
    h_att, q_b, kv_b, qi_b, ki_b = pl.pallas_call(
        functools.partial(_proj_attn_kernel, precision=precision), grid=grid,
        in_specs=[row(d), _const_spec(w_att.shape), tab, tab, tab],
        out_specs=[row(ATT_W), row(Q_W), row(2 * KV_W), row(QI_W), row(LANES)],
        out_shape=[jax.ShapeDtypeStruct((n, ATT_W), jnp.float32),
                   jax.ShapeDtypeStruct((n, Q_W), jnp.bfloat16),
                   jax.ShapeDtypeStruct((n, 2 * KV_W), jnp.bfloat16),
                   jax.ShapeDtypeStruct((n, QI_W), jnp.bfloat16),
                   jax.ShapeDtypeStruct((n, LANES), jnp.bfloat16)],
        compiler_params=_cparams(("parallel",)), name="proj_attn")(xb, w_att, rope_c, rope_a, rope_b)
    return mix, gsig, h_att, q_b, kv_b, qi_b, ki_b


HALO_B, HALO_C = 8, 32
CONV_ROWS = 64


def _mix_prompt_kernel(mix_ref, g0_ref, g1_ref, g2_ref, lag_ref, lab_ref, ws_ref, bias_ref,
                       wcb_ref, wcc_ref, bdw_ref, lcg_ref, lcb_ref, wa_ref, wb_ref, wc_ref,
                       out_ref, cb_ref, cc_ref, ext_b, ext_c, conv_b, conv_c):
    tm = mix_ref.shape[0]
    j = pl.program_id(1)

    @pl.when(j == 0)
    def _():
        ext_b[0:HALO_B, :] = jnp.zeros((HALO_B, D_B), jnp.float32)
        ext_c[0:HALO_C, :] = jnp.zeros((HALO_C, D_C), jnp.float32)

    mix = mix_ref[...]
    ua, va = mix[:, 0:D_A], mix[:, D_A:2 * D_A]
    o = 2 * D_A
    bx, cx, xin = mix[:, o:o + D_B], mix[:, o + D_B:o + 2 * D_B], mix[:, o + 2 * D_B:o + 3 * D_B]
    o += 3 * D_B
    ca, cgate = mix[:, o:o + D_C], mix[:, o + D_C:o + 2 * D_C]

    v = _ln(va, lag_ref[...], lab_ref[...])
    vb = v.astype(jnp.bfloat16)
    lane = lax.broadcasted_iota(jnp.int32, (CHUNK, D_A), 1)
    r_i = lax.broadcasted_iota(jnp.int32, (CHUNK, CHUNK), 0)
    c_i = lax.broadcasted_iota(jnp.int32, (CHUNK, CHUNK), 1)
    gw = D_A // G_A
    s_rows = []
    for c in range(tm // CHUNK):
        vc = vb[c * CHUNK:(c + 1) * CHUNK, :]
        sc = bias_ref[...]
        for g in range(G_A):
            wsg = jnp.where(c_i <= r_i, ws_ref[g], 0.0).astype(jnp.bfloat16)
            sg = jnp.dot(wsg, vc, preferred_element_type=jnp.float32)
            sc = sc + jnp.where((lane >= g * gw) & (lane < (g + 1) * gw), sg, 0.0)
        s_rows.append(sc)
    y_a = ua * jnp.concatenate(s_rows, axis=0)

    zb = cx * xin
    hc = ca * jax.nn.sigmoid(cgate)
    ext_b[HALO_B:HALO_B + tm, :] = zb
    ext_c[HALO_C:HALO_C + tm, :] = hc

    for r in range(tm // CONV_ROWS):
        r0 = r * CONV_ROWS
        acc = jnp.zeros((CONV_ROWS, D_B), jnp.float32)
        for d in range(W_B):
            acc = acc + wcb_ref[W_B - 1 - d:W_B - d, :] * ext_b[r0 + HALO_B - d:r0 + HALO_B - d + CONV_ROWS, :]
        conv_b[r0:r0 + CONV_ROWS, :] = acc
        acc = jnp.zeros((CONV_ROWS, D_C), jnp.float32)
        for d in range(W_C):
            acc = acc + wcc_ref[W_C - 1 - d:W_C - d, :] * ext_c[r0 + HALO_C - d:r0 + HALO_C - d + CONV_ROWS, :]
        conv_c[r0:r0 + CONV_ROWS, :] = acc
    y_b = bx * conv_b[...]
    yc = _ln(conv_c[...] + bdw_ref[...], lcg_ref[...], lcb_ref[...])
    y_c = yc * jax.nn.sigmoid(yc)

    ext_b[0:HALO_B, :] = zb[tm - HALO_B:, :]
    ext_c[0:HALO_C, :] = hc[tm - HALO_C:, :]
    cb_ref[0] = zb[tm - (W_B - 1):, :]
    cc_ref[0] = hc[tm - (W_C - 1):, :]

    pa = jnp.dot(y_a.astype(jnp.bfloat16), wa_ref[...], preferred_element_type=jnp.float32)
    pb = jnp.dot(y_b.astype(jnp.bfloat16), wb_ref[...], preferred_element_type=jnp.float32)
    pc = jnp.dot(y_c.astype(jnp.bfloat16), wc_ref[...], preferred_element_type=jnp.float32)
    out_ref[...] = (g0_ref[...].astype(jnp.float32) * pa + g1_ref[...].astype(jnp.float32) * pb
                    + g2_ref[...].astype(jnp.float32) * pc)


def _mix_prompt(mix, gsig, lp, nb, t, tm):
    n = mix.shape[0]
    d = lp["w_br_a"].shape[1]
    nt = t // tm
    row = lambda w, col=0: pl.BlockSpec((tm, w), lambda b, j, col=col: (b * nt + j, col))
    consts = [lp["gmlp_ln_g"], lp["gmlp_ln_b"], lp["gmlp_ws"], lp["gmlp_bias_full"], lp["conv_b_w"],
              lp["conv_c_w"], lp["conv_c_bias"], lp["conf_ln_g"], lp["conf_ln_b"],
              lp["w_br_a"], lp["w_br_b"], lp["w_br_c"]]
    out, cb, cc = pl.pallas_call(
        _mix_prompt_kernel, grid=(nb, nt),
        in_specs=[row(MIX_W), row(d, 0), row(d, 1), row(d, 2)] + [_const_spec(c.shape) for c in consts],
        out_specs=[row(d),
                   pl.BlockSpec((1, W_B - 1, D_B), lambda b, j: (b, 0, 0)),
                   pl.BlockSpec((1, W_C - 1, D_C), lambda b, j: (b, 0, 0))],
        out_shape=[jax.ShapeDtypeStruct((n, d), jnp.float32),
                   jax.ShapeDtypeStruct((nb, W_B - 1, D_B), jnp.float32),
                   jax.ShapeDtypeStruct((nb, W_C - 1, D_C), jnp.float32)],
        scratch_shapes=[pltpu.VMEM((HALO_B + tm, D_B), jnp.float32),
                        pltpu.VMEM((HALO_C + tm, D_C), jnp.float32),
                        pltpu.VMEM((tm, D_B), jnp.float32),
                        pltpu.VMEM((tm, D_C), jnp.float32)],
        compiler_params=_cparams(("arbitrary", "arbitrary")), name="mix_prompt")(
            mix, gsig, gsig, gsig, *consts)
    return out, cb, cc


def _mix_sample_kernel(mix_ref, g0_ref, g1_ref, g2_ref, pb_ref, pc_ref, lag_ref, lab_ref, ws0_ref, bs0_ref,
                       wcb_ref, wcc_ref, bdw_ref, lcg_ref, lcb_ref, wa_ref, wb_ref, wc_ref,
                       out_ref, v_ref, zb_ref, hc_ref):
    mix = mix_ref[...]
    ua, va = mix[:, 0:D_A], mix[:, D_A:2 * D_A]
    o = 2 * D_A
    bx, cx, xin = mix[:, o:o + D_B], mix[:, o + D_B:o + 2 * D_B], mix[:, o + 2 * D_B:o + 3 * D_B]
    o += 3 * D_B
    ca, cgate = mix[:, o:o + D_C], mix[:, o + D_C:o + 2 * D_C]

    v = _ln(va, lag_ref[...], lab_ref[...])
    v_ref[...] = v
    y_a = ua * (ws0_ref[...] * v + bs0_ref[...])

    zb = cx * xin
    zb_ref[...] = zb
    acc = wcb_ref[W_B - 1:W_B, :] * zb
    for k in range(W_B - 1):
        acc = acc + wcb_ref[k:k + 1, :] * pb_ref[k]
    y_b = bx * acc

    hc = ca * jax.nn.sigmoid(cgate)
    hc_ref[...] = hc
    acc = wcc_ref[W_C - 1:W_C, :] * hc
    for k in range(W_C - 1):
        acc = acc + wcc_ref[k:k + 1, :] * pc_ref[k]
    yc = _ln(acc + bdw_ref[...], lcg_ref[...], lcb_ref[...])
    y_c = yc * jax.nn.sigmoid(yc)

    full = dict(preferred_element_type=jnp.float32, precision=lax.Precision.HIGHEST)
    pa = jnp.dot(y_a, wa_ref[...], **full)
    pb = jnp.dot(y_b, wb_ref[...], **full)
    pc = jnp.dot(y_c, wc_ref[...], **full)
    out_ref[...] = g0_ref[...] * pa + g1_ref[...] * pb + g2_ref[...] * pc


def _mix_sample(mix, gsig, prev_b, prev_c, lp):
    n = mix.shape[0]
    d = lp["w_br_a"].shape[1]
    gcol = lambda col: pl.BlockSpec((n, d), lambda i, col=col: (0, col))
    pbt = jnp.transpose(prev_b, (1, 0, 2))
    pct = jnp.transpose(prev_c, (1, 0, 2))
    consts = [lp["gmlp_ln_g"], lp["gmlp_ln_b"], lp["gmlp_ws0"], lp["gmlp_bs0"], lp["conv_b_w"],
              lp["conv_c_w"], lp["conv_c_bias"], lp["conf_ln_g"], lp["conf_ln_b"],
              lp["w_br_a"], lp["w_br_b"], lp["w_br_c"]]
    return pl.pallas_call(
        _mix_sample_kernel, grid=(1,),
        in_specs=[_const_spec(mix.shape), gcol(0), gcol(1), gcol(2), _const_spec(pbt.shape),
                  _const_spec(pct.shape)] + [_const_spec(c.shape) for c in consts],
        out_specs=[_const_spec((n, d)), _const_spec((n, D_A)), _const_spec((n, D_B)), _const_spec((n, D_C))],
        out_shape=[jax.ShapeDtypeStruct((n, d), jnp.float32), jax.ShapeDtypeStruct((n, D_A), jnp.float32),
                   jax.ShapeDtypeStruct((n, D_B), jnp.float32), jax.ShapeDtypeStruct((n, D_C), jnp.float32)],
        compiler_params=_cparams(("arbitrary",)), name="mix_sample")(
            mix, gsig, gsig, gsig, pbt, pct, *consts)


def _masked_score(score, visible):
    return jnp.where(visible, score, -jnp.inf)


def _code_to_float(code):
    return pltpu.bitcast(jnp.where(code < 0, code ^ jnp.int32(0x7FFFFFFF), code), jnp.float32)


def _threshold_search(count_ge, count_gt, n_top, shape):
    def try_bit(t, cur, cnt_cur):
        cand = cur + lax.shift_left(jnp.int32(1), 31 - t)
        cnt = count_ge(_code_to_float(cand))
        ok = cnt >= n_top
        return jnp.where(ok, cand, cur), jnp.where(ok, cnt, cnt_cur)

    cur, cnt_cur = try_bit(0, jnp.full(shape, INT_MIN, jnp.int32), jnp.full(shape, 3e38, jnp.float32))
    settled = count_gt(jnp.where(cur == 0, 0.0, -jnp.inf)) < n_top

    def n_open(cnt_cur):
        return jnp.max(jnp.where(settled | (cnt_cur == n_top), 0.0, 1.0))

    def body(st):
        t, cur, cnt_cur, _ = st
        cur, cnt_cur = try_bit(t, cur, cnt_cur)
        return t + 1, cur, cnt_cur, n_open(cnt_cur)

    _, code, _, _ = lax.while_loop(lambda st: (st[0] < 32) & (st[3] > 0.0), body,
                                   (jnp.int32(1), cur, cnt_cur, n_open(cnt_cur)))
    return code, jnp.where(code == INT_MIN, -jnp.inf, _code_to_float(code))


KEY_CHUNK = 512


COUNT_ROWS = 64
HEADS_PER_KV = N_HEADS // N_KV


def _dsa_prompt_kernel(q_ref, qi_ref, wt_ref, ki_ref, k_ref, v_ref, o_ref,
                       skey, bias, satt, mscr, lscr, oscr, *, n_top):
    i = pl.program_id(1)
    sub = KEY_CHUNK // Q_BLOCK
    nch = (i + sub) // sub
    rowck = lax.broadcasted_iota(jnp.int32, (KEY_CHUNK, Q_BLOCK), 0)
    qpos = i * Q_BLOCK + lax.broadcasted_iota(jnp.int32, (1, Q_BLOCK), 1)

    def for_chunks(body, init):
        carry = lax.fori_loop(0, nch // 2, lambda c, x: body(2 * c + 1, body(2 * c, x)), init)
        return lax.cond(nch % 2 == 1, lambda x: body(nch - 1, x), lambda x: x, carry)

    def score_chunk(c, carry):
        r0 = pl.multiple_of(c * KEY_CHUNK, KEY_CHUNK)
        kc = ki_ref[0, pl.ds(r0, KEY_CHUNK), :]
        acc = jnp.zeros((KEY_CHUNK, Q_BLOCK), jnp.float32)
        for h in range(N_IDX_HEADS):
            dots = jnp.dot(kc, qi_ref[0, h], preferred_element_type=jnp.float32)
            acc = acc + wt_ref[0, h:h + 1, :] * jnp.maximum(dots, 0.0)
        skey[pl.ds(r0, KEY_CHUNK), :] = _masked_score(acc, rowck <= qpos - r0)
        return carry

    for_chunks(score_chunk, 0)

    def count(pred_fn):
        def body(c, acc):
            r0 = pl.multiple_of(c * KEY_CHUNK, KEY_CHUNK)
            m = pred_fn(skey[pl.ds(r0, KEY_CHUNK), :])
            return acc + m.reshape(KEY_CHUNK // COUNT_ROWS, COUNT_ROWS, Q_BLOCK).sum(axis=0)
        acc = lax.fori_loop(0, nch, body, jnp.zeros((COUNT_ROWS, Q_BLOCK), jnp.float32))
        return acc.sum(axis=0, keepdims=True)

    code, thr = _threshold_search(lambda v: count(lambda s: jnp.where(s >= v, 1.0, 0.0)),
                                  lambda v: count(lambda s: jnp.where(s > v, 1.0, 0.0)), n_top, (1, Q_BLOCK))
    need = jnp.where(code == INT_MIN, 0.0, n_top - count(lambda s: jnp.where(s > thr, 1.0, 0.0)))

    row128 = lax.broadcasted_iota(jnp.int32, (Q_BLOCK, Q_BLOCK), 0)
    col128 = lax.broadcasted_iota(jnp.int32, (Q_BLOCK, Q_BLOCK), 1)
    before = jnp.where(col128 < row128, 1.0, 0.0).astype(jnp.bfloat16)

    def bias_chunk(c, seen):
        r0 = pl.multiple_of(c * KEY_CHUNK, KEY_CHUNK)
        s = skey[pl.ds(r0, KEY_CHUNK), :]
        for jj in range(sub):
            sj = s[jj * Q_BLOCK:(jj + 1) * Q_BLOCK, :]
            tie = jnp.where(sj == thr, 1.0, 0.0)
            rank = jnp.dot(before, tie.astype(jnp.bfloat16), preferred_element_type=jnp.float32) + seen
            bt = jnp.where(sj > thr, 0.0, jnp.where(sj == thr, jnp.where(rank < need, 0.0, NEG), NEG))
            bias[c * sub + jj] = bt.T
            seen = seen + jnp.sum(tie, axis=0, keepdims=True)
        return seen

    for_chunks(bias_chunk, jnp.zeros((1, Q_BLOCK), jnp.float32))

    for g in range(N_KV):
        qg = q_ref[0, g, 0]
        mscr[...] = jnp.full(mscr.shape, NEG, jnp.float32)
        lscr[...] = jnp.zeros(lscr.shape, jnp.float32)
        oscr[...] = jnp.zeros(oscr.shape, jnp.float32)

        def pass_a(c, carry):
            s = jnp.dot(qg, k_ref[0, g, c], preferred_element_type=jnp.float32)
            for hh in range(HEADS_PER_KV):
                mx = mscr[hh]
                for jj in range(sub):
                    sj = s[hh * Q_BLOCK:(hh + 1) * Q_BLOCK, jj * Q_BLOCK:(jj + 1) * Q_BLOCK] + bias[c * sub + jj]
                    satt[c * sub + jj, hh] = sj
                    mx = jnp.maximum(mx, sj)
                mscr[hh] = mx
            return carry

        for_chunks(pass_a, 0)
        ms = [jnp.max(mscr[hh], axis=1, keepdims=True) for hh in range(HEADS_PER_KV)]

        def pass_b(c, carry):
            r0 = pl.multiple_of(c * KEY_CHUNK, KEY_CHUNK)
            rows = []
            for hh in range(HEADS_PER_KV):
                ps = [jnp.exp(satt[c * sub + jj, hh] - ms[hh]) for jj in range(sub)]
                tot = ps[0]
                for p in ps[1:]:
                    tot = tot + p
                lscr[hh] += tot
                rows.append(jnp.concatenate([p.astype(jnp.bfloat16) for p in ps], axis=1))
            pc = jnp.concatenate(rows, axis=0)
            oscr[...] += jnp.dot(pc, v_ref[0, g, pl.ds(r0, KEY_CHUNK), :], preferred_element_type=jnp.float32)
            return carry

        for_chunks(pass_b, 0)
        for hh in range(HEADS_PER_KV):
            l = jnp.sum(lscr[hh], axis=1, keepdims=True)
            o_ref[0, g, 0, hh * Q_BLOCK:(hh + 1) * Q_BLOCK, :] = (
                oscr[hh * Q_BLOCK:(hh + 1) * Q_BLOCK, :] / l).astype(o_ref.dtype)


def _dsa_prompt(q_b, kv_b, qi_b, ki_b, wi, nb, t):
    n = q_b.shape[0]
    n_top = min(TOPK_MAX, t // 4)
    nqb = t // Q_BLOCK
    t_pad = ((t + KEY_CHUNK - 1) // KEY_CHUNK) * KEY_CHUNK
    rows_m = HEADS_PER_KV * Q_BLOCK
    qh = q_b.reshape(nb, nqb, Q_BLOCK, N_KV, HEADS_PER_KV, HEAD_DIM).transpose(0, 3, 1, 4, 2, 5)
    qh = qh.reshape(nb, N_KV, nqb, rows_m, HEAD_DIM)
    qih = qi_b.reshape(nb, t, N_IDX_HEADS, IDX_DIM).transpose(0, 2, 3, 1)
    kv = kv_b.reshape(nb, t, 2, N_KV, HEAD_DIM)
    kh = kv[:, :, 0].transpose(0, 2, 1, 3)
    vh = kv[:, :, 1].transpose(0, 2, 1, 3)
    if t_pad != t:
        padk = ((0, 0), (0, 0), (0, t_pad - t), (0, 0))
        kh, vh = jnp.pad(kh, padk), jnp.pad(vh, padk)
    n_chunks = t_pad // KEY_CHUNK
    kh = kh.reshape(nb, N_KV, n_chunks, KEY_CHUNK, HEAD_DIM).transpose(0, 1, 2, 4, 3)
    kib = ki_b[:, :IDX_DIM].reshape(nb, t, IDX_DIM)
    if t_pad != t:
        kib = jnp.pad(kib, ((0, 0), (0, t_pad - t), (0, 0)))
    wt = jnp.pad(wi.reshape(nb, t, N_IDX_HEADS).transpose(0, 2, 1), ((0, 0), (0, SUBLANES - N_IDX_HEADS), (0, 0)))
    kern = functools.partial(_dsa_prompt_kernel, n_top=n_top)
    once = dict(pipeline_mode=pl.Buffered(1))
    nblk = t_pad // Q_BLOCK
    out = pl.pallas_call(
        kern, grid=(nb, nqb),
        in_specs=[pl.BlockSpec((1, N_KV, 1, rows_m, HEAD_DIM), lambda b, i: (b, 0, i, 0, 0)),
                  pl.BlockSpec((1, N_IDX_HEADS, IDX_DIM, Q_BLOCK), lambda b, i: (b, 0, 0, i)),
                  pl.BlockSpec((1, SUBLANES, Q_BLOCK), lambda b, i: (b, 0, i)),
                  pl.BlockSpec((1, t_pad, IDX_DIM), lambda b, i: (b, 0, 0), **once),
                  pl.BlockSpec((1, N_KV, n_chunks, HEAD_DIM, KEY_CHUNK), lambda b, i: (b, 0, 0, 0, 0), **once),
                  pl.BlockSpec((1, N_KV, t_pad, HEAD_DIM), lambda b, i: (b, 0, 0, 0), **once)],
        out_specs=pl.BlockSpec((1, N_KV, 1, rows_m, HEAD_DIM), lambda b, i: (b, 0, i, 0, 0)),
        out_shape=jax.ShapeDtypeStruct((nb, N_KV, nqb, rows_m, HEAD_DIM), jnp.bfloat16),
        scratch_shapes=[pltpu.VMEM((t_pad, Q_BLOCK), jnp.float32),
                        pltpu.VMEM((nblk, Q_BLOCK, Q_BLOCK), jnp.float32),
                        pltpu.VMEM((nblk, HEADS_PER_KV, Q_BLOCK, Q_BLOCK), jnp.float32),
                        pltpu.VMEM((HEADS_PER_KV, Q_BLOCK, Q_BLOCK), jnp.float32),
                        pltpu.VMEM((HEADS_PER_KV, Q_BLOCK, Q_BLOCK), jnp.float32),
                        pltpu.VMEM((rows_m, HEAD_DIM), jnp.float32)],
        compiler_params=_cparams(("arbitrary", "arbitrary")), name="dsa_prompt")(qh, qih, wt, kib, kh, vh)
    out = out.reshape(nb, N_KV, nqb, HEADS_PER_KV, Q_BLOCK, HEAD_DIM).transpose(0, 2, 4, 1, 3, 5)
    return out.reshape(n, Q_W)


def _page_unroll(n_pages):
    return 4 if n_pages % 4 == 0 else 1


def _dsa_s_score_kernel(pt_ref, qit_ref, w_ref, kin_ref, cache_ref, o_ref, buf, sem, *, layer):
    b = pl.program_id(0)
    n_pages = pt_ref.shape[1]
    unroll = _page_unroll(n_pages)

    def page_copy(j):
        return pltpu.make_async_copy(cache_ref.at[layer, pt_ref[b, j]], buf.at[j], sem)

    def start(j, _):
        page_copy(j).start()
        return 0

    def wait(j, _):
        page_copy(j).wait()
        return 0

    lax.fori_loop(0, n_pages, start, 0)
    lax.fori_loop(0, n_pages, wait, 0)
    qit = qit_ref[0]
    w = w_ref[0]
    qb = [jnp.broadcast_to(qit[:, h:h + 1], (IDX_DIM, PAGE)) for h in range(N_IDX_HEADS)]

    def page_scores(tile):
        s = jnp.zeros((1, PAGE), jnp.float32)
        for h in range(N_IDX_HEADS):
            dots = jnp.sum(tile * qb[h], axis=0, keepdims=True)
            s = s + w[:, h:h + 1] * jnp.maximum(dots, 0.0)
        return s

    def body(jo, _):
        for u in range(unroll):
            j = jo * unroll + u
            o_ref[0, pl.ds(j, 1), :] = page_scores(buf[j])
        return 0

    lax.fori_loop(0, n_pages // unroll, body, 0)
    d_new = jnp.sum(qit * kin_ref[0], axis=0, keepdims=True)
    s_new = jnp.sum(w * jnp.maximum(d_new, 0.0), axis=1, keepdims=True)
    lane = lax.broadcasted_iota(jnp.int32, (1, LANES), 1)
    o_ref[0, n_pages:n_pages + 1, :] = jnp.where(lane == 0, s_new, 0.0)


def _dsa_s_select_kernel(s_ref, o_ref, *, n_top, n_keys, idx_bits):
    lane = lax.broadcasted_iota(jnp.int32, s_ref.shape, 1)
    s = _masked_score(s_ref[...], lane < n_keys)
    rows = (s.shape[0], 1)
    count_ge = lambda v: jnp.sum(jnp.where(s >= v, 1.0, 0.0), axis=1, keepdims=True)
    count_gt = lambda v: jnp.sum(jnp.where(s > v, 1.0, 0.0), axis=1, keepdims=True)
    code, thr = _threshold_search(count_ge, count_gt, n_top, rows)
    need = n_top - jnp.sum(jnp.where(s > thr, 1.0, 0.0), axis=1, keepdims=True)

    def cut_step(t, cur):
        cand = cur + lax.shift_left(jnp.int32(1), idx_bits - 1 - t)
        cnt = jnp.sum(jnp.where(s == thr, jnp.where(lane < cand, 1.0, 0.0), 0.0), axis=1, keepdims=True)
        return jnp.where(cnt <= need, cand, cur)

    cut = lax.fori_loop(0, idx_bits, cut_step, jnp.zeros(rows, jnp.int32))
    cut = jnp.where(code == INT_MIN, 0, cut)
    sel = jnp.where(s > thr, 1, jnp.where(s == thr, jnp.where(lane < cut, 1, 0), 0))
    o_ref[...] = jnp.where(sel == 1, 0.0, NEG).astype(jnp.float32)


def _dsa_s_attn_kernel(pt_ref, qt_ref, bias_ref, kn_ref, vn_ref, ck_ref, cv_ref, o_ref,
                       kbuf, vbuf, sscr, sem_k, sem_v, *, layer):
    b = pl.program_id(0)
    n_pages = pt_ref.shape[1]
    unroll = _page_unroll(n_pages)
    slot = b % 2

    def k_copy(bb, sl, j):
        return pltpu.make_async_copy(ck_ref.at[layer, pt_ref[bb, j]], kbuf.at[sl, j], sem_k.at[sl])

    def v_copy(bb, sl, j):
        return pltpu.make_async_copy(cv_ref.at[layer, pt_ref[bb, j]], vbuf.at[sl, j], sem_v.at[sl])

    def start_all(bb, sl):
        def start(j, _):
            k_copy(bb, sl, j).start()
            v_copy(bb, sl, j).start()
            return 0
        lax.fori_loop(0, n_pages, start, 0)

    def wait_k(j, _):
        k_copy(b, slot, j).wait()
        return 0

    def wait_v(j, _):
        v_copy(b, slot, j).wait()
        return 0

    @pl.when(b == 0)
    def _():
        start_all(b, slot)

    @pl.when(b + 1 < pl.num_programs(0))
    def _():
        start_all(b + 1, 1 - slot)

    qt = qt_ref[0]
    qb = [jnp.broadcast_to(qt[:, h:h + 1], (HEAD_DIM, PAGE)) for h in range(N_HEADS)]
    rows_of = lambda h: slice((h // HEADS_PER_KV) * HEAD_DIM, (h // HEADS_PER_KV + 1) * HEAD_DIM)

    def page_s(tile, brow):
        rows = [jnp.sum(tile[rows_of(h), :] * qb[h], axis=0, keepdims=True) for h in range(N_HEADS)]
        return jnp.concatenate(rows, axis=0) + brow

    lax.fori_loop(0, n_pages, wait_k, 0)

    def pass1(jo, mx):
        for u in range(unroll):
            j = jo * unroll + u
            s = page_s(kbuf[slot, j], bias_ref[0, pl.ds(j, 1), :])
            sscr[j] = s
            mx = jnp.maximum(mx, s)
        return mx

    mx = lax.fori_loop(0, n_pages // unroll, pass1, jnp.full((N_HEADS, PAGE), NEG, jnp.float32))
    s_new = page_s(kn_ref[0], bias_ref[0, n_pages:n_pages + 1, :])
    m = jnp.max(jnp.maximum(mx, s_new), axis=1, keepdims=True)

    def pass2(jo, lacc):
        for u in range(unroll):
            j = jo * unroll + u
            p = jnp.exp(sscr[j] - m)
            sscr[j] = p
            lacc = lacc + p
        return lacc

    p_new = jnp.exp(s_new - m)
    lacc = lax.fori_loop(0, n_pages // unroll, pass2, p_new)
    sscr[n_pages] = p_new
    lax.fori_loop(0, n_pages, wait_v, 0)

    cols = []
    for h in range(N_HEADS):
        def pass3(jo, acc):
            for u in range(unroll):
                j = jo * unroll + u
                acc = acc + vbuf[slot, j, rows_of(h), :] * sscr[j, h:h + 1, :]
            return acc

        acc = lax.fori_loop(0, n_pages // unroll, pass3, vn_ref[0, rows_of(h), :] * p_new[h:h + 1, :])
        l = jnp.sum(lacc[h:h + 1, :], axis=1, keepdims=True)
        cols.append(jnp.sum(acc, axis=1, keepdims=True) / l)
    o_ref[0] = jnp.concatenate(cols, axis=1)


def _dsa_sample(h_att, wi, cache_kt, cache_vt, cache_kit, page_table, layer):
    nb = h_att.shape[0]
    n_pages = page_table.shape[1]
    n_keys = n_pages * PAGE + 1
    n_top = min(TOPK_MAX, n_keys // 4)
    lk = (n_pages + 1) * LANES
    q = h_att[:, :Q_W] * (HEAD_DIM ** -0.5)
    k_new = h_att[:, Q_W:Q_W + KV_W]
    v_new = h_att[:, Q_W + KV_W:Q_W + 2 * KV_W]
    qi = h_att[:, Q_W + 2 * KV_W:Q_W + 2 * KV_W + QI_W]
    ki_new = h_att[:, ATT_W - LANES:ATT_W - LANES + IDX_DIM]

    qit = jnp.pad(qi.reshape(nb, N_IDX_HEADS, IDX_DIM), ((0, 0), (0, SUBLANES - N_IDX_HEADS), (0, 0)))
    qit = qit.transpose(0, 2, 1)
    w8 = jnp.pad(wi, ((0, 0), (0, SUBLANES - N_IDX_HEADS)))[:, None, :]
    grid_spec = pltpu.PrefetchScalarGridSpec(
        num_scalar_prefetch=1, grid=(nb,),
        in_specs=[pl.BlockSpec((1, IDX_DIM, SUBLANES), lambda b, pt: (b, 0, 0)),
                  pl.BlockSpec((1, 1, SUBLANES), lambda b, pt: (b, 0, 0)),
                  pl.BlockSpec((1, IDX_DIM, 1), lambda b, pt: (b, 0, 0)),
                  pl.BlockSpec(memory_space=pl.ANY)],
        out_specs=pl.BlockSpec((1, n_pages + 1, LANES), lambda b, pt: (b, 0, 0)),
        scratch_shapes=[pltpu.VMEM((n_pages, IDX_DIM, PAGE), jnp.float32), pltpu.SemaphoreType.DMA])
    scores = pl.pallas_call(
        functools.partial(_dsa_s_score_kernel, layer=layer), grid_spec=grid_spec,
        out_shape=jax.ShapeDtypeStruct((nb, n_pages + 1, LANES), jnp.float32),
        compiler_params=_cparams(("arbitrary",)), name="dsa_sample_score")(
            page_table, qit, w8, ki_new[:, :, None], cache_kit)

    idx_bits = max(1, (lk + 1).bit_length())
    bias = pl.pallas_call(
        functools.partial(_dsa_s_select_kernel, n_top=n_top, n_keys=n_keys, idx_bits=idx_bits),
        grid=(1,), in_specs=[_const_spec((nb, lk))], out_specs=_const_spec((nb, lk)),
        out_shape=jax.ShapeDtypeStruct((nb, lk), jnp.float32),
        compiler_params=_cparams(("arbitrary",)), name="dsa_sample_select")(scores.reshape(nb, lk))

    qt = q.reshape(nb, N_HEADS, HEAD_DIM).transpose(0, 2, 1)
    first_col = ((0, 0), (0, 0), (0, PAGE - 1))
    grid_spec = pltpu.PrefetchScalarGridSpec(
        num_scalar_prefetch=1, grid=(nb,),
        in_specs=[pl.BlockSpec((1, HEAD_DIM, N_HEADS), lambda b, pt: (b, 0, 0)),
                  pl.BlockSpec((1, n_pages + 1, LANES), lambda b, pt: (b, 0, 0)),
                  pl.BlockSpec((1, KV_W, PAGE), lambda b, pt: (b, 0, 0)),
                  pl.BlockSpec((1, KV_W, PAGE), lambda b, pt: (b, 0, 0)),
                  pl.BlockSpec(memory_space=pl.ANY), pl.BlockSpec(memory_space=pl.ANY)],
        out_specs=pl.BlockSpec((1, HEAD_DIM, N_HEADS), lambda b, pt: (b, 0, 0)),
        scratch_shapes=[pltpu.VMEM((2, n_pages, KV_W, PAGE), jnp.float32),
                        pltpu.VMEM((2, n_pages, KV_W, PAGE), jnp.float32),
                        pltpu.VMEM((n_pages + 1, N_HEADS, PAGE), jnp.float32),
                        pltpu.SemaphoreType.DMA((2,)), pltpu.SemaphoreType.DMA((2,))])
    o_t = pl.pallas_call(
        functools.partial(_dsa_s_attn_kernel, layer=layer), grid_spec=grid_spec,
        out_shape=jax.ShapeDtypeStruct((nb, HEAD_DIM, N_HEADS), jnp.float32),
        compiler_params=_cparams(("arbitrary",)), name="dsa_sample_attn")(
            page_table, qt, bias.reshape(nb, n_pages + 1, LANES),
            jnp.pad(k_new[:, :, None], first_col), jnp.pad(v_new[:, :, None], first_col),
            cache_kt, cache_vt)
    return o_t.transpose(0, 2, 1).reshape(nb, Q_W)


def _post_kernel(m_ref, yd_ref, g3_ref, x_ref, wd_ref, wo_ref, lg_ref, lb_ref, o_ref, ob_ref, *, alpha, precision):
    pd = jnp.dot(yd_ref[...], wd_ref[...], preferred_element_type=jnp.float32, precision=precision)
    merged = m_ref[...] + g3_ref[...].astype(jnp.float32) * pd
    mix = jnp.dot(merged.astype(wo_ref.dtype), wo_ref[...], preferred_element_type=jnp.float32,
                  precision=precision)
    y = _ln(alpha * x_ref[...] + mix, lg_ref[...], lb_ref[...])
    o_ref[...] = y
    ob_ref[...] = y.astype(ob_ref.dtype)


def _post(merged_abc, yd, gsig, x, lp, alpha, tm, precise=False):
    n, d = x.shape
    row = lambda w, col=0: pl.BlockSpec((tm, w), lambda i, col=col: (i, col))
    consts = [lp["w_br_d"], lp["w_o"], lp["ln1_g"], lp["ln1_b"]]
    precision = lax.Precision.HIGHEST if precise else None
    return pl.pallas_call(
        functools.partial(_post_kernel, alpha=alpha, precision=precision), grid=(n // tm,),
        in_specs=[row(d), row(Q_W), row(d, N_BRANCHES - 1), row(d)] + [_const_spec(c.shape) for c in consts],
        out_specs=[row(d), row(d)],
        out_shape=[jax.ShapeDtypeStruct((n, d), jnp.float32), jax.ShapeDtypeStruct((n, d), jnp.bfloat16)],
        compiler_params=_cparams(("parallel",)), name="post")(merged_abc, yd, gsig, x, *consts)


def _ffn_kernel(xb_ref, x_ref, wg_ref, wu_ref, wd_ref, lg_ref, lb_ref, o_ref, ob_ref, acc, *, alpha, precision):
    f = pl.program_id(1)

    @pl.when(f == 0)
    def _():
        acc[...] = jnp.zeros_like(acc)

    xb = xb_ref[...]
    hg = jnp.dot(xb, wg_ref[...], preferred_element_type=jnp.float32, precision=precision)
    hu = jnp.dot(xb, wu_ref[...], preferred_element_type=jnp.float32, precision=precision)
    h = (hg * jax.nn.sigmoid(hg) * hu).astype(wd_ref.dtype)
    acc[...] += jnp.dot(h, wd_ref[...], preferred_element_type=jnp.float32, precision=precision)

    @pl.when(f == pl.num_programs(1) - 1)
    def _():
        y = _ln(alpha * x_ref[...] + acc[...], lg_ref[...], lb_ref[...])
        o_ref[...] = y
        ob_ref[...] = y.astype(ob_ref.dtype)


def _ffn(xb, x, wg, wu, wd, lg, lb, alpha, tm, precise=False):
    n, d = x.shape
    ff = wg.shape[1]
    tf = next(c for c in (512, 256, 128, ff) if ff % c == 0)
    row = pl.BlockSpec((tm, d), lambda i, f: (i, 0))
    precision = lax.Precision.HIGHEST if precise else None
    return pl.pallas_call(
        functools.partial(_ffn_kernel, alpha=alpha, precision=precision), grid=(n // tm, ff // tf),
        in_specs=[row, row, pl.BlockSpec((d, tf), lambda i, f: (0, f)), pl.BlockSpec((d, tf), lambda i, f: (0, f)),
                  pl.BlockSpec((tf, d), lambda i, f: (f, 0)), _const_spec(lg.shape), _const_spec(lb.shape)],
        out_specs=[row, row],
        out_shape=[jax.ShapeDtypeStruct((n, d), jnp.float32), jax.ShapeDtypeStruct((n, d), jnp.bfloat16)],
        scratch_shapes=[pltpu.VMEM((tm, d), jnp.float32)],
        compiler_params=_cparams(("parallel", "arbitrary")), name="ffn")(xb, x, wg, wu, wd, lg, lb)


def _router_kernel(x_ref, r_ref, g_ref):
    lane = lax.broadcasted_iota(jnp.int32, g_ref.shape, 1)
    lanef = lane.astype(jnp.float32)
    logits = jnp.dot(x_ref[...], r_ref[...], preferred_element_type=jnp.float32, precision=lax.Precision.HIGHEST)
    logits = jnp.where(lane < N_EXPERTS, logits, -jnp.inf)
    m1 = jnp.max(logits, axis=1, keepdims=True)
    i1 = jnp.min(jnp.where(logits == m1, lanef, float(LANES)), axis=1, keepdims=True)
    rest = jnp.where(lanef == i1, -jnp.inf, logits)
    m2 = jnp.max(rest, axis=1, keepdims=True)
    i2 = jnp.min(jnp.where(rest == m2, lanef, float(LANES)), axis=1, keepdims=True)
    e2 = jnp.exp(m2 - m1)
    g_ref[...] = jnp.where(lanef == i1, 1.0 / (1.0 + e2), jnp.where(lanef == i2, e2 / (1.0 + e2), 0.0))


def _moe_kernel(cnt_ref, xb_ref, g_ref, rank_ref, rankt_ref, wg_ref, wu_ref, wd_ref, o_ref, xc, yc, *, rc):
    i, e, f = pl.program_id(0), pl.program_id(1), pl.program_id(2)
    tm, d = xb_ref.shape
    nck = (cnt_ref[i, e] + rc - 1) // rc
    half = d // 2 if d % (2 * LANES) == 0 else d

    @pl.when((e == 0) & (f == 0))
    def _():
        o_ref[...] = jnp.zeros_like(o_ref)

    @pl.when(f == 0)
    def _():
        rrow = rankt_ref[pl.ds(e, 1), :]
        riota = lax.broadcasted_iota(jnp.int32, (rc, tm), 0).astype(jnp.float32)

        def compact(k, _):
            r0 = pl.multiple_of(k * rc, rc)
            sel = jnp.where(rrow - r0.astype(jnp.float32) == riota, 1.0, 0.0).astype(jnp.bfloat16)
            xc[pl.ds(r0, rc), :] = jnp.dot(sel, xb_ref[...], preferred_element_type=jnp.float32).astype(xc.dtype)
            yc[pl.ds(r0, rc), :] = jnp.zeros((rc, d), jnp.float32)
            return 0

        lax.fori_loop(0, nck, compact, 0)

    def expert(k, _):
        r0 = pl.multiple_of(k * rc, rc)
        xk = xc[pl.ds(r0, rc), :]
        hg = jnp.dot(xk, wg_ref[0], preferred_element_type=jnp.float32)
        hu = jnp.dot(xk, wu_ref[0], preferred_element_type=jnp.float32)
        h = (hg * jax.nn.sigmoid(hg) * hu).astype(jnp.bfloat16)
        yc[pl.ds(r0, rc), :] += jnp.dot(h, wd_ref[0], preferred_element_type=jnp.float32)
        return 0

    lax.fori_loop(0, nck, expert, 0)

    @pl.when(f == pl.num_programs(2) - 1)
    def _():
        lane = lax.broadcasted_iota(jnp.int32, g_ref.shape, 1)
        rcol = jnp.sum(jnp.where(lane == e, rank_ref[...], 0.0), axis=1, keepdims=True)
        gcol = jnp.sum(jnp.where(lane == e, g_ref[...], 0.0), axis=1, keepdims=True)
        ciota = lax.broadcasted_iota(jnp.int32, (tm, rc), 1).astype(jnp.float32)

        def scatter(k, _):
            r0 = pl.multiple_of(k * rc, rc)
            selt = jnp.where(rcol - r0.astype(jnp.float32) == ciota, 1.0, 0.0).astype(jnp.bfloat16)
            for c0 in range(0, d, half):
                yk = yc[pl.ds(r0, rc), c0:c0 + half].astype(jnp.bfloat16)
                o_ref[:, c0:c0 + half] += gcol * jnp.dot(selt, yk, preferred_element_type=jnp.float32)
            return 0

        lax.fori_loop(0, nck, scatter, 0)


def _add_ln_kernel(x_ref, y_ref, lg_ref, lb_ref, o_ref, ob_ref, *, alpha):
    y = _ln(alpha * x_ref[...] + y_ref[...], lg_ref[...], lb_ref[...])
    o_ref[...] = y
    ob_ref[...] = y.astype(ob_ref.dtype)


def _moe(xb, x, router_p, wg, wu, wd, lg, lb, alpha):
    n, d = x.shape
    ne, _, ff = wg.shape
    tr = _tile(n, 1024)
    gate = pl.pallas_call(
        _router_kernel, grid=(n // tr,),
        in_specs=[pl.BlockSpec((tr, d), lambda i: (i, 0)), _const_spec(router_p.shape)],
        out_specs=pl.BlockSpec((tr, LANES), lambda i: (i, 0)),
        out_shape=jax.ShapeDtypeStruct((n, LANES), jnp.float32),
        compiler_params=_cparams(("parallel",)), name="moe_router")(x, router_p)

    tm = _tile(n, 2048)
    rc = -(-(tm * TOP_K * 9) // (ne * 8 * 2 * 16)) * 16
    n_rows = -(-tm // rc) * rc
    nt = n // tm
    routed = gate > 0.0
    rank = jnp.cumsum(routed.reshape(nt, tm, LANES).astype(jnp.int32), axis=1) - 1
    rank = jnp.where(routed.reshape(nt, tm, LANES), rank, -1).astype(jnp.float32)
    cnt = jnp.sum(routed.reshape(nt, tm, LANES)[:, :, :ne], axis=1).astype(jnp.int32)
    rank_t = rank[:, :, :SUBLANES].transpose(0, 2, 1).reshape(nt * SUBLANES, tm)
    rank = rank.reshape(n, LANES)

    tf = next(c for c in (512, 256, 128, ff) if ff % c == 0)
    row = lambda w: pl.BlockSpec((tm, w), lambda i, e, f, c: (i, 0))
    grid_spec = pltpu.PrefetchScalarGridSpec(
        num_scalar_prefetch=1, grid=(nt, ne, ff // tf),
        in_specs=[row(d), row(LANES), row(LANES),
                  pl.BlockSpec((SUBLANES, tm), lambda i, e, f, c: (i, 0)),
                  pl.BlockSpec((1, d, tf), lambda i, e, f, c: (e, 0, f)),
                  pl.BlockSpec((1, d, tf), lambda i, e, f, c: (e, 0, f)),
                  pl.BlockSpec((1, tf, d), lambda i, e, f, c: (e, f, 0))],
        out_specs=row(d),
        scratch_shapes=[pltpu.VMEM((n_rows, d), jnp.bfloat16), pltpu.VMEM((n_rows, d), jnp.float32)])
    y = pl.pallas_call(
        functools.partial(_moe_kernel, rc=rc), grid_spec=grid_spec,
        out_shape=jax.ShapeDtypeStruct((n, d), jnp.float32),
        compiler_params=_cparams(("arbitrary", "arbitrary", "arbitrary")), name="moe")(
            cnt, xb, gate, rank, rank_t, wg, wu, wd)

    ta = _tile(n, 1024)
    rowa = pl.BlockSpec((ta, d), lambda i: (i, 0))
    return pl.pallas_call(
        functools.partial(_add_ln_kernel, alpha=alpha), grid=(n // ta,),
        in_specs=[rowa, rowa, _const_spec(lg.shape), _const_spec(lb.shape)], out_specs=[rowa, rowa],
        out_shape=[jax.ShapeDtypeStruct((n, d), jnp.float32), jax.ShapeDtypeStruct((n, d), jnp.bfloat16)],
        compiler_params=_cparams(("parallel",)), name="add_ln")(x, y, lg, lb)


def _rope_tables(pos):
    rot = HEAD_DIM // ROPE_FRAC
    half = rot // 2
    freqs = jnp.power(ROPE_THETA, -jnp.arange(half, dtype=jnp.float32) / half)
    ang = pos.astype(jnp.float32)[:, None] * freqs
    cos, sin = jnp.cos(ang), jnp.sin(ang)
    t = pos.shape[0]
    ones = jnp.ones((t, HEAD_DIM - rot), jnp.float32)
    zeros = jnp.zeros((t, HEAD_DIM - rot), jnp.float32)
    zh = jnp.zeros((t, half), jnp.float32)
    c = jnp.concatenate([cos, cos, ones], axis=1)
    a = jnp.concatenate([-sin, zh, zeros], axis=1)
    b = jnp.concatenate([zh, sin, zeros], axis=1)
    rep = LANES // HEAD_DIM
    return jnp.tile(c, (1, rep)), jnp.tile(a, (1, rep)), jnp.tile(b, (1, rep))


def _layer_params(l, w_in_t, gmlp_ln_g, gmlp_ln_b, gmlp_ws, gmlp_bs, conv_b_w, conv_c_w, conv_c_bias, conf_ln_g,
                  conf_ln_b, w_br_a, w_br_b, w_br_c, w_br_d, w_o, ln1_g, ln1_b, ln2_g, ln2_b, dtype):
    bf = dtype
    w = w_in_t[:, l, :].astype(dtype)
    att_end = MIX_W + Q_W + 2 * KV_W + QI_W + IDX_DIM + N_IDX_HEADS
    w_att = jnp.pad(w[MIX_W:att_end], ((0, ATT_W - (att_end - MIX_W)), (0, 0)))
    row = lambda a: a[l][None, :]
    gw = D_A // G_A
    return {
        "w_mix": w[:MIX_W], "w_att": w_att, "w_gat": w[att_end:],
        "gmlp_ln_g": row(gmlp_ln_g), "gmlp_ln_b": row(gmlp_ln_b), "gmlp_ws": gmlp_ws[l],
        "gmlp_bias_full": jnp.repeat(gmlp_bs[l].T, gw, axis=1),
        "gmlp_ws0": jnp.repeat(gmlp_ws[l][:, 0, 0], gw)[None, :],
        "gmlp_bs0": jnp.repeat(gmlp_bs[l][:, 0], gw)[None, :],
        "conv_b_w": conv_b_w[l], "conv_c_w": conv_c_w[l], "conv_c_bias": row(conv_c_bias),
        "conf_ln_g": row(conf_ln_g), "conf_ln_b": row(conf_ln_b),
        "w_br_a": w_br_a[l].astype(bf), "w_br_b": w_br_b[l].astype(bf), "w_br_c": w_br_c[l].astype(bf),
        "w_br_d": w_br_d[l].astype(bf), "w_o": w_o[l].astype(bf),
        "ln1_g": row(ln1_g), "ln1_b": row(ln1_b), "ln2_g": row(ln2_g), "ln2_b": row(ln2_b),
    }


def _channel_mixer(l, xb, x, lp, ffn_w, moe_w, alpha, tm, precise=False):
    j = l // 2
    if l % 2 == 0:
        wg, wu, wd = ffn_w
        return _ffn(x if precise else xb, x, wg[j], wu[j], wd[j], lp["ln2_g"], lp["ln2_b"], alpha, tm, precise)
    router, wg, wu, wd = moe_w
    return _moe(xb, x, router[j], wg[j], wu[j], wd[j], lp["ln2_g"], lp["ln2_b"], alpha)


def kernel(x_prompt, x_sample, cache_k, cache_v, cache_idx_k, state_conv_b, state_conv_c, page_table,
           w_in, gmlp_ln_g, gmlp_ln_b, gmlp_ws, gmlp_bs, conv_b_w, conv_c_w, conv_c_bias, conf_ln_g, conf_ln_b,
           w_br_a, w_br_b, w_br_c, w_br_d, w_o, ln1_g, ln1_b, ln2_g, ln2_b,
           ffn_w_gate, ffn_w_up, ffn_w_down, moe_router, moe_w_gate, moe_w_up, moe_w_down):
    bf = jnp.bfloat16
    nb, t, d = x_prompt.shape
    ns, ts, _ = x_sample.shape
    assert ts == 1 and t % Q_BLOCK == 0
    depth = w_in.shape[0]
    alpha = float((2 * depth) ** 0.25)
    past = page_table.shape[1] * PAGE

    ffn_w = (ffn_w_gate.astype(bf), ffn_w_up.astype(bf), ffn_w_down.astype(bf))
    router_p = jnp.pad(moe_router, ((0, 0), (0, 0), (0, LANES - N_EXPERTS)))
    moe_w = (router_p, moe_w_gate.astype(bf), moe_w_up.astype(bf), moe_w_down.astype(bf))

    rope_p = _rope_tables(jnp.arange(t, dtype=jnp.int32))
    rope_s = tuple(jnp.tile(r, (ns, 1)) for r in _rope_tables(past + jnp.arange(1, dtype=jnp.int32)))
    n_pool = cache_k.shape[1]
    cache_kt = cache_k.transpose(0, 1, 3, 4, 2).reshape(depth, n_pool, KV_W, PAGE)
    cache_vt = cache_v.transpose(0, 1, 3, 4, 2).reshape(depth, n_pool, KV_W, PAGE)
    cache_kit = cache_idx_k.transpose(0, 1, 3, 2)

    n = nb * t
    tm_p = _tile(t, 512)
    tm_f = _tile(n, 1024)
    xp = x_prompt.reshape(n, d)
    xs = x_sample.reshape(ns, d)
    xp_b, xs_b = xp.astype(bf), xs.astype(bf)
    outs = {k: [] for k in ("kp", "vp", "kip", "cbp", "ccp", "ks", "vs", "kis", "cbs", "ccs", "gvs")}
    w_in_t = w_in.transpose(2, 0, 1)
    w_in_tb = w_in_t.astype(bf)
    ffn_w_f = (ffn_w_gate, ffn_w_up, ffn_w_down)
    layer_tensors = (gmlp_ln_g, gmlp_ln_b, gmlp_ws, gmlp_bs, conv_b_w, conv_c_w, conv_c_bias, conf_ln_g, conf_ln_b,
                     w_br_a, w_br_b, w_br_c, w_br_d, w_o, ln1_g, ln1_b, ln2_g, ln2_b)
    for l in range(depth):
        lp = _layer_params(l, w_in_tb, *layer_tensors, bf)
        lp_s = _layer_params(l, w_in_t, *layer_tensors, jnp.float32)
        mix, gsig, h_att, q_b, kv_b, qi_b, ki_b = _in_projection(
            xp_b, lp["w_mix"], lp["w_att"], lp["w_gat"], *rope_p, t // tm_p, tm_p)
        merged_abc, cb, cc = _mix_prompt(mix, gsig, lp, nb, t, tm_p)
        wi = h_att[:, ATT_W - LANES + IDX_DIM:ATT_W - LANES + IDX_DIM + N_IDX_HEADS]
        y_d = _dsa_prompt(q_b, kv_b, qi_b, ki_b, wi, nb, t)
        x1, x1_b = _post(merged_abc, y_d, gsig, xp, lp, alpha, tm_p)
        xp, xp_b = _channel_mixer(l, x1_b, x1, lp, ffn_w, moe_w, alpha, tm_f)
        outs["kp"].append(h_att[:, Q_W:Q_W + KV_W].reshape(nb, t, N_KV, HEAD_DIM))
        outs["vp"].append(h_att[:, Q_W + KV_W:Q_W + 2 * KV_W].reshape(nb, t, N_KV, HEAD_DIM))
        outs["kip"].append(h_att[:, ATT_W - LANES:ATT_W - LANES + IDX_DIM].reshape(nb, t, IDX_DIM))
        outs["cbp"].append(cb)
        outs["ccp"].append(cc)
        mix, gsig, h_att, _, _, _, _ = _in_projection(
            xs, lp_s["w_mix"], lp_s["w_att"], lp_s["w_gat"], *rope_s, 1, ns, precise=True)
        merged_abc, v_rows, zb, hc = _mix_sample(mix, gsig, state_conv_b[l], state_conv_c[l], lp_s)
        wi = h_att[:, ATT_W - LANES + IDX_DIM:ATT_W - LANES + IDX_DIM + N_IDX_HEADS]
        y_d = _dsa_sample(h_att, wi, cache_kt, cache_vt, cache_kit, page_table, l)
        x1, x1_b = _post(merged_abc, y_d, gsig, xs, lp_s, alpha, ns, precise=True)
        xs, _ = _channel_mixer(l, x1_b, x1, lp_s, ffn_w_f, moe_w, alpha, ns, precise=True)
        outs["ks"].append(h_att[:, Q_W:Q_W + KV_W].reshape(ns, 1, N_KV, HEAD_DIM))
        outs["vs"].append(h_att[:, Q_W + KV_W:Q_W + 2 * KV_W].reshape(ns, 1, N_KV, HEAD_DIM))
        outs["kis"].append(h_att[:, ATT_W - LANES:ATT_W - LANES + IDX_DIM].reshape(ns, 1, IDX_DIM))
        outs["cbs"].append(jnp.concatenate([state_conv_b[l][:, 1:], zb[:, None, :]], axis=1))
        outs["ccs"].append(jnp.concatenate([state_conv_c[l][:, 1:], hc[:, None, :]], axis=1))
        outs["gvs"].append(v_rows[:, None, :])
    st = lambda k: jnp.stack(outs[k])
    return (xp.reshape(nb, t, d), xs.reshape(ns, 1, d), st("kp"), st("vp"), st("kip"), st("cbp"), st("ccp"),
            st("ks"), st("vs"), st("kis"), st("cbs"), st("ccs"), st("gvs"))
```
